```python
import math
import jax, jax.numpy as jnp
from jax import lax
import numpy as np

D_MODEL = 1024
BATCH = 4
SEQ = 8192
DEPTH = 4
DEC_BATCH = 32
DEC_SEQ = 16
PAST_LEN = 4096

CHUNK = 64
HEAD_DIM = 64
D_MIX = D_MODEL
A_HEADS = 4
A_DK = HEAD_DIM
A_DV = HEAD_DIM
A_W = A_HEADS * A_DV
CONV_W = 4
B_HEADS = 4
B_DK = HEAD_DIM
B_DV = 2 * HEAD_DIM
B_W = B_HEADS * B_DV
Q_BLOCK = 128
C_HEADS = 4
C_DK = HEAD_DIM
C_DV = HEAD_DIM
C_W = C_HEADS * C_DV
D_FF = 2816
LN_EPS = 1e-5
RMS_EPS = 1e-6
NEG_BIG = -1e30

A_QK0 = 0
A_V0 = A_QK0 + 2 * A_W
A_I0 = A_V0 + A_W
A_F0 = A_I0 + A_HEADS
A_O0 = A_F0 + A_HEADS
B_Q0 = A_O0 + A_W
B_K0 = B_Q0 + B_HEADS * 2 * B_DK
B_V0 = B_K0 + B_HEADS * 2 * B_DK
C_Q0 = B_V0 + B_W
C_F0 = C_Q0 + C_HEADS * C_DK
C_I0 = C_F0 + C_HEADS * C_DK
C_G0 = C_I0 + C_W
N_IN = C_G0 + C_W

kernel_name = "hybrid_mlstm_diffattn_hgrn2_streaming_step"


def layer_norm(x, g, b):
    xf = x.astype(jnp.float32)
    mu = jnp.mean(xf, axis=-1, keepdims=True)
    var = jnp.mean(jnp.square(xf - mu), axis=-1, keepdims=True)
    return ((xf - mu) * lax.rsqrt(var + LN_EPS) * g + b).astype(x.dtype)


def rms_norm(x, g):
    xf = x.astype(jnp.float32)
    return (xf * lax.rsqrt(jnp.mean(xf * xf, axis=-1, keepdims=True) + RMS_EPS) * g).astype(x.dtype)


def swiglu(x, w_in, w_out):
    gate, up = jnp.split(x @ w_in, 2, axis=-1)
    return (jax.nn.silu(gate) * up) @ w_out


def causal_conv(u, buf, w, b):
    T = u.shape[1]
    up = jnp.concatenate([buf.astype(u.dtype), u], axis=1)
    y = b + sum(w[j] * up[:, j:j + T] for j in range(CONV_W))
    return jax.nn.silu(y), up[:, -(CONV_W - 1):]


def to_chunks(a, L):
    B, T = a.shape[:2]
    return jnp.moveaxis(a.reshape((B, T // L, L) + a.shape[2:]), 1, 0)


def from_chunks(a):
    NC, B, L = a.shape[:3]
    return jnp.moveaxis(a, 0, 1).reshape((B, NC * L) + a.shape[3:])


def mlstm_scan(q, k, v, i_log, f_log, c0, n0, m0):
    f32 = jnp.float32
    dt = q.dtype
    T = q.shape[1]
    L = min(CHUNK, T)
    causal = jnp.tril(jnp.ones((L, L), dtype=bool))
    scale = A_DK ** -0.5

    def step(carry, inp):
        c, n, m = carry
        qc, kc, vc, ic, fc = inp
        b = jnp.cumsum(fc, axis=1)
        dmat = b[:, :, None] - b[:, None] + ic[:, None]
        dmat = jnp.where(causal[None, :, :, None], dmat, NEG_BIG)
        g = b + m[:, None]
        mrow = jnp.maximum(g, jnp.max(dmat, axis=2))
        wd = jnp.exp(dmat - mrow[:, :, None])
        wg = jnp.exp(g - mrow)
        qs = qc * scale
        qk = jnp.einsum('bthd,bshd->btsh', qs, kc) * wd
        num = wg[..., None] * jnp.einsum('bthk,bhkv->bthv', qs, c) + jnp.einsum('btsh,bshv->bthv', qk, vc)
        den = wg * jnp.einsum('bthk,bhk->bth', qs, n) + jnp.sum(qk, axis=2)
        h = num / jnp.maximum(jnp.abs(den), jnp.exp(-mrow))[..., None]
        m_new = mrow[:, -1]
        decay = jnp.exp(b[:, -1] + m - m_new)
        ws = jnp.exp(b[:, -1:] - b + ic - m_new[:, None])
        c_new = decay[..., None, None] * c + jnp.einsum('bsh,bshk,bshv->bhkv', ws, kc, vc)
        n_new = decay[..., None] * n + jnp.einsum('bsh,bshk->bhk', ws, kc)
        return (c_new, n_new, m_new), h

    xs = tuple(to_chunks(a.astype(f32), L) for a in (q, k, v, i_log, f_log))
    init = (c0.astype(f32), n0.astype(f32), m0.astype(f32))
    (c, n, m), h = lax.scan(step, init, xs)
    return from_chunks(h).astype(dt), c.astype(dt), n.astype(dt), m.astype(dt)


def hgrn2_scan(q, key, log_f, v, s0):
    f32 = jnp.float32
    dt = q.dtype
    T = q.shape[1]
    L = min(CHUNK, T)
    causal = jnp.tril(jnp.ones((L, L), dtype=bool))

    def step(s, inp):
        qc, kc, fc, vc = inp
        b = jnp.cumsum(fc, axis=1)
        diff = b[:, :, None] - b[:, None]
        decay = jnp.exp(jnp.where(causal[None, :, :, None, None], diff, NEG_BIG))
        att = jnp.einsum('bthk,btshk,bshk->btsh', qc, decay, kc)
        o = jnp.einsum('bthk,bhkv->bthv', qc * jnp.exp(b), s) + jnp.einsum('btsh,bshv->bthv', att, vc)
        s_new = jnp.exp(b[:, -1])[..., None] * s + jnp.einsum('bshk,bshv->bhkv', kc * jnp.exp(b[:, -1:] - b), vc)
        return s_new, o

    xs = tuple(to_chunks(a.astype(f32), L) for a in (q, key, log_f, v))
    s, o = lax.scan(step, s0.astype(f32), xs)
    return from_chunks(o).astype(dt), s.astype(dt)


def diff_attn_core(q, k, v, mask, lam):
    s = jnp.einsum('bqhcd,bkhcd->bhcqk', q, k).astype(jnp.float32) * (B_DK ** -0.5)
    if mask is not None:
        s = jnp.where(mask, s, NEG_BIG)
    p = jax.nn.softmax(s, axis=-1)
    a = p[:, :, 0] - lam * p[:, :, 1]
    return jnp.einsum('bhqk,bkhe->bqhe', a.astype(v.dtype), v)


def diff_attn_prompt(q, k, v, lam):
    B, T = q.shape[:2]
    nqb = T // Q_BLOCK
    qb = jnp.moveaxis(q.reshape(B, nqb, Q_BLOCK, B_HEADS, 2, B_DK), 1, 0)
    k_chunk = jnp.arange(T) // CHUNK

    def block(args):
        q_blk, i = args
        q_chunk = (i * Q_BLOCK + jnp.arange(Q_BLOCK)) // CHUNK
        mask = k_chunk[None, :] <= q_chunk[:, None]
        return diff_attn_core(q_blk, k, v, mask, lam)

    o = lax.map(block, (qb, jnp.arange(nqb)))
    return jnp.moveaxis(o, 0, 1).reshape(B, T, B_HEADS, B_DV)


def token_mix(l, x, past_k, past_v, conv_buf, c0, n0, m0, s0,
              w_in, b_in, w_out, conv_w, conv_b, lam_p, dn_g, lb, hn_g):
    f32 = jnp.float32
    B, T, _ = x.shape
    p = x @ w_in + b_in
    qk, conv_new = causal_conv(p[..., A_QK0:A_V0], conv_buf, conv_w, conv_b)
    aq = qk[..., :A_W].reshape(B, T, A_HEADS, A_DK)
    ak = qk[..., A_W:].reshape(B, T, A_HEADS, A_DK)
    av = p[..., A_V0:A_I0].reshape(B, T, A_HEADS, A_DV)
    i_log = p[..., A_I0:A_F0].astype(f32)
    f_log = jax.nn.log_sigmoid(p[..., A_F0:A_O0].astype(f32))
    h, c1, n1, m1 = mlstm_scan(aq, ak, av, i_log, f_log, c0, n0, m0)
    ya = jax.nn.sigmoid(p[..., A_O0:B_Q0]) * h.reshape(B, T, A_W)
    bq = p[..., B_Q0:B_K0].reshape(B, T, B_HEADS, 2, B_DK)
    k_rows = p[..., B_K0:B_V0].reshape(B, T, B_HEADS, 2 * B_DK)
    v_rows = p[..., B_V0:C_Q0].reshape(B, T, B_HEADS, B_DV)
    lam_init = 0.8 - 0.6 * math.exp(-0.3 * l)
    lp = lam_p.astype(f32)
    lam = jnp.exp(jnp.sum(lp[0] * lp[1])) - jnp.exp(jnp.sum(lp[2] * lp[3])) + lam_init
    if past_k is None:
        o = diff_attn_prompt(bq, k_rows.reshape(B, T, B_HEADS, 2, B_DK), v_rows, lam)
    else:
        k_all = jnp.concatenate([past_k, k_rows], axis=1)
        v_all = jnp.concatenate([past_v, v_rows], axis=1)
        o = diff_attn_core(bq, k_all.reshape(B, -1, B_HEADS, 2, B_DK), v_all, None, lam)
    yb = (rms_norm(o, dn_g) * (1.0 - lam_init)).reshape(B, T, B_W)
    cq = p[..., C_Q0:C_F0].reshape(B, T, C_HEADS, C_DK)
    f_pre = p[..., C_F0:C_I0].astype(f32)
    forget = lb + (1.0 - lb) * jax.nn.sigmoid(f_pre)
    log_forget = jnp.log(forget)
    key = 1.0 - forget
    ci = p[..., C_I0:C_G0].reshape(B, T, C_HEADS, C_DV)
    oc, s1 = hgrn2_scan(cq, key.reshape(B, T, C_HEADS, C_DK), log_forget.reshape(B, T, C_HEADS, C_DK), ci, s0)
    yc = rms_norm(oc, hn_g).reshape(B, T, C_W) * jax.nn.silu(p[..., C_G0:N_IN])
    y = jnp.concatenate([ya, yb, yc], axis=-1) @ w_out
    return y, (k_rows, v_rows, c1, n1, m1, conv_new, s1)


def encoder_layer(l, x, st, wts, alpha):
    past_k, past_v, conv_buf, c0, n0, m0, s0 = st
    (w_in, b_in, w_out, conv_w, conv_b, lam_p, dn_g, lb, hn_g, f_in, f_out, ln_g, ln_b) = wts
    x = layer_norm(alpha * x + 0.5 * swiglu(x, f_in[0], f_out[0]), ln_g[0], ln_b[0])
    y, new_st = token_mix(l, x, past_k, past_v, conv_buf, c0, n0, m0, s0,
                          w_in, b_in, w_out, conv_w, conv_b, lam_p, dn_g, lb, hn_g)
    x = layer_norm(alpha * x + y, ln_g[1], ln_b[1])
    x = layer_norm(alpha * x + 0.5 * swiglu(x, f_in[1], f_out[1]), ln_g[2], ln_b[2])
    return x, new_st


def setup_inputs(seed: int = 0) -> dict:
    key = jax.random.key(seed)
    ks = jax.random.split(key, 24)
    f32 = jnp.float32
    nrm = lambda k, shape, s: s * jax.random.normal(k, shape, f32)
    beta = (8.0 * DEPTH) ** -0.25
    b_in = nrm(ks[10], (DEPTH, N_IN), 0.02)
    b_in = b_in.at[:, A_F0:A_O0].add(jnp.linspace(3.0, 6.0, A_HEADS))
    return {
        "x_prompt": nrm(ks[0], (BATCH, SEQ, D_MODEL), 1.0),
        "x_sample": nrm(ks[1], (DEC_BATCH, DEC_SEQ, D_MODEL), 1.0),
        "cache_diff_k": nrm(ks[2], (DEPTH, DEC_BATCH, PAST_LEN, B_HEADS, 2 * B_DK), 1.0),
        "cache_diff_v": nrm(ks[3], (DEPTH, DEC_BATCH, PAST_LEN, B_HEADS, B_DV), 1.0),
        "state_mlstm_c": nrm(ks[4], (DEPTH, DEC_BATCH, A_HEADS, A_DK, A_DV), 0.3),
        "state_mlstm_n": jnp.abs(nrm(ks[5], (DEPTH, DEC_BATCH, A_HEADS, A_DK), 1.0)),
        "state_mlstm_m": nrm(ks[6], (DEPTH, DEC_BATCH, A_HEADS), 1.0),
        "state_mlstm_conv": nrm(ks[7], (DEPTH, DEC_BATCH, CONV_W - 1, 2 * A_W), 1.0),
        "state_hgrn_s": nrm(ks[8], (DEPTH, DEC_BATCH, C_HEADS, C_DK, C_DV), 0.5),
        "w_in": nrm(ks[9], (DEPTH, D_MODEL, N_IN), D_MODEL ** -0.5),
        "b_in": b_in,
        "w_out": nrm(ks[11], (DEPTH, D_MIX, D_MODEL), beta * D_MIX ** -0.5),
        "mlstm_conv_w": nrm(ks[12], (DEPTH, CONV_W, 2 * A_W), CONV_W ** -0.5),
        "mlstm_conv_b": nrm(ks[13], (DEPTH, 2 * A_W), 0.02),
        "diff_lambda": nrm(ks[14], (DEPTH, 4, B_DK), 0.1),
        "diff_norm_g": 1.0 + nrm(ks[15], (DEPTH, B_DV), 0.02),
        "hgrn_lb_logits": nrm(ks[16], (DEPTH, C_W), 0.5),
        "hgrn_norm_g": 1.0 + nrm(ks[17], (DEPTH, C_DV), 0.02),
        "ffn_w_in": nrm(ks[18], (DEPTH, 2, D_MODEL, 2 * D_FF), D_MODEL ** -0.5),
        "ffn_w_out": nrm(ks[19], (DEPTH, 2, D_FF, D_MODEL), beta * D_FF ** -0.5),
        "ln_g": 1.0 + nrm(ks[20], (DEPTH, 3, D_MODEL), 0.02),
        "ln_b": nrm(ks[21], (DEPTH, 3, D_MODEL), 0.02),
    }


def reference(x_prompt, x_sample, cache_diff_k, cache_diff_v, state_mlstm_c, state_mlstm_n, state_mlstm_m,
              state_mlstm_conv, state_hgrn_s, w_in, b_in, w_out, mlstm_conv_w, mlstm_conv_b, diff_lambda,
              diff_norm_g, hgrn_lb_logits, hgrn_norm_g, ffn_w_in, ffn_w_out, ln_g, ln_b):
    alpha = (2.0 * DEPTH) ** 0.25
    p_lb = jax.nn.softmax(hgrn_lb_logits.astype(jnp.float32), axis=0)
    lower_bounds = jnp.cumsum(p_lb, axis=0) - p_lb[0]

    def wts(l):
        return (w_in[l], b_in[l], w_out[l], mlstm_conv_w[l], mlstm_conv_b[l], diff_lambda[l],
                diff_norm_g[l], lower_bounds[l], hgrn_norm_g[l], ffn_w_in[l], ffn_w_out[l], ln_g[l], ln_b[l])

    x = x_prompt
    B = x.shape[0]
    dt = x.dtype
    p_out = []
    for l in range(DEPTH):
        st = (None, None,
              jnp.zeros((B, CONV_W - 1, 2 * A_W), dt),
              jnp.zeros((B, A_HEADS, A_DK, A_DV), dt),
              jnp.zeros((B, A_HEADS, A_DK), dt),
              jnp.zeros((B, A_HEADS), dt),
              jnp.zeros((B, C_HEADS, C_DK, C_DV), dt))
        x, ns = encoder_layer(l, x, st, wts(l), alpha)
        p_out.append(ns)
    y_prompt = x
    p_k, p_v, p_c, p_n, p_m, p_conv, p_s = (jnp.stack(a) for a in zip(*p_out))

    x = x_sample
    s_out = []
    for l in range(DEPTH):
        st = (cache_diff_k[l], cache_diff_v[l], state_mlstm_conv[l], state_mlstm_c[l],
              state_mlstm_n[l], state_mlstm_m[l], state_hgrn_s[l])
        x, ns = encoder_layer(l, x, st, wts(l), alpha)
        s_out.append(ns)
    y_sample = x
    s_k, s_v, s_c, s_n, s_m, s_conv, s_s = (jnp.stack(a) for a in zip(*s_out))

    return (y_prompt, y_sample, p_k, p_v, p_c, p_n, p_m, p_conv, p_s,
            s_k, s_v, s_c, s_n, s_m, s_conv, s_s)
```

```python
import functools
import math

import jax
import jax.numpy as jnp
from jax import lax
from jax.experimental import pallas as pl
from jax.experimental.pallas import tpu as pltpu

F32 = jnp.float32
BF16 = jnp.bfloat16

HEAD_DIM = 64
N_HEADS = 4
CONV_W = 4
CHUNK = 64
HB_W = N_HEADS * HEAD_DIM
ATT_W = N_HEADS * 2 * HEAD_DIM
LN_EPS = 1e-5
RMS_EPS = 1e-6
NEG_BIG = -1e30
SCAN_ROWS = 128
SUB = 16
V7X_VMEM_LIMIT = 56 * 1024 * 1024

A_QK0 = 0
A_V0 = A_QK0 + 2 * HB_W
A_I0 = A_V0 + HB_W
A_F0 = A_I0 + N_HEADS
A_O0 = A_F0 + N_HEADS
B_Q0 = A_O0 + HB_W
B_K0 = B_Q0 + ATT_W
B_V0 = B_K0 + ATT_W
C_Q0 = B_V0 + ATT_W
N_IN = C_Q0 + 4 * HB_W

_SEGS = {}
_off = 0
for _name, _w in (("aqk", 2 * HB_W), ("av", HB_W), ("ao", HB_W), ("gi", HB_W), ("gf", HB_W),
                  ("bq", ATT_W), ("bk", ATT_W), ("bv", ATT_W), ("c", 4 * HB_W)):
    _SEGS[_name] = (_off, _w)
    _off += _w
N_CAT = _off


def _tile(n, pref, mult=8):
    t = min(pref, n)
    while t > mult and (n % t or t % mult):
        t -= mult
    assert n % t == 0 and t % mult == 0, (n, pref, mult)
    return t


def _cparams(sem):
    return pltpu.CompilerParams(dimension_semantics=sem, vmem_limit_bytes=V7X_VMEM_LIMIT)


def _const_spec(shape):
    nd = len(shape)
    return pl.BlockSpec(shape, lambda *_: (0,) * nd)


def _dot(a, b):
    return jnp.dot(a, b, preferred_element_type=F32)


def _dot_nt(a, b):
    return lax.dot_general(a, b, (((1,), (1,)), ((), ())), preferred_element_type=F32)


def _split3(x):
    a = x.astype(BF16)
    r = x - a.astype(F32)
    b = r.astype(BF16)
    c = (r - b.astype(F32)).astype(BF16)
    return a, b, c


def _dot_exact_left(mat01, x):
    a, b, c = _split3(x)
    return _dot(mat01, a) + _dot(mat01, b) + _dot(mat01, c)


def _dot_exact_right(x, mat01):
    a, b, c = _split3(x)
    return _dot(a, mat01) + _dot(b, mat01) + _dot(c, mat01)


def _layer_norm(y, g, b):
    mu = jnp.mean(y, axis=-1, keepdims=True)
    d = y - mu
    var = jnp.mean(d * d, axis=-1, keepdims=True)
    return d * lax.rsqrt(var + LN_EPS) * g + b


def _pad_rows(x, rows):
    if x.shape[0] == rows:
        return x
    return jnp.concatenate([x, jnp.zeros((rows - x.shape[0],) + x.shape[1:], x.dtype)], axis=0)


def _ffn_ln_body(x_ref, wi_ref, wo_ref, g_ref, b_ref, o_ref, acc_ref, *, alpha, d_ff, tf):
    x = x_ref[...]
    xb = x.astype(BF16)
    for c in range(d_ff // tf):
        gate = _dot(xb, wi_ref[:, c * tf:(c + 1) * tf])
        up = _dot(xb, wi_ref[:, d_ff + c * tf:d_ff + (c + 1) * tf])
        h = (gate * jax.nn.sigmoid(gate) * up).astype(BF16)
        part = _dot(h, wo_ref[c * tf:(c + 1) * tf, :])
        if c == 0:
            acc_ref[...] = part
        else:
            acc_ref[...] += part
    y = alpha * x + 0.5 * acc_ref[...]
    o_ref[...] = _layer_norm(y, g_ref[...], b_ref[...])


def _ffn_ln(x, wi, wo, g, b, alpha):
    n, d = x.shape
    d_ff = wo.shape[0]
    tm = _tile(n, 512)
    tf = _tile(d_ff, 256, 128)
    return pl.pallas_call(
        functools.partial(_ffn_ln_body, alpha=alpha, d_ff=d_ff, tf=tf),
        grid=(n // tm,),
        in_specs=[pl.BlockSpec((tm, d), lambda i: (i, 0)),
                  _const_spec(wi.shape), _const_spec(wo.shape),
                  _const_spec((1, d)), _const_spec((1, d))],
        out_specs=pl.BlockSpec((tm, d), lambda i: (i, 0)),
        out_shape=jax.ShapeDtypeStruct((n, d), F32),
        scratch_shapes=[pltpu.VMEM((tm, d), F32)],
        compiler_params=_cparams(("parallel",)),
        name="ffn_ln",
    )(x, wi, wo, g.reshape(1, d), b.reshape(1, d))


_PROJ_OUTS = ("aqk", "av", "ao", "gi", "gf", "bq", "bk", "bv", "c")


def _in_proj_body(x_ref, w_ref, bias_ref, wvt_ref, bvt_ref, *out_refs, emit_vt):
    xb = x_ref[...].astype(BF16)
    outs = dict(zip(_PROJ_OUTS, out_refs))
    for name in _PROJ_OUTS:
        c0, w = _SEGS[name]
        for s in range(0, w, 2 * HB_W):
            e = min(s + 2 * HB_W, w)
            val = _dot(xb, w_ref[:, c0 + s:c0 + e]) + bias_ref[:, c0 + s:c0 + e]
            outs[name][:, s:e] = val
            if name == "bk":
                out_refs[len(_PROJ_OUTS)][:, s:e] = val.astype(BF16)
    if emit_vt:
        vt = _dot_nt(wvt_ref[...], xb) + bvt_ref[...]
        out_refs[len(_PROJ_OUTS) + 1][...] = vt.astype(BF16)


def _in_proj(x, w_cat, b_cat, w_vt, b_vt, emit_vt):
    n, d = x.shape
    tm = _tile(n, 512, 128 if emit_vt else 8)
    out_shape = [jax.ShapeDtypeStruct((n, _SEGS[k][1]), F32) for k in _PROJ_OUTS]
    out_specs = [pl.BlockSpec((tm, _SEGS[k][1]), lambda i: (i, 0)) for k in _PROJ_OUTS]
    out_shape.append(jax.ShapeDtypeStruct((n, ATT_W), BF16))
    out_specs.append(pl.BlockSpec((tm, ATT_W), lambda i: (i, 0)))
    if emit_vt:
        out_shape.append(jax.ShapeDtypeStruct((ATT_W, n), BF16))
        out_specs.append(pl.BlockSpec((ATT_W, tm), lambda i: (0, i)))
    return pl.pallas_call(
        functools.partial(_in_proj_body, emit_vt=emit_vt),
        grid=(n // tm,),
        in_specs=[pl.BlockSpec((tm, d), lambda i: (i, 0)),
                  _const_spec(w_cat.shape), _const_spec(b_cat.shape),
                  _const_spec(w_vt.shape), _const_spec(b_vt.shape)],
        out_specs=out_specs,
        out_shape=out_shape,
        compiler_params=_cparams(("parallel",)),
        name="in_proj",
    )(x, w_cat, b_cat, w_vt, b_vt)


def _out_proj_ln_body(ya_ref, yb_ref, yc_ref, x_ref, w_ref, g_ref, b_ref, o_ref, *, alpha):
    y = _dot(ya_ref[...].astype(BF16), w_ref[0:HB_W, :])
    y += _dot(yb_ref[...].astype(BF16), w_ref[HB_W:HB_W + ATT_W, :])
    y += _dot(yc_ref[...].astype(BF16), w_ref[HB_W + ATT_W:, :])
    o_ref[...] = _layer_norm(alpha * x_ref[...] + y, g_ref[...], b_ref[...])


def _out_proj_ln(ya, yb, yc, x, w, g, b, alpha):
    n, d = x.shape
    tm = _tile(n, 512)
    row = lambda width: pl.BlockSpec((tm, width), lambda i: (i, 0))
    return pl.pallas_call(
        functools.partial(_out_proj_ln_body, alpha=alpha),
        grid=(n // tm,),
        in_specs=[row(HB_W), row(ATT_W), row(HB_W), row(d), _const_spec(w.shape),
                  _const_spec((1, d)), _const_spec((1, d))],
        out_specs=row(d),
        out_shape=jax.ShapeDtypeStruct((n, d), F32),
        compiler_params=_cparams(("parallel",)),
        name="out_proj_ln",
    )(ya, yb, yc, x, w, g.reshape(1, d), b.reshape(1, d))


def _mlstm_body(qk_ref, v_ref, o_ref, gi_ref, gf_ref, cw_ref, cb_ref, conv0_ref, c0_ref, n0_ref, m0_ref,
                tri_ref, eblk_ref,
                ya_ref, c1_ref, n1_ref, m1_ref, convn_ref,
                up_ref, c_ref, n_ref, m_ref, *, rows):
    step = pl.program_id(1)
    L = SCAN_ROWS

    @pl.when(step == 0)
    def _():
        up_ref[0:8, :] = conv0_ref[...]
        c_ref[...] = c0_ref[...]
        n_ref[...] = n0_ref[...]
        m_ref[...] = m0_ref[...]

    u = _pad_rows(qk_ref[...], L)
    up_ref[8:8 + L, :] = u
    y = cb_ref[...]
    for j in range(CONV_W):
        y = y + cw_ref[j:j + 1, :] * up_ref[5 + j:5 + j + L, :]
    convn_ref[...] = up_ref[rows:rows + 8, :]
    up_ref[0:8, :] = up_ref[L:L + 8, :]
    qk = y * jax.nn.sigmoid(y)
    q = qk[:, :HB_W] * (HEAD_DIM ** -0.5)
    k = qk[:, HB_W:]
    v = _pad_rows(v_ref[...], L)
    kb = k.astype(BF16)
    vb = v.astype(BF16)

    gi = _pad_rows(gi_ref[...], L)
    flog = jax.nn.log_sigmoid(_pad_rows(gf_ref[...], L))
    if rows < L:
        live = lax.broadcasted_iota(jnp.int32, (L, 1), 0) < rows
        gi = jnp.where(live, gi, NEG_BIG)
        flog = jnp.where(live, flog, 0.0)
    b = _dot_exact_left(tri_ref[...], flog)
    b_t = b.T
    gi_t = gi.T
    m_prev = m_ref[...]
    g = b + m_prev

    lane_head = lax.broadcasted_iota(jnp.int32, (1, HB_W), 1) // HEAD_DIM
    t_idx = lax.broadcasted_iota(jnp.int32, (L, L), 0)
    s_idx = lax.broadcasted_iota(jnp.int32, (L, L), 1)
    causal = s_idx <= t_idx

    mrow_hb = jnp.zeros((L, HB_W), F32)
    wg_hb = jnp.zeros((L, HB_W), F32)
    den_hb = jnp.zeros((L, HB_W), F32)
    num_hb = jnp.zeros((L, HB_W), F32)
    for h in range(N_HEADS):
        sel = lane_head == h
        c0 = h * HEAD_DIM
        dmat = b[:, c0:c0 + 1] - b_t[c0:c0 + 1, :] + gi_t[c0:c0 + 1, :]
        dmat = jnp.where(causal, dmat, NEG_BIG)
        gcol = g[:, c0:c0 + 1]
        mrow = jnp.maximum(gcol, jnp.max(dmat, axis=1, keepdims=True))
        wd = jnp.exp(dmat - mrow)
        wg = jnp.exp(gcol - mrow)
        qh = jnp.where(sel, q, 0.0).astype(BF16)
        qkw = _dot_nt(qh, kb) * wd
        den = jnp.sum(qkw, axis=1, keepdims=True)
        num = _dot(qkw.astype(BF16), vb)
        mrow_hb = jnp.where(sel, mrow, mrow_hb)
        wg_hb = jnp.where(sel, wg, wg_hb)
        den_hb = jnp.where(sel, den, den_hb)
        num_hb = jnp.where(sel, num, num_hb)

    c_prev = c_ref[...]
    n_prev = n_ref[...]
    eblk = eblk_ref[...]
    qb = q.astype(BF16)
    num_hb = wg_hb * _dot(qb, c_prev.astype(BF16)) + num_hb
    den_hb = wg_hb * _dot((q * n_prev).astype(BF16), eblk) + den_hb
    hout = num_hb / jnp.maximum(jnp.abs(den_hb), jnp.exp(-mrow_hb))
    ya = jax.nn.sigmoid(_pad_rows(o_ref[...], L)) * hout
    ya_ref[...] = ya[:rows]

    m_new = mrow_hb[L - 1:L, :]
    b_last = b[L - 1:L, :]
    decay = jnp.exp(b_last + m_prev - m_new)
    ws = jnp.exp(b_last - b + gi - m_new)
    kw = k * ws
    upd = _dot(kw.T.astype(BF16), vb) * eblk.astype(F32)
    c_new = decay * c_prev + upd
    n_new = decay * n_prev + jnp.sum(kw, axis=0, keepdims=True)
    c_ref[...] = c_new
    n_ref[...] = n_new
    m_ref[...] = m_new

    @pl.when(step == pl.num_programs(1) - 1)
    def _():
        c1_ref[...] = c_new
        n1_ref[...] = n_new
        m1_ref[...] = m_new


def _mlstm(proj, conv_w, conv_b, conv0, c0_bd, n0, m0_hb, tri, eblk, batch, t_len):
    rows = min(SCAN_ROWS, t_len)
    steps = t_len // rows
    n = batch * t_len
    tok = lambda width: pl.BlockSpec((rows, width), lambda bi, ci: (bi * steps + ci, 0))
    per_b = lambda shape: pl.BlockSpec((None,) + shape, lambda bi, ci: (bi,) + (0,) * len(shape))
    return pl.pallas_call(
        functools.partial(_mlstm_body, rows=rows),
        grid=(batch, steps),
        in_specs=[tok(2 * HB_W), tok(HB_W), tok(HB_W), tok(HB_W), tok(HB_W),
                  _const_spec(conv_w.shape), _const_spec(conv_b.shape),
                  per_b((8, 2 * HB_W)), per_b((HB_W, HB_W)), per_b((1, HB_W)), per_b((1, HB_W)),
                  _const_spec(tri.shape), _const_spec(eblk.shape)],
        out_specs=[tok(HB_W), per_b((HB_W, HB_W)), per_b((1, HB_W)), per_b((1, HB_W)), per_b((8, 2 * HB_W))],
        out_shape=[jax.ShapeDtypeStruct((n, HB_W), F32),
                   jax.ShapeDtypeStruct((batch, HB_W, HB_W), F32),
                   jax.ShapeDtypeStruct((batch, 1, HB_W), F32),
                   jax.ShapeDtypeStruct((batch, 1, HB_W), F32),
                   jax.ShapeDtypeStruct((batch, 8, 2 * HB_W), F32)],
        scratch_shapes=[pltpu.VMEM((SCAN_ROWS + 8, 2 * HB_W), F32),
                        pltpu.VMEM((HB_W, HB_W), F32),
                        pltpu.VMEM((1, HB_W), F32),
                        pltpu.VMEM((1, HB_W), F32)],
        compiler_params=_cparams(("parallel", "arbitrary")),
        name="mlstm",
    )(proj["aqk"], proj["av"], proj["ao"], proj["gi"], proj["gf"], conv_w, conv_b, conv0, c0_bd, n0, m0_hb,
      tri, eblk)


def _hgrn_body(c_ref, lb_ref, hng_ref, s0_ref, tri16_ref, eblk_ref, yc_ref, s1_ref, s_ref, *, rows):
    step = pl.program_id(1)
    L = SCAN_ROWS

    @pl.when(step == 0)
    def _():
        s_ref[...] = s0_ref[...]

    cin = _pad_rows(c_ref[...], L)
    q = cin[:, 0:HB_W]
    f_pre = cin[:, HB_W:2 * HB_W]
    v = cin[:, 2 * HB_W:3 * HB_W]
    gate = cin[:, 3 * HB_W:]
    lb = lb_ref[...]
    forget = lb + (1.0 - lb) * jax.nn.sigmoid(f_pre)
    logf = jnp.log(forget)
    key = 1.0 - forget
    if rows < L:
        live = lax.broadcasted_iota(jnp.int32, (L, 1), 0) < rows
        logf = jnp.where(live, logf, 0.0)
        key = jnp.where(live, key, 0.0)
    bl = _dot_exact_left(tri16_ref[...], logf)
    eblk = eblk_ref[...]
    t_loc = lax.broadcasted_iota(jnp.int32, (L, 1), 0) % SUB

    o = jnp.zeros((L, HB_W), F32)
    for d in range(SUB):
        if d == 0:
            b_s, k_s, v_s = bl, key, v
        else:
            b_s = pltpu.roll(bl, d, 0)
            k_s = pltpu.roll(key, d, 0)
            v_s = pltpu.roll(v, d, 0)
        w = jnp.exp(jnp.minimum(bl - b_s, 0.0))
        p = jnp.where(t_loc >= d, q * k_s * w, 0.0)
        o = o + _dot(p.astype(BF16), eblk) * v_s

    row_sub = lax.broadcasted_iota(jnp.int32, (L, 1), 0) // SUB
    n_sub = max(rows // SUB, 1)
    qd = (q * jnp.exp(bl)).astype(BF16)
    bl_t = bl.T
    state = s_ref[...]
    eblk_f = eblk.astype(F32)
    o_inter = jnp.zeros((L, HB_W), F32)
    b_end = jnp.zeros((L, HB_W), F32)
    for j in range(n_sub):
        b_end = jnp.where(row_sub == j, bl[j * SUB + SUB - 1:j * SUB + SUB, :], b_end)
    kw_t = (key * jnp.exp(b_end - bl)).T.astype(BF16)
    for j in range(n_sub):
        in_j = row_sub == j
        o_j = _dot(qd, state.astype(BF16))
        o_inter = jnp.where(in_j, o_j, o_inter)
        v_j = jnp.where(in_j, v, 0.0).astype(BF16)
        upd = _dot(kw_t, v_j) * eblk_f
        dcol = jnp.exp(bl_t[:, j * SUB + SUB - 1:j * SUB + SUB])
        state = dcol * state + upd
    s_ref[...] = state
    o = o + o_inter

    ms = _dot_exact_right(o * o, eblk) * (1.0 / HEAD_DIM)
    yc = o * lax.rsqrt(ms + RMS_EPS) * hng_ref[...] * (gate * jax.nn.sigmoid(gate))
    yc_ref[...] = yc[:rows]

    @pl.when(step == pl.num_programs(1) - 1)
    def _():
        s1_ref[...] = state


def _hgrn(c_all, lb, hn_g, s0_bd, tri16, eblk, batch, t_len):
    rows = min(SCAN_ROWS, t_len)
    steps = t_len // rows
    n = batch * t_len
    return pl.pallas_call(
        functools.partial(_hgrn_body, rows=rows),
        grid=(batch, steps),
        in_specs=[pl.BlockSpec((rows, 4 * HB_W), lambda bi, ci: (bi * steps + ci, 0)),
                  _const_spec((1, HB_W)), _const_spec((1, HB_W)),
                  pl.BlockSpec((None, HB_W, HB_W), lambda bi, ci: (bi, 0, 0)),
                  _const_spec(tri16.shape), _const_spec(eblk.shape)],
        out_specs=[pl.BlockSpec((rows, HB_W), lambda bi, ci: (bi * steps + ci, 0)),
                   pl.BlockSpec((None, HB_W, HB_W), lambda bi, ci: (bi, 0, 0))],
        out_shape=[jax.ShapeDtypeStruct((n, HB_W), F32),
                   jax.ShapeDtypeStruct((batch, HB_W, HB_W), F32)],
        scratch_shapes=[pltpu.VMEM((HB_W, HB_W), F32)],
        compiler_params=_cparams(("parallel", "arbitrary")),
        name="hgrn",
    )(c_all, lb, hn_g, s0_bd, tri16, eblk)


def _attn_prompt_body(lam_ref, q_ref, k_ref, vt_ref, g_ref, o_ref,
                      qa_ref, qb_ref, m1_ref, l1_ref, a1_ref, m2_ref, l2_ref, a2_ref, *, tq, out_scale):
    qi = pl.program_id(2)
    ki = pl.program_id(3)

    @pl.when(ki == 0)
    def _():
        qt = (q_ref[...] * (HEAD_DIM ** -0.5)).T
        first = lax.broadcasted_iota(jnp.int32, (2 * HEAD_DIM, 1), 0) < HEAD_DIM
        qa_ref[...] = jnp.where(first, qt, 0.0).astype(BF16)
        qb_ref[...] = jnp.where(first, 0.0, qt).astype(BF16)
        for m_ref, l_ref, a_ref in ((m1_ref, l1_ref, a1_ref), (m2_ref, l2_ref, a2_ref)):
            m_ref[...] = jnp.full(m_ref.shape, NEG_BIG, F32)
            l_ref[...] = jnp.zeros(l_ref.shape, F32)
            a_ref[...] = jnp.zeros(a_ref.shape, F32)

    def update(masked):
        kb = k_ref[...]
        vt = vt_ref[...]
        if masked:
            k_chunk = lax.broadcasted_iota(jnp.int32, (tq, tq), 0) // CHUNK
            q_chunk = lax.broadcasted_iota(jnp.int32, (tq, tq), 1) // CHUNK
            visible = k_chunk <= q_chunk
        for qx_ref, m_ref, l_ref, a_ref in ((qa_ref, m1_ref, l1_ref, a1_ref), (qb_ref, m2_ref, l2_ref, a2_ref)):
            s = _dot(kb, qx_ref[...])
            if masked:
                s = jnp.where(visible, s, NEG_BIG)
            m_old = m_ref[...]
            m_new = jnp.maximum(m_old, jnp.max(s, axis=0, keepdims=True))
            p = jnp.exp(s - m_new)
            corr = jnp.exp(m_old - m_new)
            l_ref[...] = corr * l_ref[...] + jnp.sum(p, axis=0, keepdims=True)
            a_ref[...] = corr * a_ref[...] + _dot(vt, p.astype(BF16))
            m_ref[...] = m_new

    @pl.when(ki < qi)
    def _():
        update(False)

    @pl.when(ki == qi)
    def _():
        update(True)
        lam = lam_ref[0, 0]
        o = a1_ref[...] / l1_ref[...] - lam * (a2_ref[...] / l2_ref[...])
        ms = jnp.mean(o * o, axis=0, keepdims=True)
        o = o * lax.rsqrt(ms + RMS_EPS) * g_ref[...] * out_scale
        o_ref[...] = o.T


def _attn_prompt(lam, bq, bkb, vt, dn_g, batch, t_len, out_scale):
    tq = _tile(t_len, 512, 128)
    nq = t_len // tq
    n = batch * t_len
    kmap = lambda b, h, qi, ki: (b * nq + jnp.minimum(ki, qi), h)
    vmap_ = lambda b, h, qi, ki: (h, b * nq + jnp.minimum(ki, qi))
    qmap = lambda b, h, qi, ki: (b * nq + qi, h)
    w = 2 * HEAD_DIM
    return pl.pallas_call(
        functools.partial(_attn_prompt_body, tq=tq, out_scale=out_scale),
        grid=(batch, N_HEADS, nq, nq),
        in_specs=[pl.BlockSpec(memory_space=pltpu.SMEM),
                  pl.BlockSpec((tq, w), qmap), pl.BlockSpec((tq, w), kmap), pl.BlockSpec((w, tq), vmap_),
                  _const_spec((w, 1))],
        out_specs=pl.BlockSpec((tq, w), qmap),
        out_shape=jax.ShapeDtypeStruct((n, ATT_W), F32),
        scratch_shapes=[pltpu.VMEM((w, tq), BF16), pltpu.VMEM((w, tq), BF16),
                        pltpu.VMEM((1, tq), F32), pltpu.VMEM((1, tq), F32), pltpu.VMEM((w, tq), F32),
                        pltpu.VMEM((1, tq), F32), pltpu.VMEM((1, tq), F32), pltpu.VMEM((w, tq), F32)],
        compiler_params=_cparams(("parallel", "parallel", "parallel", "arbitrary")),
        name="attn_prompt",
    )(lam, bq, bkb, vt, dn_g.reshape(w, 1))


def _attn_sample_body(lam_ref, q_ref, kn_ref, vn_ref, kp_ref, vp_ref, g_ref, o_ref, *, t_new, out_scale):
    q = q_ref[...] * (HEAD_DIM ** -0.5)
    first = lax.broadcasted_iota(jnp.int32, (1, 2 * HEAD_DIM), 1) < HEAD_DIM
    q2 = jnp.concatenate([jnp.where(first, q, 0.0), jnp.where(first, 0.0, q)], axis=0).astype(BF16)
    s_p = _dot_nt(q2, kp_ref[...].astype(BF16))
    s_n = _dot_nt(q2, kn_ref[...].astype(BF16))
    m = jnp.maximum(jnp.max(s_p, axis=1, keepdims=True), jnp.max(s_n, axis=1, keepdims=True))
    p_p = jnp.exp(s_p - m)
    p_n = jnp.exp(s_n - m)
    l = jnp.sum(p_p, axis=1, keepdims=True) + jnp.sum(p_n, axis=1, keepdims=True)
    acc = _dot(p_p.astype(BF16), vp_ref[...].astype(BF16)) + _dot(p_n.astype(BF16), vn_ref[...].astype(BF16))
    o2 = acc / l
    o = o2[:t_new] - lam_ref[0, 0] * o2[t_new:]
    ms = jnp.mean(o * o, axis=1, keepdims=True)
    o_ref[...] = o * lax.rsqrt(ms + RMS_EPS) * g_ref[...] * out_scale


def _attn_sample(lam, bq, bk, bv, cache_k, cache_v, layer, dn_g, batch, t_new, out_scale):
    w = 2 * HEAD_DIM
    past = cache_k.shape[2]
    new = pl.BlockSpec((t_new, w), lambda b, h: (b, h))
    old = pl.BlockSpec((None, None, past, w), lambda b, h: (layer, b, 0, h))
    return pl.pallas_call(
        functools.partial(_attn_sample_body, t_new=t_new, out_scale=out_scale),
        grid=(batch, N_HEADS),
        in_specs=[pl.BlockSpec(memory_space=pltpu.SMEM), new, new, new, old, old, _const_spec((1, w))],
        out_specs=new,
        out_shape=jax.ShapeDtypeStruct((batch * t_new, ATT_W), F32),
        compiler_params=_cparams(("parallel", "parallel")),
        name="attn_sample",
    )(lam, bq, bk, bv, cache_k, cache_v, dn_g.reshape(1, w))


def _to_block_diag(c):
    b = c.shape[0]
    eye = jnp.eye(N_HEADS, dtype=c.dtype)
    return (c[:, :, :, None, :] * eye[None, :, None, :, None]).reshape(b, HB_W, HB_W)


def _from_block_diag(cbd):
    b = cbd.shape[0]
    r = cbd.reshape(b, N_HEADS, HEAD_DIM, N_HEADS, HEAD_DIM)
    return jnp.stack([r[:, h, :, h, :] for h in range(N_HEADS)], axis=1)


def _proj_columns():
    rep = lambda start: [start + h for h in range(N_HEADS) for _ in range(HEAD_DIM)]
    cols = (list(range(A_QK0, A_V0)) + list(range(A_V0, A_I0)) + list(range(A_O0, B_Q0))
            + rep(A_I0) + rep(A_F0)
            + list(range(B_Q0, B_K0)) + list(range(B_K0, B_V0)) + list(range(B_V0, C_Q0))
            + list(range(C_Q0, N_IN)))
    assert len(cols) == N_CAT
    return jnp.asarray(cols, dtype=jnp.int32)


def _consts():
    r = jnp.arange(SCAN_ROWS)
    tri = (r[None, :] <= r[:, None]).astype(BF16)
    tri16 = ((r[None, :] <= r[:, None]) & (r[None, :] // SUB == r[:, None] // SUB)).astype(BF16)
    hh = jnp.arange(HB_W) // HEAD_DIM
    eblk = (hh[None, :] == hh[:, None]).astype(BF16)
    return tri, tri16, eblk


def _group(x, depth, layer_w, states, cache, batch, t_len, alpha, consts):
    tri, tri16, eblk = consts
    prompt = cache is None
    outs = []
    for l in range(depth):
        w = layer_w[l]
        conv0, c0, n0, m0, s0 = states(l)
        x = _ffn_ln(x, w["f_in0"], w["f_out0"], w["ln_g"][0], w["ln_b"][0], alpha)
        res = _in_proj(x, w["w_cat"], w["b_cat"], w["w_vt"], w["b_vt"], prompt)
        proj = dict(zip(_PROJ_OUTS, res[:len(_PROJ_OUTS)]))
        bkb = res[len(_PROJ_OUTS)]

        conv0_p = jnp.pad(conv0, ((0, 0), (8 - (CONV_W - 1), 0), (0, 0)))
        ya, c1, n1, m1, convn = _mlstm(proj, w["conv_w"], w["conv_b"], conv0_p, _to_block_diag(c0),
                                       n0.reshape(batch, 1, HB_W),
                                       jnp.repeat(m0, HEAD_DIM, axis=-1).reshape(batch, 1, HB_W),
                                       tri, eblk, batch, t_len)
        lam_init = 0.8 - 0.6 * math.exp(-0.3 * l)
        if prompt:
            yb = _attn_prompt(w["lam"], proj["bq"], bkb, res[len(_PROJ_OUTS) + 1], w["dn_g"], batch, t_len,
                              1.0 - lam_init)
        else:
            yb = _attn_sample(w["lam"], proj["bq"], proj["bk"], proj["bv"], cache[0], cache[1], l, w["dn_g"],
                              batch, t_len, 1.0 - lam_init)
        yc, s1 = _hgrn(proj["c"], w["lb"], w["hn_g"], _to_block_diag(s0), tri16, eblk, batch, t_len)
        x = _out_proj_ln(ya, yb, yc, x, w["w_out"], w["ln_g"][1], w["ln_b"][1], alpha)
        x = _ffn_ln(x, w["f_in1"], w["f_out1"], w["ln_g"][2], w["ln_b"][2], alpha)
        outs.append((proj["bk"].reshape(batch, t_len, N_HEADS, 2 * HEAD_DIM),
                     proj["bv"].reshape(batch, t_len, N_HEADS, 2 * HEAD_DIM),
                     _from_block_diag(c1),
                     n1.reshape(batch, N_HEADS, HEAD_DIM),
                     m1.reshape(batch, N_HEADS, HEAD_DIM)[:, :, 0],
                     convn[:, 8 - (CONV_W - 1):, :],
                     _from_block_diag(s1)))
    return x, tuple(jnp.stack(a) for a in zip(*outs))


def kernel(x_prompt, x_sample, cache_diff_k, cache_diff_v, state_mlstm_c, state_mlstm_n, state_mlstm_m,
           state_mlstm_conv, state_hgrn_s, w_in, b_in, w_out, mlstm_conv_w, mlstm_conv_b, diff_lambda,
           diff_norm_g, hgrn_lb_logits, hgrn_norm_g, ffn_w_in, ffn_w_out, ln_g, ln_b):
    depth = w_in.shape[0]
    alpha = (2.0 * depth) ** 0.25
    batch, seq, d_model = x_prompt.shape
    dec_batch, dec_seq, _ = x_sample.shape
    past = cache_diff_k.shape[2]

    p_lb = jax.nn.softmax(hgrn_lb_logits.astype(F32), axis=0)
    lower_bounds = jnp.cumsum(p_lb, axis=0) - p_lb[0]
    cols = _proj_columns()
    w_cat = w_in[:, :, cols].astype(BF16)
    b_cat = b_in[:, cols].reshape(depth, 1, N_CAT)
    w_vt = jnp.swapaxes(w_in[:, :, B_V0:C_Q0], 1, 2).astype(BF16)
    b_vt = b_in[:, B_V0:C_Q0].reshape(depth, ATT_W, 1)
    lp = diff_lambda.astype(F32)
    lam_init = jnp.asarray([0.8 - 0.6 * math.exp(-0.3 * l) for l in range(depth)], F32)
    lam = (jnp.exp(jnp.sum(lp[:, 0] * lp[:, 1], axis=-1)) - jnp.exp(jnp.sum(lp[:, 2] * lp[:, 3], axis=-1))
           + lam_init)
    f_in = ffn_w_in.astype(BF16)
    f_out = ffn_w_out.astype(BF16)
    w_out_b = w_out.astype(BF16)
    layer_w = [dict(w_cat=w_cat[l], b_cat=b_cat[l], w_vt=w_vt[l], b_vt=b_vt[l],
                    conv_w=mlstm_conv_w[l], conv_b=mlstm_conv_b[l].reshape(1, 2 * HB_W),
                    lam=lam[l].reshape(1, 1), dn_g=diff_norm_g[l],
                    lb=lower_bounds[l].reshape(1, HB_W),
                    hn_g=jnp.tile(hgrn_norm_g[l], N_HEADS).reshape(1, HB_W),
                    w_out=w_out_b[l], f_in0=f_in[l, 0], f_out0=f_out[l, 0], f_in1=f_in[l, 1], f_out1=f_out[l, 1],
                    ln_g=ln_g[l], ln_b=ln_b[l]) for l in range(depth)]
    consts = _consts()

    def zero_states(_):
        return (jnp.zeros((batch, CONV_W - 1, 2 * HB_W), F32), jnp.zeros((batch, N_HEADS, HEAD_DIM, HEAD_DIM), F32),
                jnp.zeros((batch, N_HEADS, HEAD_DIM), F32), jnp.zeros((batch, N_HEADS), F32),
                jnp.zeros((batch, N_HEADS, HEAD_DIM, HEAD_DIM), F32))

    def carried_states(l):
        return (state_mlstm_conv[l], state_mlstm_c[l], state_mlstm_n[l], state_mlstm_m[l], state_hgrn_s[l])

    y_p, p_out = _group(x_prompt.reshape(batch * seq, d_model), depth, layer_w, zero_states, None,
                        batch, seq, alpha, consts)
    cache = (cache_diff_k.reshape(depth, dec_batch, past, ATT_W), cache_diff_v.reshape(depth, dec_batch, past, ATT_W))
    y_s, s_out = _group(x_sample.reshape(dec_batch * dec_seq, d_model), depth, layer_w, carried_states, cache,
                        dec_batch, dec_seq, alpha, consts)
    return (y_p.reshape(batch, seq, d_model), y_s.reshape(dec_batch, dec_seq, d_model)) + p_out + s_out
```

```python
import functools
import math

import jax
import jax.numpy as jnp
from jax import lax
from jax.experimental import pallas as pl
from jax.experimental.pallas import tpu as pltpu

F32 = jnp.float32
BF16 = jnp.bfloat16

HEAD_DIM = 64
N_HEADS = 4
CONV_W = 4
CHUNK = 64
HB_W = N_HEADS * HEAD_DIM
ATT_W = N_HEADS * 2 * HEAD_DIM
LN_EPS = 1e-5
RMS_EPS = 1e-6
NEG_BIG = -1e30
SCAN_ROWS = 128
SUB = 16
LOG2E = 1.4426950408889634
VT_ROWS = 2 * HEAD_DIM + 16
Q_STRIP = 256
V7X_VMEM_LIMIT = 56 * 1024 * 1024

A_QK0 = 0
A_V0 = A_QK0 + 2 * HB_W
A_I0 = A_V0 + HB_W
A_F0 = A_I0 + N_HEADS
A_O0 = A_F0 + N_HEADS
B_Q0 = A_O0 + HB_W
B_K0 = B_Q0 + ATT_W
B_V0 = B_K0 + ATT_W
C_Q0 = B_V0 + ATT_W
N_IN = C_Q0 + 4 * HB_W

_SEGS = {}
_off = 0
for _name, _w in (("aqk", 2 * HB_W), ("av", HB_W), ("ao", HB_W), ("gi", HB_W), ("gf", HB_W),
                  ("bq", ATT_W), ("bk", ATT_W), ("bv", ATT_W), ("c", 4 * HB_W)):
    _SEGS[_name] = (_off, _w)
    _off += _w
N_CAT = _off


def _tile(n, pref, mult=8):
    t = min(pref, n)
    while t > mult and (n % t or t % mult):
        t -= mult
    assert n % t == 0 and t % mult == 0, (n, pref, mult)
    return t


def _cparams(sem):
    return pltpu.CompilerParams(dimension_semantics=sem, vmem_limit_bytes=V7X_VMEM_LIMIT)


def _const_spec(shape):
    nd = len(shape)
    return pl.BlockSpec(shape, lambda *_: (0,) * nd)


def _dot(a, b):
    return jnp.dot(a, b, preferred_element_type=F32)


def _dot_nt(a, b):
    return lax.dot_general(a, b, (((1,), (1,)), ((), ())), preferred_element_type=F32)


def _split3(x):
    a = x.astype(BF16)
    r = x - a.astype(F32)
    b = r.astype(BF16)
    c = (r - b.astype(F32)).astype(BF16)
    return a, b, c


def _dot_exact_left(mat01, x):
    a, b, c = _split3(x)
    return _dot(mat01, a) + _dot(mat01, b) + _dot(mat01, c)


def _dot_exact_right(x, mat01):
    a, b, c = _split3(x)
    return _dot(a, mat01) + _dot(b, mat01) + _dot(c, mat01)


def _layer_norm(y, g, b):
    mu = jnp.mean(y, axis=-1, keepdims=True)
    d = y - mu
    var = jnp.mean(d * d, axis=-1, keepdims=True)
    return d * lax.rsqrt(var + LN_EPS) * g + b


def _pad_rows(x, rows):
    if x.shape[0] == rows:
        return x
    return jnp.concatenate([x, jnp.zeros((rows - x.shape[0],) + x.shape[1:], x.dtype)], axis=0)


def _ffn_ln_body(x_ref, wi_ref, wo_ref, g_ref, b_ref, o_ref, acc_ref, *, alpha, d_ff, tf):
    x = x_ref[...]
    xb = x.astype(BF16)
    for c in range(d_ff // tf):
        gate = _dot(xb, wi_ref[:, c * tf:(c + 1) * tf])
        up = _dot(xb, wi_ref[:, d_ff + c * tf:d_ff + (c + 1) * tf])
        h = (gate * jax.nn.sigmoid(gate) * up).astype(BF16)
        part = _dot(h, wo_ref[c * tf:(c + 1) * tf, :])
        if c == 0:
            acc_ref[...] = part
        else:
            acc_ref[...] += part
    y = alpha * x + 0.5 * acc_ref[...]
    o_ref[...] = _layer_norm(y, g_ref[...], b_ref[...])


def _ffn_ln(x, wi, wo, g, b, alpha):
    n, d = x.shape
    d_ff = wo.shape[0]
    tm = _tile(n, 512)
    tf = _tile(d_ff, 256, 128)
    return pl.pallas_call(
        functools.partial(_ffn_ln_body, alpha=alpha, d_ff=d_ff, tf=tf),
        grid=(n // tm,),
        in_specs=[pl.BlockSpec((tm, d), lambda i: (i, 0)),
                  _const_spec(wi.shape), _const_spec(wo.shape),
                  _const_spec((1, d)), _const_spec((1, d))],
        out_specs=pl.BlockSpec((tm, d), lambda i: (i, 0)),
        out_shape=jax.ShapeDtypeStruct((n, d), F32),
        scratch_shapes=[pltpu.VMEM((tm, d), F32)],
        compiler_params=_cparams(("parallel",)),
        name="ffn_ln",
    )(x, wi, wo, g.reshape(1, d), b.reshape(1, d))


_PROJ_OUTS = ("aqk", "av", "ao", "gi", "gf", "bq", "bk", "bv", "c")


def _in_proj_body(x_ref, w_ref, bias_ref, wvt_ref, bvt_ref, *out_refs, emit_vt):
    xb = x_ref[...].astype(BF16)
    outs = dict(zip(_PROJ_OUTS, out_refs))
    for name in _PROJ_OUTS:
        c0, w = _SEGS[name]
        for s in range(0, w, 2 * HB_W):
            e = min(s + 2 * HB_W, w)
            val = _dot(xb, w_ref[:, c0 + s:c0 + e]) + bias_ref[:, c0 + s:c0 + e]
            outs[name][:, s:e] = val
            if name == "bk":
                out_refs[len(_PROJ_OUTS)][:, s:e] = val.astype(BF16)
    if emit_vt:
        vt = (_dot_nt(wvt_ref[...], xb) + bvt_ref[...]).astype(BF16)
        for h in range(N_HEADS):
            for j in range(vt.shape[1] // Q_STRIP):
                out_refs[len(_PROJ_OUTS) + 1][h, j] = vt[h * VT_ROWS:(h + 1) * VT_ROWS, j * Q_STRIP:(j + 1) * Q_STRIP]


def _in_proj(x, w_cat, b_cat, w_vt, b_vt, emit_vt):
    n, d = x.shape
    tm = _tile(n, 512, Q_STRIP if emit_vt else 8)
    out_shape = [jax.ShapeDtypeStruct((n, _SEGS[k][1]), F32) for k in _PROJ_OUTS]
    out_specs = [pl.BlockSpec((tm, _SEGS[k][1]), lambda i: (i, 0)) for k in _PROJ_OUTS]
    out_shape.append(jax.ShapeDtypeStruct((n, ATT_W), BF16))
    out_specs.append(pl.BlockSpec((tm, ATT_W), lambda i: (i, 0)))
    if emit_vt:
        out_shape.append(jax.ShapeDtypeStruct((N_HEADS, n // Q_STRIP, VT_ROWS, Q_STRIP), BF16))
        out_specs.append(pl.BlockSpec((N_HEADS, tm // Q_STRIP, VT_ROWS, Q_STRIP), lambda i: (0, i, 0, 0)))
    return pl.pallas_call(
        functools.partial(_in_proj_body, emit_vt=emit_vt),
        grid=(n // tm,),
        in_specs=[pl.BlockSpec((tm, d), lambda i: (i, 0)),
                  _const_spec(w_cat.shape), _const_spec(b_cat.shape),
                  _const_spec(w_vt.shape), _const_spec(b_vt.shape)],
        out_specs=out_specs,
        out_shape=out_shape,
        compiler_params=_cparams(("parallel",)),
        name="in_proj",
    )(x, w_cat, b_cat, w_vt, b_vt)


def _out_proj_ln_body(ya_ref, yb_ref, yc_ref, x_ref, w_ref, g_ref, b_ref, o_ref, *, alpha):
    y = _dot(ya_ref[...].astype(BF16), w_ref[0:HB_W, :])
    y += _dot(yb_ref[...].astype(BF16), w_ref[HB_W:HB_W + ATT_W, :])
    y += _dot(yc_ref[...].astype(BF16), w_ref[HB_W + ATT_W:, :])
    o_ref[...] = _layer_norm(alpha * x_ref[...] + y, g_ref[...], b_ref[...])


def _out_proj_ln(ya, yb, yc, x, w, g, b, alpha):
    n, d = x.shape
    tm = _tile(n, 512)
    row = lambda width: pl.BlockSpec((tm, width), lambda i: (i, 0))
    return pl.pallas_call(
        functools.partial(_out_proj_ln_body, alpha=alpha),
        grid=(n // tm,),
        in_specs=[row(HB_W), row(ATT_W), row(HB_W), row(d), _const_spec(w.shape),
                  _const_spec((1, d)), _const_spec((1, d))],
        out_specs=row(d),
        out_shape=jax.ShapeDtypeStruct((n, d), F32),
        compiler_params=_cparams(("parallel",)),
        name="out_proj_ln",
    )(ya, yb, yc, x, w, g.reshape(1, d), b.reshape(1, d))


def _mlstm_body(qk_ref, v_ref, o_ref, gi_ref, gf_ref, cw_ref, cb_ref, conv0_ref, c0_ref, n0_ref, m0_ref,
                tri_ref, eblk_ref,
                ya_ref, c1_ref, n1_ref, m1_ref, convn_ref,
                up_ref, c_ref, n_ref, m_ref, *, rows):
    step = pl.program_id(1)
    L = SCAN_ROWS

    @pl.when(step == 0)
    def _():
        up_ref[0:8, :] = conv0_ref[...]
        c_ref[...] = c0_ref[...]
        n_ref[...] = n0_ref[...]
        m_ref[...] = m0_ref[...]

    u = _pad_rows(qk_ref[...], L)
    up_ref[8:8 + L, :] = u
    y = cb_ref[...]
    for j in range(CONV_W):
        y = y + cw_ref[j:j + 1, :] * up_ref[5 + j:5 + j + L, :]
    convn_ref[...] = up_ref[rows:rows + 8, :]
    up_ref[0:8, :] = up_ref[L:L + 8, :]
    qk = y * jax.nn.sigmoid(y)
    q = qk[:, :HB_W] * (HEAD_DIM ** -0.5)
    k = qk[:, HB_W:]
    v = _pad_rows(v_ref[...], L)
    kb = k.astype(BF16)
    vb = v.astype(BF16)

    gi = _pad_rows(gi_ref[...], L)
    flog = jax.nn.log_sigmoid(_pad_rows(gf_ref[...], L))
    if rows < L:
        live = lax.broadcasted_iota(jnp.int32, (L, 1), 0) < rows
        gi = jnp.where(live, gi, NEG_BIG)
        flog = jnp.where(live, flog, 0.0)
    b = _dot_exact_left(tri_ref[...], flog)
    b_t = b.T
    gi_t = gi.T
    m_prev = m_ref[...]
    g = b + m_prev

    lane_head = lax.broadcasted_iota(jnp.int32, (1, HB_W), 1) // HEAD_DIM
    t_idx = lax.broadcasted_iota(jnp.int32, (L, L), 0)
    s_idx = lax.broadcasted_iota(jnp.int32, (L, L), 1)
    causal = s_idx <= t_idx

    mrow_hb = jnp.zeros((L, HB_W), F32)
    wg_hb = jnp.zeros((L, HB_W), F32)
    den_hb = jnp.zeros((L, HB_W), F32)
    num_hb = jnp.zeros((L, HB_W), F32)
    for h in range(N_HEADS):
        sel = lane_head == h
        c0 = h * HEAD_DIM
        dmat = b[:, c0:c0 + 1] - b_t[c0:c0 + 1, :] + gi_t[c0:c0 + 1, :]
        dmat = jnp.where(causal, dmat, NEG_BIG)
        gcol = g[:, c0:c0 + 1]
        mrow = jnp.maximum(gcol, jnp.max(dmat, axis=1, keepdims=True))
        wd = jnp.exp(dmat - mrow)
        wg = jnp.exp(gcol - mrow)
        qh = jnp.where(sel, q, 0.0).astype(BF16)
        qkw = _dot_nt(qh, kb) * wd
        den = jnp.sum(qkw, axis=1, keepdims=True)
        num = _dot(qkw.astype(BF16), vb)
        mrow_hb = jnp.where(sel, mrow, mrow_hb)
        wg_hb = jnp.where(sel, wg, wg_hb)
        den_hb = jnp.where(sel, den, den_hb)
        num_hb = jnp.where(sel, num, num_hb)

    c_prev = c_ref[...]
    n_prev = n_ref[...]
    eblk = eblk_ref[...]
    qb = q.astype(BF16)
    num_hb = wg_hb * _dot(qb, c_prev.astype(BF16)) + num_hb
    den_hb = wg_hb * _dot((q * n_prev).astype(BF16), eblk) + den_hb
    hout = num_hb / jnp.maximum(jnp.abs(den_hb), jnp.exp(-mrow_hb))
    ya = jax.nn.sigmoid(_pad_rows(o_ref[...], L)) * hout
    ya_ref[...] = ya[:rows]

    m_new = mrow_hb[L - 1:L, :]
    b_last = b[L - 1:L, :]
    decay = jnp.exp(b_last + m_prev - m_new)
    ws = jnp.exp(b_last - b + gi - m_new)
    kw = k * ws
    upd = _dot(kw.T.astype(BF16), vb) * eblk.astype(F32)
    c_new = decay * c_prev + upd
    n_new = decay * n_prev + jnp.sum(kw, axis=0, keepdims=True)
    c_ref[...] = c_new
    n_ref[...] = n_new
    m_ref[...] = m_new

    @pl.when(step == pl.num_programs(1) - 1)
    def _():
        c1_ref[...] = c_new
        n1_ref[...] = n_new
        m1_ref[...] = m_new


def _mlstm(proj, conv_w, conv_b, conv0, c0_bd, n0, m0_hb, tri, eblk, batch, t_len):
    rows = min(SCAN_ROWS, t_len)
    steps = t_len // rows
    n = batch * t_len
    tok = lambda width: pl.BlockSpec((rows, width), lambda bi, ci: (bi * steps + ci, 0))
    per_b = lambda shape: pl.BlockSpec((None,) + shape, lambda bi, ci: (bi,) + (0,) * len(shape))
    return pl.pallas_call(
        functools.partial(_mlstm_body, rows=rows),
        grid=(batch, steps),
        in_specs=[tok(2 * HB_W), tok(HB_W), tok(HB_W), tok(HB_W), tok(HB_W),
                  _const_spec(conv_w.shape), _const_spec(conv_b.shape),
                  per_b((8, 2 * HB_W)), per_b((HB_W, HB_W)), per_b((1, HB_W)), per_b((1, HB_W)),
                  _const_spec(tri.shape), _const_spec(eblk.shape)],
        out_specs=[tok(HB_W), per_b((HB_W, HB_W)), per_b((1, HB_W)), per_b((1, HB_W)), per_b((8, 2 * HB_W))],
        out_shape=[jax.ShapeDtypeStruct((n, HB_W), F32),
                   jax.ShapeDtypeStruct((batch, HB_W, HB_W), F32),
                   jax.ShapeDtypeStruct((batch, 1, HB_W), F32),
                   jax.ShapeDtypeStruct((batch, 1, HB_W), F32),
                   jax.ShapeDtypeStruct((batch, 8, 2 * HB_W), F32)],
        scratch_shapes=[pltpu.VMEM((SCAN_ROWS + 8, 2 * HB_W), F32),
                        pltpu.VMEM((HB_W, HB_W), F32),
                        pltpu.VMEM((1, HB_W), F32),
                        pltpu.VMEM((1, HB_W), F32)],
        compiler_params=_cparams(("parallel", "arbitrary")),
        name="mlstm",
    )(proj["aqk"], proj["av"], proj["ao"], proj["gi"], proj["gf"], conv_w, conv_b, conv0, c0_bd, n0, m0_hb,
      tri, eblk)


def _hgrn_body(c_ref, lb_ref, hng_ref, s0_ref, tri16_ref, eblk_ref, yc_ref, s1_ref, s_ref, *, rows):
    step = pl.program_id(1)
    L = SCAN_ROWS

    @pl.when(step == 0)
    def _():
        s_ref[...] = s0_ref[...]

    cin = _pad_rows(c_ref[...], L)
    q = cin[:, 0:HB_W]
    f_pre = cin[:, HB_W:2 * HB_W]
    v = cin[:, 2 * HB_W:3 * HB_W]
    gate = cin[:, 3 * HB_W:]
    lb = lb_ref[...]
    forget = lb + (1.0 - lb) * jax.nn.sigmoid(f_pre)
    logf = jnp.log(forget)
    key = 1.0 - forget
    if rows < L:
        live = lax.broadcasted_iota(jnp.int32, (L, 1), 0) < rows
        logf = jnp.where(live, logf, 0.0)
        key = jnp.where(live, key, 0.0)
    bl = _dot_exact_left(tri16_ref[...], logf) * LOG2E
    eblk = eblk_ref[...]
    t_loc = lax.broadcasted_iota(jnp.int32, (L, 1), 0) % SUB

    o = jnp.zeros((L, HB_W), F32)
    for d in range(SUB):
        if d == 0:
            b_s, k_s, v_s = bl, key, v
        else:
            b_s = pltpu.roll(bl, d, 0)
            k_s = pltpu.roll(key, d, 0)
            v_s = pltpu.roll(v, d, 0)
        w = jnp.exp2(bl - b_s)
        p = jnp.where(t_loc >= d, q * k_s * w, 0.0)
        o = o + _dot(p.astype(BF16), eblk) * v_s

    row_sub = lax.broadcasted_iota(jnp.int32, (L, 1), 0) // SUB
    n_sub = max(rows // SUB, 1)
    qd = (q * jnp.exp2(bl)).astype(BF16)
    bl_t = bl.T
    state = s_ref[...]
    eblk_f = eblk.astype(F32)
    o_inter = jnp.zeros((L, HB_W), F32)
    b_end = jnp.zeros((L, HB_W), F32)
    for j in range(n_sub):
        b_end = jnp.where(row_sub == j, bl[j * SUB + SUB - 1:j * SUB + SUB, :], b_end)
    kw_t = (key * jnp.exp2(b_end - bl)).T.astype(BF16)
    for j in range(n_sub):
        in_j = row_sub == j
        o_j = _dot(qd, state.astype(BF16))
        o_inter = jnp.where(in_j, o_j, o_inter)
        v_j = jnp.where(in_j, v, 0.0).astype(BF16)
        upd = _dot(kw_t, v_j) * eblk_f
        dcol = jnp.exp2(bl_t[:, j * SUB + SUB - 1:j * SUB + SUB])
        state = dcol * state + upd
    s_ref[...] = state
    o = o + o_inter

    ms = _dot_exact_right(o * o, eblk) * (1.0 / HEAD_DIM)
    yc = o * lax.rsqrt(ms + RMS_EPS) * hng_ref[...] * (gate * jax.nn.sigmoid(gate))
    yc_ref[...] = yc[:rows]

    @pl.when(step == pl.num_programs(1) - 1)
    def _():
        s1_ref[...] = state


def _hgrn(c_all, lb, hn_g, s0_bd, tri16, eblk, batch, t_len):
    rows = min(SCAN_ROWS, t_len)
    steps = t_len // rows
    n = batch * t_len
    return pl.pallas_call(
        functools.partial(_hgrn_body, rows=rows),
        grid=(batch, steps),
        in_specs=[pl.BlockSpec((rows, 4 * HB_W), lambda bi, ci: (bi * steps + ci, 0)),
                  _const_spec((1, HB_W)), _const_spec((1, HB_W)),
                  pl.BlockSpec((None, HB_W, HB_W), lambda bi, ci: (bi, 0, 0)),
                  _const_spec(tri16.shape), _const_spec(eblk.shape)],
        out_specs=[pl.BlockSpec((rows, HB_W), lambda bi, ci: (bi * steps + ci, 0)),
                   pl.BlockSpec((None, HB_W, HB_W), lambda bi, ci: (bi, 0, 0))],
        out_shape=[jax.ShapeDtypeStruct((n, HB_W), F32),
                   jax.ShapeDtypeStruct((batch, HB_W, HB_W), F32)],
        scratch_shapes=[pltpu.VMEM((HB_W, HB_W), F32)],
        compiler_params=_cparams(("parallel", "arbitrary")),
        name="hgrn",
    )(c_all, lb, hn_g, s0_bd, tri16, eblk)


def _attn_prompt_body(qi_tab, ki_tab, lam_ref, q_ref, k_ref, vt_ref, g_ref, o_ref, qx_ref, m_ref, acc_ref,
                      s_ref, bmax_ref, *, tq, out_scale):
    pair = pl.program_id(2)
    qi = qi_tab[pair]
    ki = ki_tab[pair]
    ns = tq // Q_STRIP
    dv = 2 * HEAD_DIM

    @pl.when(ki == 0)
    def _():
        qt = (q_ref[...] * (HEAD_DIM ** -0.5 * LOG2E)).T
        first = lax.broadcasted_iota(jnp.int32, (dv, 1), 0) < HEAD_DIM
        qa = jnp.where(first, qt, 0.0).astype(BF16)
        qb = jnp.where(first, 0.0, qt).astype(BF16)
        for st in range(ns):
            qx_ref[0, st] = qa[:, st * Q_STRIP:(st + 1) * Q_STRIP]
            qx_ref[1, st] = qb[:, st * Q_STRIP:(st + 1) * Q_STRIP]
        m_ref[...] = jnp.full(m_ref.shape, NEG_BIG, F32)
        acc_ref[...] = jnp.zeros(acc_ref.shape, F32)

    def scores(kb, diag):
        buf = kb % 2
        row0 = kb * Q_STRIP if isinstance(kb, int) else pl.multiple_of(kb * Q_STRIP, Q_STRIP)
        k = k_ref[pl.ds(row0, Q_STRIP), :]
        k_chunk = kb * (Q_STRIP // CHUNK) + lax.broadcasted_iota(jnp.int32, (Q_STRIP, 1), 0) // CHUNK
        for st in range(ns):
            def one_strip(st=st):
                for mp in range(2):
                    s = _dot(k, qx_ref[mp, st])
                    if diag:
                        q_chunk = (st * (Q_STRIP // CHUNK)
                                   + lax.broadcasted_iota(jnp.int32, (1, Q_STRIP), 1) // CHUNK)
                        s = jnp.where(k_chunk <= q_chunk, s, NEG_BIG)
                    s_ref[buf, mp, st] = s
                    bmax_ref[buf, mp, st] = jnp.max(s, axis=0, keepdims=True)
            if diag and not isinstance(kb, int):
                pl.when(st >= kb)(one_strip)
            elif not diag or st >= kb:
                one_strip()

    def accumulate(kb, diag):
        buf = kb % 2
        vt = vt_ref[kb]
        for st in range(ns):
            def one_strip(st=st):
                for mp in range(2):
                    m_old = m_ref[mp, st]
                    m_new = jnp.maximum(m_old, bmax_ref[buf, mp, st])
                    p = jnp.exp2(s_ref[buf, mp, st] - m_new).astype(BF16)
                    corr = jnp.exp2(m_old - m_new)
                    acc_ref[mp, st] = corr * acc_ref[mp, st] + _dot(vt, p)
                    m_ref[mp, st] = m_new
            if diag and not isinstance(kb, int):
                pl.when(st >= kb)(one_strip)
            elif not diag or st >= kb:
                one_strip()

    def key_tile(diag):
        scores(0, diag)

        def body(kb, carry):
            accumulate(kb, diag)
            scores(kb + 1, diag)
            return carry
        for kb in range(ns - 1):
            body(kb, 0)
        accumulate(ns - 1, diag)

    @pl.when(ki < qi)
    def _():
        key_tile(False)

    @pl.when(ki == qi)
    def _():
        key_tile(True)
        lam = lam_ref[0, 0]
        for st in range(ns):
            a1 = acc_ref[0, st]
            a2 = acc_ref[1, st]
            o = a1[:dv] / a1[dv:dv + 1] - lam * (a2[:dv] / a2[dv:dv + 1])
            ms = jnp.mean(o * o, axis=0, keepdims=True)
            o = o * lax.rsqrt(ms + RMS_EPS) * g_ref[...] * out_scale
            o_ref[st * Q_STRIP:(st + 1) * Q_STRIP, :] = o.T


def _attn_prompt(lam, bq, bkb, vt, dn_g, batch, t_len, out_scale):
    tq = _tile(t_len, 1024, Q_STRIP)
    nq = t_len // tq
    ns = tq // Q_STRIP
    n = batch * t_len
    w = 2 * HEAD_DIM
    pairs = [(qi, ki) for qi in range(nq) for ki in range(qi + 1)]
    qi_tab = jnp.asarray([p[0] for p in pairs], jnp.int32)
    ki_tab = jnp.asarray([p[1] for p in pairs], jnp.int32)
    qmap = lambda b, h, p, qt, kt: (b * nq + qt[p], h)
    kmap = lambda b, h, p, qt, kt: (b * nq + kt[p], h)
    vmap_ = lambda b, h, p, qt, kt: (h, b * nq + kt[p], 0, 0)
    grid_spec = pltpu.PrefetchScalarGridSpec(
        num_scalar_prefetch=2,
        grid=(batch, N_HEADS, len(pairs)),
        in_specs=[pl.BlockSpec(memory_space=pltpu.SMEM),
                  pl.BlockSpec((tq, w), qmap), pl.BlockSpec((tq, w), kmap),
                  pl.BlockSpec((None, ns, VT_ROWS, Q_STRIP), vmap_),
                  pl.BlockSpec((w, 1), lambda b, h, p, qt, kt: (0, 0))],
        out_specs=pl.BlockSpec((tq, w), qmap),
        scratch_shapes=[pltpu.VMEM((2, ns, w, Q_STRIP), BF16),
                        pltpu.VMEM((2, ns, 1, Q_STRIP), F32),
                        pltpu.VMEM((2, ns, VT_ROWS, Q_STRIP), F32),
                        pltpu.VMEM((2, 2, ns, Q_STRIP, Q_STRIP), F32),
                        pltpu.VMEM((2, 2, ns, 1, Q_STRIP), F32)])
    return pl.pallas_call(
        functools.partial(_attn_prompt_body, tq=tq, out_scale=out_scale),
        grid_spec=grid_spec,
        out_shape=jax.ShapeDtypeStruct((n, ATT_W), F32),
        compiler_params=_cparams(("parallel", "parallel", "arbitrary")),
        name="attn_prompt",
    )(qi_tab, ki_tab, lam, bq, bkb, vt, dn_g.reshape(w, 1))


def _attn_sample_body(lam_ref, q_ref, kn_ref, vn_ref, kp_ref, vp_ref, g_ref, o_ref, *, t_new, past, out_scale):
    w = 2 * HEAD_DIM
    first = lax.broadcasted_iota(jnp.int32, (1, w), 1) < HEAD_DIM
    for h in range(N_HEADS):
        cols = slice(h * w, (h + 1) * w)
        q = q_ref[:, cols] * (HEAD_DIM ** -0.5)
        q2 = jnp.concatenate([jnp.where(first, q, 0.0), jnp.where(first, 0.0, q)], axis=0).astype(BF16)
        kp = kp_ref[pl.ds(h, past, stride=N_HEADS), :].astype(BF16)
        vp = vp_ref[pl.ds(h, past, stride=N_HEADS), :].astype(BF16)
        s_p = _dot_nt(q2, kp)
        s_n = _dot_nt(q2, kn_ref[:, cols].astype(BF16))
        m = jnp.maximum(jnp.max(s_p, axis=1, keepdims=True), jnp.max(s_n, axis=1, keepdims=True))
        p_p = jnp.exp(s_p - m)
        p_n = jnp.exp(s_n - m)
        l = jnp.sum(p_p, axis=1, keepdims=True) + jnp.sum(p_n, axis=1, keepdims=True)
        acc = _dot(p_p.astype(BF16), vp) + _dot(p_n.astype(BF16), vn_ref[:, cols].astype(BF16))
        o2 = acc / l
        o = o2[:t_new] - lam_ref[0, 0] * o2[t_new:]
        ms = jnp.mean(o * o, axis=1, keepdims=True)
        o_ref[:, cols] = o * lax.rsqrt(ms + RMS_EPS) * g_ref[...] * out_scale


def _attn_sample(lam, bq, bk, bv, cache_k, cache_v, layer, dn_g, batch, t_new, out_scale):
    w = 2 * HEAD_DIM
    depth, _, past = cache_k.shape[:3]
    rows = past * N_HEADS
    new = pl.BlockSpec((t_new, ATT_W), lambda b: (b, 0))
    old = pl.BlockSpec((None, None, rows, w), lambda b: (layer, b, 0, 0))
    return pl.pallas_call(
        functools.partial(_attn_sample_body, t_new=t_new, past=past, out_scale=out_scale),
        grid=(batch,),
        in_specs=[pl.BlockSpec(memory_space=pltpu.SMEM), new, new, new, old, old, _const_spec((1, w))],
        out_specs=new,
        out_shape=jax.ShapeDtypeStruct((batch * t_new, ATT_W), F32),
        compiler_params=_cparams(("parallel",)),
        name="attn_sample",
    )(lam, bq, bk, bv, cache_k.reshape(depth, batch, rows, w), cache_v.reshape(depth, batch, rows, w),
      dn_g.reshape(1, w))


def _to_block_diag(c):
    b = c.shape[0]
    eye = jnp.eye(N_HEADS, dtype=c.dtype)
    return (c[:, :, :, None, :] * eye[None, :, None, :, None]).reshape(b, HB_W, HB_W)


def _from_block_diag(cbd):
    b = cbd.shape[0]
    r = cbd.reshape(b, N_HEADS, HEAD_DIM, N_HEADS, HEAD_DIM)
    return jnp.stack([r[:, h, :, h, :] for h in range(N_HEADS)], axis=1)


def _proj_columns():
    rep = lambda start: [start + h for h in range(N_HEADS) for _ in range(HEAD_DIM)]
    cols = (list(range(A_QK0, A_V0)) + list(range(A_V0, A_I0)) + list(range(A_O0, B_Q0))
            + rep(A_I0) + rep(A_F0)
            + list(range(B_Q0, B_K0)) + list(range(B_K0, B_V0)) + list(range(B_V0, C_Q0))
            + list(range(C_Q0, N_IN)))
    assert len(cols) == N_CAT
    return jnp.asarray(cols, dtype=jnp.int32)


def _consts():
    r = jnp.arange(SCAN_ROWS)
    tri = (r[None, :] <= r[:, None]).astype(BF16)
    tri16 = ((r[None, :] <= r[:, None]) & (r[None, :] // SUB == r[:, None] // SUB)).astype(BF16)
    hh = jnp.arange(HB_W) // HEAD_DIM
    eblk = (hh[None, :] == hh[:, None]).astype(BF16)
    return tri, tri16, eblk


def _group(x, depth, layer_w, states, cache, batch, t_len, alpha, consts):
    tri, tri16, eblk = consts
    prompt = cache is None
    outs = []
    for l in range(depth):
        w = layer_w[l]
        conv0, c0, n0, m0, s0 = states(l)
        x = _ffn_ln(x, w["f_in0"], w["f_out0"], w["ln_g"][0], w["ln_b"][0], alpha)
        res = _in_proj(x, w["w_cat"], w["b_cat"], w["w_vt"], w["b_vt"], prompt)
        proj = dict(zip(_PROJ_OUTS, res[:len(_PROJ_OUTS)]))
        bkb = res[len(_PROJ_OUTS)]

        conv0_p = jnp.pad(conv0, ((0, 0), (8 - (CONV_W - 1), 0), (0, 0)))
        ya, c1, n1, m1, convn = _mlstm(proj, w["conv_w"], w["conv_b"], conv0_p, _to_block_diag(c0),
                                       n0.reshape(batch, 1, HB_W),
                                       jnp.repeat(m0, HEAD_DIM, axis=-1).reshape(batch, 1, HB_W),
                                       tri, eblk, batch, t_len)
        lam_init = 0.8 - 0.6 * math.exp(-0.3 * l)
        if prompt:
            yb = _attn_prompt(w["lam"], proj["bq"], bkb, res[len(_PROJ_OUTS) + 1], w["dn_g"], batch, t_len,
                              1.0 - lam_init)
        else:
            yb = _attn_sample(w["lam"], proj["bq"], proj["bk"], proj["bv"], cache[0], cache[1], l, w["dn_g"],
                              batch, t_len, 1.0 - lam_init)
        yc, s1 = _hgrn(proj["c"], w["lb"], w["hn_g"], _to_block_diag(s0), tri16, eblk, batch, t_len)
        x = _out_proj_ln(ya, yb, yc, x, w["w_out"], w["ln_g"][1], w["ln_b"][1], alpha)
        x = _ffn_ln(x, w["f_in1"], w["f_out1"], w["ln_g"][2], w["ln_b"][2], alpha)
        outs.append((proj["bk"].reshape(batch, t_len, N_HEADS, 2 * HEAD_DIM),
                     proj["bv"].reshape(batch, t_len, N_HEADS, 2 * HEAD_DIM),
                     _from_block_diag(c1),
                     n1.reshape(batch, N_HEADS, HEAD_DIM),
                     m1.reshape(batch, N_HEADS, HEAD_DIM)[:, :, 0],
                     convn[:, 8 - (CONV_W - 1):, :],
                     _from_block_diag(s1)))
    return x, tuple(jnp.stack(a) for a in zip(*outs))


def kernel(x_prompt, x_sample, cache_diff_k, cache_diff_v, state_mlstm_c, state_mlstm_n, state_mlstm_m,
           state_mlstm_conv, state_hgrn_s, w_in, b_in, w_out, mlstm_conv_w, mlstm_conv_b, diff_lambda,
           diff_norm_g, hgrn_lb_logits, hgrn_norm_g, ffn_w_in, ffn_w_out, ln_g, ln_b):
    depth = w_in.shape[0]
    alpha = (2.0 * depth) ** 0.25
    batch, seq, d_model = x_prompt.shape
    dec_batch, dec_seq, _ = x_sample.shape
    past = cache_diff_k.shape[2]

    p_lb = jax.nn.softmax(hgrn_lb_logits.astype(F32), axis=0)
    lower_bounds = jnp.cumsum(p_lb, axis=0) - p_lb[0]
    cols = _proj_columns()
    w_cat = w_in[:, :, cols].astype(BF16)
    b_cat = b_in[:, cols].reshape(depth, 1, N_CAT)
    w_vt = jnp.swapaxes(w_in[:, :, B_V0:C_Q0], 1, 2).reshape(depth, N_HEADS, 2 * HEAD_DIM, d_model)
    w_vt = jnp.pad(w_vt, ((0, 0), (0, 0), (0, VT_ROWS - 2 * HEAD_DIM), (0, 0)))
    w_vt = w_vt.reshape(depth, N_HEADS * VT_ROWS, d_model).astype(BF16)
    b_vt = jnp.pad(b_in[:, B_V0:C_Q0].reshape(depth, N_HEADS, 2 * HEAD_DIM),
                   ((0, 0), (0, 0), (0, VT_ROWS - 2 * HEAD_DIM)))
    b_vt = b_vt.at[:, :, 2 * HEAD_DIM].set(1.0).reshape(depth, N_HEADS * VT_ROWS, 1)
    lp = diff_lambda.astype(F32)
    lam_init = jnp.asarray([0.8 - 0.6 * math.exp(-0.3 * l) for l in range(depth)], F32)
    lam = (jnp.exp(jnp.sum(lp[:, 0] * lp[:, 1], axis=-1)) - jnp.exp(jnp.sum(lp[:, 2] * lp[:, 3], axis=-1))
           + lam_init)
    f_in = ffn_w_in.astype(BF16)
    f_out = ffn_w_out.astype(BF16)
    w_out_b = w_out.astype(BF16)
    layer_w = [dict(w_cat=w_cat[l], b_cat=b_cat[l], w_vt=w_vt[l], b_vt=b_vt[l],
                    conv_w=mlstm_conv_w[l], conv_b=mlstm_conv_b[l].reshape(1, 2 * HB_W),
                    lam=lam[l].reshape(1, 1), dn_g=diff_norm_g[l],
                    lb=lower_bounds[l].reshape(1, HB_W),
                    hn_g=jnp.tile(hgrn_norm_g[l], N_HEADS).reshape(1, HB_W),
                    w_out=w_out_b[l], f_in0=f_in[l, 0], f_out0=f_out[l, 0], f_in1=f_in[l, 1], f_out1=f_out[l, 1],
                    ln_g=ln_g[l], ln_b=ln_b[l]) for l in range(depth)]
    consts = _consts()

    def zero_states(_):
        return (jnp.zeros((batch, CONV_W - 1, 2 * HB_W), F32), jnp.zeros((batch, N_HEADS, HEAD_DIM, HEAD_DIM), F32),
                jnp.zeros((batch, N_HEADS, HEAD_DIM), F32), jnp.zeros((batch, N_HEADS), F32),
                jnp.zeros((batch, N_HEADS, HEAD_DIM, HEAD_DIM), F32))

    def carried_states(l):
        return (state_mlstm_conv[l], state_mlstm_c[l], state_mlstm_n[l], state_mlstm_m[l], state_hgrn_s[l])

    y_p, p_out = _group(x_prompt.reshape(batch * seq, d_model), depth, layer_w, zero_states, None,
                        batch, seq, alpha, consts)
    cache = (cache_diff_k, cache_diff_v)
    y_s, s_out = _group(x_sample.reshape(dec_batch * dec_seq, d_model), depth, layer_w, carried_states, cache,
                        dec_batch, dec_seq, alpha, consts)
    return (y_p.reshape(batch, seq, d_model), y_s.reshape(dec_batch, dec_seq, d_model)) + p_out + s_out
```

```python
import functools
import math

import jax
import jax.numpy as jnp
from jax import lax
from jax.experimental import pallas as pl
from jax.experimental.pallas import tpu as pltpu

F32 = jnp.float32
BF16 = jnp.bfloat16

HEAD_DIM = 64
N_HEADS = 4
CONV_W = 4
CHUNK = 64
HB_W = N_HEADS * HEAD_DIM
ATT_W = N_HEADS * 2 * HEAD_DIM
LN_EPS = 1e-5
RMS_EPS = 1e-6
NEG_BIG = -1e30
SCAN_ROWS = 128
SUB = 16
LOG2E = 1.4426950408889634
VT_ROWS = 2 * HEAD_DIM + 16
Q_STRIP = 256
V7X_VMEM_LIMIT = 56 * 1024 * 1024

A_QK0 = 0
A_V0 = A_QK0 + 2 * HB_W
A_I0 = A_V0 + HB_W
A_F0 = A_I0 + N_HEADS
A_O0 = A_F0 + N_HEADS
B_Q0 = A_O0 + HB_W
B_K0 = B_Q0 + ATT_W
B_V0 = B_K0 + ATT_W
C_Q0 = B_V0 + ATT_W
N_IN = C_Q0 + 4 * HB_W

_SEGS = {}
_off = 0
for _name, _w in (("aqk", 2 * HB_W), ("av", HB_W), ("ao", HB_W), ("gi", HB_W), ("gf", HB_W),
                  ("bq", ATT_W), ("bk", ATT_W), ("bv", ATT_W), ("c", 4 * HB_W)):
    _SEGS[_name] = (_off, _w)
    _off += _w
N_CAT = _off


def _tile(n, pref, mult=8):
    t = min(pref, n)
    while t > mult and (n % t or t % mult):
        t -= mult
    assert n % t == 0 and t % mult == 0, (n, pref, mult)
    return t


def _cparams(sem):
    return pltpu.CompilerParams(dimension_semantics=sem, vmem_limit_bytes=V7X_VMEM_LIMIT)


def _const_spec(shape):
    nd = len(shape)
    return pl.BlockSpec(shape, lambda *_: (0,) * nd)


def _dot(a, b):
    return jnp.dot(a, b, preferred_element_type=F32)


def _dot_nt(a, b):
    return lax.dot_general(a, b, (((1,), (1,)), ((), ())), preferred_element_type=F32)


def _split3(x):
    a = x.astype(BF16)
    r = x - a.astype(F32)
    b = r.astype(BF16)
    c = (r - b.astype(F32)).astype(BF16)
    return a, b, c


def _dot_exact_left(mat01, x):
    a, b, c = _split3(x)
    return _dot(mat01, a) + _dot(mat01, b) + _dot(mat01, c)


def _dot_exact_right(x, mat01):
    a, b, c = _split3(x)
    return _dot(a, mat01) + _dot(b, mat01) + _dot(c, mat01)


def _layer_norm(y, g, b):
    mu = jnp.mean(y, axis=-1, keepdims=True)
    d = y - mu
    var = jnp.mean(d * d, axis=-1, keepdims=True)
    return d * lax.rsqrt(var + LN_EPS) * g + b


def _pad_rows(x, rows):
    if x.shape[0] == rows:
        return x
    return jnp.concatenate([x, jnp.zeros((rows - x.shape[0],) + x.shape[1:], x.dtype)], axis=0)


def _swiglu_ln(x, wi_ref, wo_ref, g_ref, b_ref, acc_ref, alpha, d_ff, tf):
    xb = x.astype(BF16)
    for c in range(d_ff // tf):
        gate = _dot(xb, wi_ref[:, c * tf:(c + 1) * tf])
        up = _dot(xb, wi_ref[:, d_ff + c * tf:d_ff + (c + 1) * tf])
        h = (gate * jax.nn.sigmoid(gate) * up).astype(BF16)
        part = _dot(h, wo_ref[c * tf:(c + 1) * tf, :])
        if c == 0:
            acc_ref[...] = part
        else:
            acc_ref[...] += part
    y = alpha * x + 0.5 * acc_ref[...]
    return _layer_norm(y, g_ref[...], b_ref[...])


def _ffn_ln_body(x_ref, wi_ref, wo_ref, g_ref, b_ref, o_ref, acc_ref, *, alpha, d_ff, tf):
    o_ref[...] = _swiglu_ln(x_ref[...], wi_ref, wo_ref, g_ref, b_ref, acc_ref, alpha, d_ff, tf)


def _ffn_ln(x, wi, wo, g, b, alpha):
    n, d = x.shape
    d_ff = wo.shape[0]
    tm = _tile(n, 512)
    tf = _tile(d_ff, 256, 128)
    return pl.pallas_call(
        functools.partial(_ffn_ln_body, alpha=alpha, d_ff=d_ff, tf=tf),
        grid=(n // tm,),
        in_specs=[pl.BlockSpec((tm, d), lambda i: (i, 0)),
                  _const_spec(wi.shape), _const_spec(wo.shape),
                  _const_spec((1, d)), _const_spec((1, d))],
        out_specs=pl.BlockSpec((tm, d), lambda i: (i, 0)),
        out_shape=jax.ShapeDtypeStruct((n, d), F32),
        scratch_shapes=[pltpu.VMEM((tm, d), F32)],
        compiler_params=_cparams(("parallel",)),
        name="ffn_ln",
    )(x, wi, wo, g.reshape(1, d), b.reshape(1, d))


_PROJ_OUTS = ("aqk", "av", "ao", "gi", "gf", "bq", "bk", "bv", "c")


def _in_proj_body(*refs, emit_vt, n_alias):
    x_ref, w_ref, bias_ref, wvt_ref, bvt_ref = refs[:5]
    out_refs = refs[5 + n_alias:]
    xb = x_ref[...].astype(BF16)
    tm = xb.shape[0]
    outs = dict(zip(_PROJ_OUTS, out_refs))
    for name in _PROJ_OUTS:
        c0, w = _SEGS[name]
        for s in range(0, w, 2 * HB_W):
            e = min(s + 2 * HB_W, w)
            val = _dot(xb, w_ref[:, c0 + s:c0 + e]) + bias_ref[:, c0 + s:c0 + e]
            if name in ("bk", "bv"):
                for h in range(N_HEADS):
                    outs[name][pl.ds(h, tm, stride=N_HEADS), :] = val[:, h * 2 * HEAD_DIM:(h + 1) * 2 * HEAD_DIM]
            else:
                outs[name][:, s:e] = val
            if name == "bk":
                out_refs[len(_PROJ_OUTS)][:, s:e] = val.astype(BF16)
    if emit_vt:
        vt = (_dot_nt(wvt_ref[...], xb) + bvt_ref[...]).astype(BF16)
        for h in range(N_HEADS):
            for j in range(vt.shape[1] // Q_STRIP):
                out_refs[len(_PROJ_OUTS) + 1][h, j] = vt[h * VT_ROWS:(h + 1) * VT_ROWS, j * Q_STRIP:(j + 1) * Q_STRIP]


def _in_proj(x, w_cat, b_cat, w_vt, b_vt, emit_vt, layer, depth, kv_bufs):
    n, d = x.shape
    tm = _tile(n, 512, Q_STRIP if emit_vt else 8)
    out_shape, out_specs = [], []
    for k in _PROJ_OUTS:
        if k in ("bk", "bv"):
            out_shape.append(jax.ShapeDtypeStruct((depth, n * N_HEADS, 2 * HEAD_DIM), F32))
            out_specs.append(pl.BlockSpec((None, tm * N_HEADS, 2 * HEAD_DIM), lambda i: (layer, i, 0)))
        else:
            out_shape.append(jax.ShapeDtypeStruct((n, _SEGS[k][1]), F32))
            out_specs.append(pl.BlockSpec((tm, _SEGS[k][1]), lambda i: (i, 0)))
    out_shape.append(jax.ShapeDtypeStruct((n, ATT_W), BF16))
    out_specs.append(pl.BlockSpec((tm, ATT_W), lambda i: (i, 0)))
    if emit_vt:
        out_shape.append(jax.ShapeDtypeStruct((N_HEADS, n // Q_STRIP, VT_ROWS, Q_STRIP), BF16))
        out_specs.append(pl.BlockSpec((N_HEADS, tm // Q_STRIP, VT_ROWS, Q_STRIP), lambda i: (0, i, 0, 0)))
    in_specs = [pl.BlockSpec((tm, d), lambda i: (i, 0)),
                _const_spec(w_cat.shape), _const_spec(b_cat.shape),
                _const_spec(w_vt.shape), _const_spec(b_vt.shape)]
    args = [x, w_cat, b_cat, w_vt, b_vt]
    aliases = {}
    if kv_bufs is not None:
        for buf, name in zip(kv_bufs, ("bk", "bv")):
            aliases[len(args)] = _PROJ_OUTS.index(name)
            in_specs.append(pl.BlockSpec(memory_space=pl.ANY))
            args.append(buf)
    return pl.pallas_call(
        functools.partial(_in_proj_body, emit_vt=emit_vt, n_alias=len(aliases)),
        grid=(n // tm,),
        in_specs=in_specs,
        out_specs=out_specs,
        out_shape=out_shape,
        input_output_aliases=aliases,
        compiler_params=_cparams(("parallel",)),
        name="in_proj",
    )(*args)


def _out_proj_ffn_body(ya_ref, yb_ref, yc_ref, x_ref, w_ref, g1_ref, b1_ref, wi_ref, wo_ref, g2_ref, b2_ref,
                       o_ref, acc_ref, *, alpha, d_ff, tf):
    y = _dot(ya_ref[...].astype(BF16), w_ref[0:HB_W, :])
    y += _dot(yb_ref[...].astype(BF16), w_ref[HB_W:HB_W + ATT_W, :])
    y += _dot(yc_ref[...].astype(BF16), w_ref[HB_W + ATT_W:, :])
    x = _layer_norm(alpha * x_ref[...] + y, g1_ref[...], b1_ref[...])
    o_ref[...] = _swiglu_ln(x, wi_ref, wo_ref, g2_ref, b2_ref, acc_ref, alpha, d_ff, tf)


def _out_proj_ffn(ya, yb, yc, x, w, g1, b1, wi, wo, g2, b2, alpha):
    n, d = x.shape
    d_ff = wo.shape[0]
    tm = _tile(n, 512)
    tf = _tile(d_ff, 256, 128)
    row = lambda width: pl.BlockSpec((tm, width), lambda i: (i, 0))
    vec = _const_spec((1, d))
    return pl.pallas_call(
        functools.partial(_out_proj_ffn_body, alpha=alpha, d_ff=d_ff, tf=tf),
        grid=(n // tm,),
        in_specs=[row(HB_W), row(ATT_W), row(HB_W), row(d), _const_spec(w.shape), vec, vec,
                  _const_spec(wi.shape), _const_spec(wo.shape), vec, vec],
        out_specs=row(d),
        out_shape=jax.ShapeDtypeStruct((n, d), F32),
        scratch_shapes=[pltpu.VMEM((tm, d), F32)],
        compiler_params=_cparams(("parallel",)),
        name="out_proj_ffn",
    )(ya, yb, yc, x, w, g1.reshape(1, d), b1.reshape(1, d), wi, wo, g2.reshape(1, d), b2.reshape(1, d))


def _mlstm_body(qk_ref, v_ref, o_ref, gi_ref, gf_ref, cw_ref, cb_ref, conv0_ref, c0_ref, n0_ref, m0_ref,
                tri_ref, eblk_ref,
                ya_ref, c1_ref, n1_ref, m1_ref, convn_ref,
                up_ref, c_ref, n_ref, m_ref, *, rows):
    step = pl.program_id(1)
    L = SCAN_ROWS

    @pl.when(step == 0)
    def _():
        up_ref[0:8, :] = conv0_ref[...]
        c_ref[...] = c0_ref[...]
        n_ref[...] = n0_ref[...]
        m_ref[...] = m0_ref[...]

    u = _pad_rows(qk_ref[...], L)
    up_ref[8:8 + L, :] = u
    y = cb_ref[...]
    for j in range(CONV_W):
        y = y + cw_ref[j:j + 1, :] * up_ref[5 + j:5 + j + L, :]
    convn_ref[...] = up_ref[rows:rows + 8, :]
    up_ref[0:8, :] = up_ref[L:L + 8, :]
    qk = y * jax.nn.sigmoid(y)
    q = qk[:, :HB_W] * (HEAD_DIM ** -0.5)
    k = qk[:, HB_W:]
    v = _pad_rows(v_ref[...], L)
    kb = k.astype(BF16)
    vb = v.astype(BF16)

    gi = _pad_rows(gi_ref[...], L)
    flog = jax.nn.log_sigmoid(_pad_rows(gf_ref[...], L))
    if rows < L:
        live = lax.broadcasted_iota(jnp.int32, (L, 1), 0) < rows
        gi = jnp.where(live, gi, NEG_BIG)
        flog = jnp.where(live, flog, 0.0)
    b = _dot_exact_left(tri_ref[...], flog)
    b_t = b.T
    gi_t = gi.T
    m_prev = m_ref[...]
    g = b + m_prev

    lane_head = lax.broadcasted_iota(jnp.int32, (1, HB_W), 1) // HEAD_DIM
    t_idx = lax.broadcasted_iota(jnp.int32, (L, L), 0)
    s_idx = lax.broadcasted_iota(jnp.int32, (L, L), 1)
    causal = s_idx <= t_idx

    mrow_hb = jnp.zeros((L, HB_W), F32)
    wg_hb = jnp.zeros((L, HB_W), F32)
    den_hb = jnp.zeros((L, HB_W), F32)
    num_hb = jnp.zeros((L, HB_W), F32)
    for h in range(N_HEADS):
        sel = lane_head == h
        c0 = h * HEAD_DIM
        dmat = b[:, c0:c0 + 1] - b_t[c0:c0 + 1, :] + gi_t[c0:c0 + 1, :]
        dmat = jnp.where(causal, dmat, NEG_BIG)
        gcol = g[:, c0:c0 + 1]
        mrow = jnp.maximum(gcol, jnp.max(dmat, axis=1, keepdims=True))
        wd = jnp.exp(dmat - mrow)
        wg = jnp.exp(gcol - mrow)
        qh = jnp.where(sel, q, 0.0).astype(BF16)
        qkw = _dot_nt(qh, kb) * wd
        den = jnp.sum(qkw, axis=1, keepdims=True)
        num = _dot(qkw.astype(BF16), vb)
        mrow_hb = jnp.where(sel, mrow, mrow_hb)
        wg_hb = jnp.where(sel, wg, wg_hb)
        den_hb = jnp.where(sel, den, den_hb)
        num_hb = jnp.where(sel, num, num_hb)

    c_prev = c_ref[...]
    n_prev = n_ref[...]
    eblk = eblk_ref[...]
    qb = q.astype(BF16)
    num_hb = wg_hb * _dot(qb, c_prev.astype(BF16)) + num_hb
    den_hb = wg_hb * _dot((q * n_prev).astype(BF16), eblk) + den_hb
    hout = num_hb / jnp.maximum(jnp.abs(den_hb), jnp.exp(-mrow_hb))
    ya = jax.nn.sigmoid(_pad_rows(o_ref[...], L)) * hout
    ya_ref[...] = ya[:rows]

    m_new = mrow_hb[L - 1:L, :]
    b_last = b[L - 1:L, :]
    decay = jnp.exp(b_last + m_prev - m_new)
    ws = jnp.exp(b_last - b + gi - m_new)
    kw = k * ws
    upd = _dot(kw.T.astype(BF16), vb) * eblk.astype(F32)
    c_new = decay * c_prev + upd
    n_new = decay * n_prev + jnp.sum(kw, axis=0, keepdims=True)
    c_ref[...] = c_new
    n_ref[...] = n_new
    m_ref[...] = m_new

    @pl.when(step == pl.num_programs(1) - 1)
    def _():
        c1_ref[...] = c_new
        n1_ref[...] = n_new
        m1_ref[...] = m_new


def _mlstm(proj, conv_w, conv_b, conv0, c0_bd, n0, m0_hb, tri, eblk, batch, t_len):
    rows = min(SCAN_ROWS, t_len)
    steps = t_len // rows
    n = batch * t_len
    tok = lambda width: pl.BlockSpec((rows, width), lambda bi, ci: (bi * steps + ci, 0))
    per_b = lambda shape: pl.BlockSpec((None,) + shape, lambda bi, ci: (bi,) + (0,) * len(shape))
    return pl.pallas_call(
        functools.partial(_mlstm_body, rows=rows),
        grid=(batch, steps),
        in_specs=[tok(2 * HB_W), tok(HB_W), tok(HB_W), tok(HB_W), tok(HB_W),
                  _const_spec(conv_w.shape), _const_spec(conv_b.shape),
                  per_b((8, 2 * HB_W)), per_b((HB_W, HB_W)), per_b((1, HB_W)), per_b((1, HB_W)),
                  _const_spec(tri.shape), _const_spec(eblk.shape)],
        out_specs=[tok(HB_W), per_b((HB_W, HB_W)), per_b((1, HB_W)), per_b((1, HB_W)), per_b((8, 2 * HB_W))],
        out_shape=[jax.ShapeDtypeStruct((n, HB_W), F32),
                   jax.ShapeDtypeStruct((batch, HB_W, HB_W), F32),
                   jax.ShapeDtypeStruct((batch, 1, HB_W), F32),
                   jax.ShapeDtypeStruct((batch, 1, HB_W), F32),
                   jax.ShapeDtypeStruct((batch, 8, 2 * HB_W), F32)],
        scratch_shapes=[pltpu.VMEM((SCAN_ROWS + 8, 2 * HB_W), F32),
                        pltpu.VMEM((HB_W, HB_W), F32),
                        pltpu.VMEM((1, HB_W), F32),
                        pltpu.VMEM((1, HB_W), F32)],
        compiler_params=_cparams(("parallel", "arbitrary")),
        name="mlstm",
    )(proj["aqk"], proj["av"], proj["ao"], proj["gi"], proj["gf"], conv_w, conv_b, conv0, c0_bd, n0, m0_hb,
      tri, eblk)


def _hgrn_body(c_ref, lb_ref, hng_ref, s0_ref, tri16_ref, eblk_ref, esel_ref, erep_ref, yc_ref, s1_ref, s_ref,
               *, rows):
    step = pl.program_id(1)
    L = SCAN_ROWS

    @pl.when(step == 0)
    def _():
        s_ref[...] = s0_ref[...]

    cin = _pad_rows(c_ref[...], L)
    q = cin[:, 0:HB_W]
    f_pre = cin[:, HB_W:2 * HB_W]
    v = cin[:, 2 * HB_W:3 * HB_W]
    gate = cin[:, 3 * HB_W:]
    lb = lb_ref[...]
    forget = lb + (1.0 - lb) * jax.nn.sigmoid(f_pre)
    logf = jnp.log(forget)
    key = 1.0 - forget
    if rows < L:
        live = lax.broadcasted_iota(jnp.int32, (L, 1), 0) < rows
        logf = jnp.where(live, logf, 0.0)
        key = jnp.where(live, key, 0.0)
    bl = _dot_exact_left(tri16_ref[...], logf) * LOG2E
    eblk = eblk_ref[...]
    groups = L // SUB
    t_loc = lax.broadcasted_iota(jnp.int32, (L, 1), 0) % SUB
    vb = v.astype(BF16)

    def from_source(x, s):
        picked = x.reshape(groups, SUB, HB_W)[:, s:s + 1, :]
        return jnp.broadcast_to(picked, (groups, SUB, HB_W)).reshape(L, HB_W)

    n_sub = max(rows // SUB, 1)
    qd = (q * jnp.exp2(bl)).astype(BF16)
    bl_t = bl.T
    low = lax.broadcasted_iota(jnp.int32, (1, 2 * HEAD_DIM), 1) < HEAD_DIM
    zero_b = jnp.zeros((HEAD_DIM, 2 * HEAD_DIM), BF16)

    def contribution(j):
        r0 = j * SUB
        b_last = bl[r0 + SUB - 1:r0 + SUB, :]
        kw_j = (key[r0:r0 + SUB] * jnp.exp2(b_last - bl[r0:r0 + SUB])).astype(BF16)
        upd = lax.dot_general(kw_j, vb[r0:r0 + SUB], (((0,), (0,)), ((), ())),
                              preferred_element_type=F32)
        dec = jnp.exp2(bl_t[:, r0 + SUB - 1:r0 + SUB])
        pair = []
        for i in range(2):
            rk = 2 * i * HEAD_DIM
            u = jnp.where(low, upd[rk:rk + HEAD_DIM, rk:rk + 2 * HEAD_DIM],
                          upd[rk + HEAD_DIM:rk + 2 * HEAD_DIM, rk:rk + 2 * HEAD_DIM])
            d = jnp.where(low, dec[rk:rk + HEAD_DIM], dec[rk + HEAD_DIM:rk + 2 * HEAD_DIM])
            pair.append((d, u))
        return pair

    contrib = []
    att = jnp.zeros((L, 2 * HEAD_DIM), F32)
    for s in range(SUB):
        w = jnp.exp2(bl - from_source(bl, s))
        p = jnp.where(t_loc >= s, q * from_source(key, s) * w, 0.0)
        att = att + _dot(p.astype(BF16), esel_ref[s])
        if s % 2 == 1 and s // 2 < n_sub:
            contrib.append(contribution(s // 2))
    attb = att.astype(BF16)
    same_sub = (lax.broadcasted_iota(jnp.int32, (L, L), 0) // SUB
                == lax.broadcasted_iota(jnp.int32, (L, L), 1) // SUB)
    lane_head = lax.broadcasted_iota(jnp.int32, (1, HB_W), 1) // HEAD_DIM
    o = jnp.zeros((L, HB_W), F32)
    for h in range(N_HEADS):
        full = jnp.where(same_sub, _dot(attb, erep_ref[h]), 0.0)
        o = jnp.where(lane_head == h, _dot(full.astype(BF16), vb), o)

    pk = [s_ref[0], s_ref[1]]
    o_parts = []
    for j in range(n_sub):
        r0 = j * SUB
        pb = [x.astype(BF16) for x in pk]
        s_bd = jnp.concatenate([
            jnp.concatenate([jnp.where(low, pb[0], 0.0).astype(BF16), zero_b], axis=1),
            jnp.concatenate([jnp.where(low, 0.0, pb[0]).astype(BF16), zero_b], axis=1),
            jnp.concatenate([zero_b, jnp.where(low, pb[1], 0.0).astype(BF16)], axis=1),
            jnp.concatenate([zero_b, jnp.where(low, 0.0, pb[1]).astype(BF16)], axis=1)], axis=0)
        o_parts.append(_dot(qd[r0:r0 + SUB], s_bd))
        pk = [d * p + u for p, (d, u) in zip(pk, contrib[j])]
    s_ref[0] = pk[0]
    s_ref[1] = pk[1]
    o_inter = jnp.concatenate(o_parts, axis=0)
    if rows < L:
        o_inter = _pad_rows(o_inter, L)
    o = o + o_inter

    ms = _dot_exact_right(o * o, eblk) * (1.0 / HEAD_DIM)
    yc = o * lax.rsqrt(ms + RMS_EPS) * hng_ref[...] * (gate * jax.nn.sigmoid(gate))
    yc_ref[...] = yc[:rows]

    @pl.when(step == pl.num_programs(1) - 1)
    def _():
        s1_ref[0] = pk[0]
        s1_ref[1] = pk[1]


def _hgrn(c_all, lb, hn_g, s0_pk, tri16, eblk, esel, erep, batch, t_len):
    rows = min(SCAN_ROWS, t_len)
    steps = t_len // rows
    n = batch * t_len
    pk_spec = pl.BlockSpec((None, 2, HEAD_DIM, 2 * HEAD_DIM), lambda bi, ci: (bi, 0, 0, 0))
    return pl.pallas_call(
        functools.partial(_hgrn_body, rows=rows),
        grid=(batch, steps),
        in_specs=[pl.BlockSpec((rows, 4 * HB_W), lambda bi, ci: (bi * steps + ci, 0)),
                  _const_spec((1, HB_W)), _const_spec((1, HB_W)), pk_spec,
                  _const_spec(tri16.shape), _const_spec(eblk.shape), _const_spec(esel.shape),
                  _const_spec(erep.shape)],
        out_specs=[pl.BlockSpec((rows, HB_W), lambda bi, ci: (bi * steps + ci, 0)), pk_spec],
        out_shape=[jax.ShapeDtypeStruct((n, HB_W), F32),
                   jax.ShapeDtypeStruct((batch, 2, HEAD_DIM, 2 * HEAD_DIM), F32)],
        scratch_shapes=[pltpu.VMEM((2, HEAD_DIM, 2 * HEAD_DIM), F32)],
        compiler_params=_cparams(("parallel", "arbitrary")),
        name="hgrn",
    )(c_all, lb, hn_g, s0_pk, tri16, eblk, esel, erep)


def _attn_prompt_body(qi_tab, ki_tab, lam_ref, q_ref, k_ref, vt_ref, g_ref, o_ref, qx_ref, m_ref, acc_ref,
                      s_ref, bmax_ref, *, tq, out_scale):
    pair = pl.program_id(2)
    qi = qi_tab[pair]
    ki = ki_tab[pair]
    ns = tq // Q_STRIP
    dv = 2 * HEAD_DIM

    @pl.when(ki == 0)
    def _():
        qt = (q_ref[...] * (HEAD_DIM ** -0.5 * LOG2E)).T
        first = lax.broadcasted_iota(jnp.int32, (dv, 1), 0) < HEAD_DIM
        qa = jnp.where(first, qt, 0.0).astype(BF16)
        qb = jnp.where(first, 0.0, qt).astype(BF16)
        for st in range(ns):
            qx_ref[0, st] = qa[:, st * Q_STRIP:(st + 1) * Q_STRIP]
            qx_ref[1, st] = qb[:, st * Q_STRIP:(st + 1) * Q_STRIP]
        m_ref[...] = jnp.full(m_ref.shape, NEG_BIG, F32)
        acc_ref[...] = jnp.zeros(acc_ref.shape, F32)

    def scores(kb, diag):
        buf = kb % 2
        row0 = kb * Q_STRIP if isinstance(kb, int) else pl.multiple_of(kb * Q_STRIP, Q_STRIP)
        k = k_ref[pl.ds(row0, Q_STRIP), :]
        k_chunk = kb * (Q_STRIP // CHUNK) + lax.broadcasted_iota(jnp.int32, (Q_STRIP, 1), 0) // CHUNK
        for st in range(ns):
            def one_strip(st=st):
                for mp in range(2):
                    s = _dot(k, qx_ref[mp, st])
                    if diag:
                        q_chunk = (st * (Q_STRIP // CHUNK)
                                   + lax.broadcasted_iota(jnp.int32, (1, Q_STRIP), 1) // CHUNK)
                        s = jnp.where(k_chunk <= q_chunk, s, NEG_BIG)
                    s_ref[buf, mp, st] = s
                    bmax_ref[buf, mp, st] = jnp.max(s, axis=0, keepdims=True)
            if diag and not isinstance(kb, int):
                pl.when(st >= kb)(one_strip)
            elif not diag or st >= kb:
                one_strip()

    def accumulate(kb, diag):
        buf = kb % 2
        vt = vt_ref[kb]
        for st in range(ns):
            def one_strip(st=st):
                for mp in range(2):
                    m_old = m_ref[mp, st]
                    m_new = jnp.maximum(m_old, bmax_ref[buf, mp, st])
                    p = jnp.exp2(s_ref[buf, mp, st] - m_new).astype(BF16)
                    corr = jnp.exp2(m_old - m_new)
                    acc_ref[mp, st] = corr * acc_ref[mp, st] + _dot(vt, p)
                    m_ref[mp, st] = m_new
            if diag and not isinstance(kb, int):
                pl.when(st >= kb)(one_strip)
            elif not diag or st >= kb:
                one_strip()

    def key_tile(diag):
        scores(0, diag)

        def body(kb, carry):
            accumulate(kb, diag)
            scores(kb + 1, diag)
            return carry
        for kb in range(ns - 1):
            body(kb, 0)
        accumulate(ns - 1, diag)

    @pl.when(ki < qi)
    def _():
        key_tile(False)

    @pl.when(ki == qi)
    def _():
        key_tile(True)
        lam = lam_ref[0, 0]
        for st in range(ns):
            a1 = acc_ref[0, st]
            a2 = acc_ref[1, st]
            o = a1[:dv] / a1[dv:dv + 1] - lam * (a2[:dv] / a2[dv:dv + 1])
            ms = jnp.mean(o * o, axis=0, keepdims=True)
            o = o * lax.rsqrt(ms + RMS_EPS) * g_ref[...] * out_scale
            o_ref[st * Q_STRIP:(st + 1) * Q_STRIP, :] = o.T


def _attn_prompt(lam, bq, bkb, vt, dn_g, batch, t_len, out_scale):
    tq = _tile(t_len, 1024, Q_STRIP)
    nq = t_len // tq
    ns = tq // Q_STRIP
    n = batch * t_len
    w = 2 * HEAD_DIM
    pairs = [(qi, ki) for qi in range(nq) for ki in range(qi + 1)]
    qi_tab = jnp.asarray([p[0] for p in pairs], jnp.int32)
    ki_tab = jnp.asarray([p[1] for p in pairs], jnp.int32)
    qmap = lambda b, h, p, qt, kt: (b * nq + qt[p], h)
    kmap = lambda b, h, p, qt, kt: (b * nq + kt[p], h)
    vmap_ = lambda b, h, p, qt, kt: (h, b * nq + kt[p], 0, 0)
    grid_spec = pltpu.PrefetchScalarGridSpec(
        num_scalar_prefetch=2,
        grid=(batch, N_HEADS, len(pairs)),
        in_specs=[pl.BlockSpec(memory_space=pltpu.SMEM),
                  pl.BlockSpec((tq, w), qmap), pl.BlockSpec((tq, w), kmap),
                  pl.BlockSpec((None, ns, VT_ROWS, Q_STRIP), vmap_),
                  pl.BlockSpec((w, 1), lambda b, h, p, qt, kt: (0, 0))],
        out_specs=pl.BlockSpec((tq, w), qmap),
        scratch_shapes=[pltpu.VMEM((2, ns, w, Q_STRIP), BF16),
                        pltpu.VMEM((2, ns, 1, Q_STRIP), F32),
                        pltpu.VMEM((2, ns, VT_ROWS, Q_STRIP), F32),
                        pltpu.VMEM((2, 2, ns, Q_STRIP, Q_STRIP), F32),
                        pltpu.VMEM((2, 2, ns, 1, Q_STRIP), F32)])
    return pl.pallas_call(
        functools.partial(_attn_prompt_body, tq=tq, out_scale=out_scale),
        grid_spec=grid_spec,
        out_shape=jax.ShapeDtypeStruct((n, ATT_W), F32),
        compiler_params=_cparams(("parallel", "parallel", "arbitrary")),
        name="attn_prompt",
    )(qi_tab, ki_tab, lam, bq, bkb, vt, dn_g.reshape(w, 1))


def _attn_sample_body(lam_ref, q_ref, kn_ref, vn_ref, kp_ref, vp_ref, g_ref, o_ref, *, t_new, past, out_scale):
    w = 2 * HEAD_DIM
    first = lax.broadcasted_iota(jnp.int32, (1, w), 1) < HEAD_DIM
    for h in range(N_HEADS):
        cols = slice(h * w, (h + 1) * w)
        q = q_ref[:, cols] * (HEAD_DIM ** -0.5)
        q2 = jnp.concatenate([jnp.where(first, q, 0.0), jnp.where(first, 0.0, q)], axis=0).astype(BF16)
        kp = kp_ref[pl.ds(h, past, stride=N_HEADS), :].astype(BF16)
        vp = vp_ref[pl.ds(h, past, stride=N_HEADS), :].astype(BF16)
        s_p = _dot_nt(q2, kp)
        kn = kn_ref[pl.ds(h, t_new, stride=N_HEADS), :].astype(BF16)
        vn = vn_ref[pl.ds(h, t_new, stride=N_HEADS), :].astype(BF16)
        s_n = _dot_nt(q2, kn)
        m = jnp.maximum(jnp.max(s_p, axis=1, keepdims=True), jnp.max(s_n, axis=1, keepdims=True))
        p_p = jnp.exp(s_p - m)
        p_n = jnp.exp(s_n - m)
        l = jnp.sum(p_p, axis=1, keepdims=True) + jnp.sum(p_n, axis=1, keepdims=True)
        acc = _dot(p_p.astype(BF16), vp) + _dot(p_n.astype(BF16), vn)
        o2 = acc / l
        o = o2[:t_new] - lam_ref[0, 0] * o2[t_new:]
        ms = jnp.mean(o * o, axis=1, keepdims=True)
        o_ref[:, cols] = o * lax.rsqrt(ms + RMS_EPS) * g_ref[...] * out_scale


def _attn_sample(lam, bq, bk, bv, cache_k, cache_v, layer, dn_g, batch, t_new, out_scale):
    w = 2 * HEAD_DIM
    depth, _, past = cache_k.shape[:3]
    rows = past * N_HEADS
    new = pl.BlockSpec((t_new, ATT_W), lambda b: (b, 0))
    new_kv = pl.BlockSpec((None, t_new * N_HEADS, w), lambda b: (layer, b, 0))
    old = pl.BlockSpec((None, None, rows, w), lambda b: (layer, b, 0, 0))
    return pl.pallas_call(
        functools.partial(_attn_sample_body, t_new=t_new, past=past, out_scale=out_scale),
        grid=(batch,),
        in_specs=[pl.BlockSpec(memory_space=pltpu.SMEM), new, new_kv, new_kv, old, old, _const_spec((1, w))],
        out_specs=new,
        out_shape=jax.ShapeDtypeStruct((batch * t_new, ATT_W), F32),
        compiler_params=_cparams(("parallel",)),
        name="attn_sample",
    )(lam, bq, bk, bv, cache_k.reshape(depth, batch, rows, w), cache_v.reshape(depth, batch, rows, w),
      dn_g.reshape(1, w))


def _to_block_diag(c):
    b = c.shape[0]
    eye = jnp.eye(N_HEADS, dtype=c.dtype)
    return (c[:, :, :, None, :] * eye[None, :, None, :, None]).reshape(b, HB_W, HB_W)


def _from_block_diag(cbd):
    b = cbd.shape[0]
    r = cbd.reshape(b, N_HEADS, HEAD_DIM, N_HEADS, HEAD_DIM)
    return jnp.stack([r[:, h, :, h, :] for h in range(N_HEADS)], axis=1)


def _to_head_pairs(s):
    b = s.shape[0]
    r = s.reshape(b, N_HEADS // 2, 2, HEAD_DIM, HEAD_DIM)
    return jnp.swapaxes(r, 2, 3).reshape(b, N_HEADS // 2, HEAD_DIM, 2 * HEAD_DIM)


def _from_head_pairs(p):
    b = p.shape[0]
    r = p.reshape(b, N_HEADS // 2, HEAD_DIM, 2, HEAD_DIM)
    return jnp.swapaxes(r, 2, 3).reshape(b, N_HEADS, HEAD_DIM, HEAD_DIM)


def _rearrange_proj(a):
    gates = lambda c0: jnp.repeat(a[..., c0:c0 + N_HEADS], HEAD_DIM, axis=-1)
    out = jnp.concatenate([a[..., A_QK0:A_I0], a[..., A_O0:B_Q0], gates(A_I0), gates(A_F0), a[..., B_Q0:N_IN]],
                          axis=-1)
    assert out.shape[-1] == N_CAT
    return out


def _consts():
    r = jnp.arange(SCAN_ROWS)
    tri = (r[None, :] <= r[:, None]).astype(BF16)
    tri16 = ((r[None, :] <= r[:, None]) & (r[None, :] // SUB == r[:, None] // SUB)).astype(BF16)
    hh = jnp.arange(HB_W) // HEAD_DIM
    eblk = (hh[None, :] == hh[:, None]).astype(BF16)
    lane = jnp.arange(2 * HEAD_DIM)
    src = jnp.arange(SUB)
    esel = ((hh[None, :, None] * SUB + src[:, None, None]) == lane[None, None, :]).astype(BF16)
    erep = ((lane[None, :, None] // SUB == jnp.arange(N_HEADS)[:, None, None])
            & (lane[None, :, None] % SUB == r[None, None, :] % SUB)).astype(BF16)
    return tri, tri16, eblk, esel, erep


def _group(x, depth, layer_w, states, cache, batch, t_len, alpha, consts):
    tri, tri16, eblk, esel, erep = consts
    prompt = cache is None
    outs = []
    kv_bufs = None
    for l in range(depth):
        w = layer_w[l]
        conv0, c0, n0, m0, s0 = states(l)
        x = _ffn_ln(x, w["f_in0"], w["f_out0"], w["ln_g"][0], w["ln_b"][0], alpha)
        res = _in_proj(x, w["w_cat"], w["b_cat"], w["w_vt"], w["b_vt"], prompt, l, depth, kv_bufs)
        proj = dict(zip(_PROJ_OUTS, res[:len(_PROJ_OUTS)]))
        bkb = res[len(_PROJ_OUTS)]
        kv_bufs = (proj["bk"], proj["bv"])

        conv0_p = jnp.pad(conv0, ((0, 0), (8 - (CONV_W - 1), 0), (0, 0)))
        ya, c1, n1, m1, convn = _mlstm(proj, w["conv_w"], w["conv_b"], conv0_p, _to_block_diag(c0),
                                       n0.reshape(batch, 1, HB_W),
                                       jnp.repeat(m0, HEAD_DIM, axis=-1).reshape(batch, 1, HB_W),
                                       tri, eblk, batch, t_len)
        lam_init = 0.8 - 0.6 * math.exp(-0.3 * l)
        if prompt:
            yb = _attn_prompt(w["lam"], proj["bq"], bkb, res[len(_PROJ_OUTS) + 1], w["dn_g"], batch, t_len,
                              1.0 - lam_init)
        else:
            yb = _attn_sample(w["lam"], proj["bq"], proj["bk"], proj["bv"], cache[0], cache[1], l, w["dn_g"],
                              batch, t_len, 1.0 - lam_init)
        yc, s1 = _hgrn(proj["c"], w["lb"], w["hn_g"], _to_head_pairs(s0), tri16, eblk, esel, erep, batch, t_len)
        x = _out_proj_ffn(ya, yb, yc, x, w["w_out"], w["ln_g"][1], w["ln_b"][1],
                          w["f_in1"], w["f_out1"], w["ln_g"][2], w["ln_b"][2], alpha)
        outs.append((_from_block_diag(c1),
                     n1.reshape(batch, N_HEADS, HEAD_DIM),
                     m1.reshape(batch, N_HEADS, HEAD_DIM)[:, :, 0],
                     convn[:, 8 - (CONV_W - 1):, :],
                     _from_head_pairs(s1)))
    kv = tuple(buf.reshape(depth, batch, t_len, N_HEADS, 2 * HEAD_DIM) for buf in kv_bufs)
    return x, kv + tuple(jnp.stack(a) for a in zip(*outs))


def kernel(x_prompt, x_sample, cache_diff_k, cache_diff_v, state_mlstm_c, state_mlstm_n, state_mlstm_m,
           state_mlstm_conv, state_hgrn_s, w_in, b_in, w_out, mlstm_conv_w, mlstm_conv_b, diff_lambda,
           diff_norm_g, hgrn_lb_logits, hgrn_norm_g, ffn_w_in, ffn_w_out, ln_g, ln_b):
    depth = w_in.shape[0]
    alpha = (2.0 * depth) ** 0.25
    batch, seq, d_model = x_prompt.shape
    dec_batch, dec_seq, _ = x_sample.shape
    past = cache_diff_k.shape[2]

    p_lb = jax.nn.softmax(hgrn_lb_logits.astype(F32), axis=0)
    lower_bounds = jnp.cumsum(p_lb, axis=0) - p_lb[0]
    w_cat = _rearrange_proj(w_in.astype(BF16))
    b_cat = _rearrange_proj(b_in).reshape(depth, 1, N_CAT)
    w_vt = jnp.swapaxes(w_in[:, :, B_V0:C_Q0], 1, 2).reshape(depth, N_HEADS, 2 * HEAD_DIM, d_model)
    w_vt = jnp.pad(w_vt, ((0, 0), (0, 0), (0, VT_ROWS - 2 * HEAD_DIM), (0, 0)))
    w_vt = w_vt.reshape(depth, N_HEADS * VT_ROWS, d_model).astype(BF16)
    b_vt = jnp.pad(b_in[:, B_V0:C_Q0].reshape(depth, N_HEADS, 2 * HEAD_DIM),
                   ((0, 0), (0, 0), (0, VT_ROWS - 2 * HEAD_DIM)))
    b_vt = b_vt.at[:, :, 2 * HEAD_DIM].set(1.0).reshape(depth, N_HEADS * VT_ROWS, 1)
    lp = diff_lambda.astype(F32)
    lam_init = jnp.asarray([0.8 - 0.6 * math.exp(-0.3 * l) for l in range(depth)], F32)
    lam = (jnp.exp(jnp.sum(lp[:, 0] * lp[:, 1], axis=-1)) - jnp.exp(jnp.sum(lp[:, 2] * lp[:, 3], axis=-1))
           + lam_init)
    f_in = ffn_w_in.astype(BF16)
    f_out = ffn_w_out.astype(BF16)
    w_out_b = w_out.astype(BF16)
    layer_w = [dict(w_cat=w_cat[l], b_cat=b_cat[l], w_vt=w_vt[l], b_vt=b_vt[l],
                    conv_w=mlstm_conv_w[l], conv_b=mlstm_conv_b[l].reshape(1, 2 * HB_W),
                    lam=lam[l].reshape(1, 1), dn_g=diff_norm_g[l],
                    lb=lower_bounds[l].reshape(1, HB_W),
                    hn_g=jnp.tile(hgrn_norm_g[l], N_HEADS).reshape(1, HB_W),
                    w_out=w_out_b[l], f_in0=f_in[l, 0], f_out0=f_out[l, 0], f_in1=f_in[l, 1], f_out1=f_out[l, 1],
                    ln_g=ln_g[l], ln_b=ln_b[l]) for l in range(depth)]
    consts = _consts()

    def zero_states(_):
        return (jnp.zeros((batch, CONV_W - 1, 2 * HB_W), F32), jnp.zeros((batch, N_HEADS, HEAD_DIM, HEAD_DIM), F32),
                jnp.zeros((batch, N_HEADS, HEAD_DIM), F32), jnp.zeros((batch, N_HEADS), F32),
                jnp.zeros((batch, N_HEADS, HEAD_DIM, HEAD_DIM), F32))

    def carried_states(l):
        return (state_mlstm_conv[l], state_mlstm_c[l], state_mlstm_n[l], state_mlstm_m[l], state_hgrn_s[l])

    y_p, p_out = _group(x_prompt.reshape(batch * seq, d_model), depth, layer_w, zero_states, None,
                        batch, seq, alpha, consts)
    cache = (cache_diff_k, cache_diff_v)
    y_s, s_out = _group(x_sample.reshape(dec_batch * dec_seq, d_model), depth, layer_w, carried_states, cache,
                        dec_batch, dec_seq, alpha, consts)
    return (y_p.reshape(batch, seq, d_model), y_s.reshape(dec_batch, dec_seq, d_model)) + p_out + s_out
```

```python
import functools
import math

import jax
import jax.numpy as jnp
from jax import lax
from jax.experimental import pallas as pl
from jax.experimental.pallas import tpu as pltpu

F32 = jnp.float32
BF16 = jnp.bfloat16

HEAD_DIM = 64
N_HEADS = 4
CONV_W = 4
CHUNK = 64
HB_W = N_HEADS * HEAD_DIM
ATT_W = N_HEADS * 2 * HEAD_DIM
LN_EPS = 1e-5
RMS_EPS = 1e-6
NEG_BIG = -1e30
SCAN_ROWS = 128
SCAN_STREAMS = 1
SUB = 16
LOG2E = 1.4426950408889634
VT_ROWS = 2 * HEAD_DIM + 16
Q_STRIP = 256
V7X_VMEM_LIMIT = 56 * 1024 * 1024

A_QK0 = 0
A_V0 = A_QK0 + 2 * HB_W
A_I0 = A_V0 + HB_W
A_F0 = A_I0 + N_HEADS
A_O0 = A_F0 + N_HEADS
B_Q0 = A_O0 + HB_W
B_K0 = B_Q0 + ATT_W
B_V0 = B_K0 + ATT_W
C_Q0 = B_V0 + ATT_W
N_IN = C_Q0 + 4 * HB_W

_SEGS = {}
_off = 0
for _name, _w in (("aqk", 2 * HB_W), ("av", HB_W), ("ao", HB_W), ("gi", HB_W), ("gf", HB_W),
                  ("bq", ATT_W), ("bk", ATT_W), ("bv", ATT_W), ("c", 4 * HB_W)):
    _SEGS[_name] = (_off, _w)
    _off += _w
N_CAT = _off


def _tile(n, pref, mult=8):
    t = min(pref, n)
    while t > mult and (n % t or t % mult):
        t -= mult
    assert n % t == 0 and t % mult == 0, (n, pref, mult)
    return t


def _cparams(sem):
    return pltpu.CompilerParams(dimension_semantics=sem, vmem_limit_bytes=V7X_VMEM_LIMIT)


def _const_spec(shape):
    nd = len(shape)
    return pl.BlockSpec(shape, lambda *_: (0,) * nd)


def _dot(a, b):
    return jnp.dot(a, b, preferred_element_type=F32)


def _dot_nt(a, b):
    return lax.dot_general(a, b, (((1,), (1,)), ((), ())), preferred_element_type=F32)


def _split3(x):
    a = x.astype(BF16)
    r = x - a.astype(F32)
    b = r.astype(BF16)
    c = (r - b.astype(F32)).astype(BF16)
    return a, b, c


def _dot_exact_left(mat01, x):
    a, b, c = _split3(x)
    return _dot(mat01, a) + _dot(mat01, b) + _dot(mat01, c)


def _dot_exact_right(x, mat01):
    a, b, c = _split3(x)
    return _dot(a, mat01) + _dot(b, mat01) + _dot(c, mat01)


def _layer_norm(y, g, b):
    mu = jnp.mean(y, axis=-1, keepdims=True)
    d = y - mu
    var = jnp.mean(d * d, axis=-1, keepdims=True)
    return d * lax.rsqrt(var + LN_EPS) * g + b


def _round_robin(phased):
    last = [None] * len(phased)
    live = list(range(len(phased)))
    while live:
        for g in list(live):
            try:
                out = next(phased[g])
            except StopIteration:
                live.remove(g)
            else:
                if out is not None:
                    last[g] = out
    return last


def _pad_rows(x, rows):
    if x.shape[0] == rows:
        return x
    return jnp.concatenate([x, jnp.zeros((rows - x.shape[0],) + x.shape[1:], x.dtype)], axis=0)


def _swiglu_ln(x, wi_ref, wo_ref, g_ref, b_ref, acc_ref, alpha, d_ff, tf):
    xb = x.astype(BF16)
    for c in range(d_ff // tf):
        gate = _dot(xb, wi_ref[:, c * tf:(c + 1) * tf])
        up = _dot(xb, wi_ref[:, d_ff + c * tf:d_ff + (c + 1) * tf])
        h = (gate * jax.nn.sigmoid(gate) * up).astype(BF16)
        part = _dot(h, wo_ref[c * tf:(c + 1) * tf, :])
        if c == 0:
            acc_ref[...] = part
        else:
            acc_ref[...] += part
    y = alpha * x + 0.5 * acc_ref[...]
    return _layer_norm(y, g_ref[...], b_ref[...])


def _ffn_ln_body(x_ref, wi_ref, wo_ref, g_ref, b_ref, o_ref, acc_ref, *, alpha, d_ff, tf):
    o_ref[...] = _swiglu_ln(x_ref[...], wi_ref, wo_ref, g_ref, b_ref, acc_ref, alpha, d_ff, tf)


def _ffn_ln(x, wi, wo, g, b, alpha):
    n, d = x.shape
    d_ff = wo.shape[0]
    tm = _tile(n, 512)
    tf = _tile(d_ff, 256, 128)
    return pl.pallas_call(
        functools.partial(_ffn_ln_body, alpha=alpha, d_ff=d_ff, tf=tf),
        grid=(n // tm,),
        in_specs=[pl.BlockSpec((tm, d), lambda i: (i, 0)),
                  _const_spec(wi.shape), _const_spec(wo.shape),
                  _const_spec((1, d)), _const_spec((1, d))],
        out_specs=pl.BlockSpec((tm, d), lambda i: (i, 0)),
        out_shape=jax.ShapeDtypeStruct((n, d), F32),
        scratch_shapes=[pltpu.VMEM((tm, d), F32)],
        compiler_params=_cparams(("parallel",)),
        name="ffn_ln",
    )(x, wi, wo, g.reshape(1, d), b.reshape(1, d))


_PROJ_OUTS = ("aqk", "av", "ao", "gi", "gf", "bq", "bk", "bv", "c")


def _in_proj_body(*refs, emit_vt, n_alias):
    x_ref, w_ref, bias_ref, wvt_ref, bvt_ref = refs[:5]
    out_refs = refs[5 + n_alias:]
    xb = x_ref[...].astype(BF16)
    tm = xb.shape[0]
    outs = dict(zip(_PROJ_OUTS, out_refs))
    for name in _PROJ_OUTS:
        c0, w = _SEGS[name]
        for s in range(0, w, 2 * HB_W):
            e = min(s + 2 * HB_W, w)
            val = _dot(xb, w_ref[:, c0 + s:c0 + e]) + bias_ref[:, c0 + s:c0 + e]
            if name in ("bk", "bv"):
                for h in range(N_HEADS):
                    outs[name][pl.ds(h, tm, stride=N_HEADS), :] = val[:, h * 2 * HEAD_DIM:(h + 1) * 2 * HEAD_DIM]
            else:
                outs[name][:, s:e] = val
            if name == "bk":
                out_refs[len(_PROJ_OUTS)][:, s:e] = val.astype(BF16)
    if emit_vt:
        vt = (_dot_nt(wvt_ref[...], xb) + bvt_ref[...]).astype(BF16)
        for h in range(N_HEADS):
            for j in range(vt.shape[1] // Q_STRIP):
                out_refs[len(_PROJ_OUTS) + 1][h, j] = vt[h * VT_ROWS:(h + 1) * VT_ROWS, j * Q_STRIP:(j + 1) * Q_STRIP]


def _in_proj(x, w_cat, b_cat, w_vt, b_vt, emit_vt, layer, depth, kv_bufs):
    n, d = x.shape
    tm = _tile(n, 512, Q_STRIP if emit_vt else 8)
    out_shape, out_specs = [], []
    for k in _PROJ_OUTS:
        if k in ("bk", "bv"):
            out_shape.append(jax.ShapeDtypeStruct((depth, n * N_HEADS, 2 * HEAD_DIM), F32))
            out_specs.append(pl.BlockSpec((None, tm * N_HEADS, 2 * HEAD_DIM), lambda i: (layer, i, 0)))
        else:
            out_shape.append(jax.ShapeDtypeStruct((n, _SEGS[k][1]), F32))
            out_specs.append(pl.BlockSpec((tm, _SEGS[k][1]), lambda i: (i, 0)))
    out_shape.append(jax.ShapeDtypeStruct((n, ATT_W), BF16))
    out_specs.append(pl.BlockSpec((tm, ATT_W), lambda i: (i, 0)))
    if emit_vt:
        out_shape.append(jax.ShapeDtypeStruct((N_HEADS, n // Q_STRIP, VT_ROWS, Q_STRIP), BF16))
        out_specs.append(pl.BlockSpec((N_HEADS, tm // Q_STRIP, VT_ROWS, Q_STRIP), lambda i: (0, i, 0, 0)))
    in_specs = [pl.BlockSpec((tm, d), lambda i: (i, 0)),
                _const_spec(w_cat.shape), _const_spec(b_cat.shape),
                _const_spec(w_vt.shape), _const_spec(b_vt.shape)]
    args = [x, w_cat, b_cat, w_vt, b_vt]
    aliases = {}
    for buf, name in zip(kv_bufs, ("bk", "bv")):
        aliases[len(args)] = _PROJ_OUTS.index(name)
        in_specs.append(pl.BlockSpec(memory_space=pl.ANY))
        args.append(buf)
    return pl.pallas_call(
        functools.partial(_in_proj_body, emit_vt=emit_vt, n_alias=len(aliases)),
        grid=(n // tm,),
        in_specs=in_specs,
        out_specs=out_specs,
        out_shape=out_shape,
        input_output_aliases=aliases,
        compiler_params=_cparams(("parallel",)),
        name="in_proj",
    )(*args)


def _out_proj_ffn_body(ya_ref, yb_ref, yc_ref, x_ref, w_ref, g1_ref, b1_ref, wi_ref, wo_ref, g2_ref, b2_ref,
                       o_ref, acc_ref, *, alpha, d_ff, tf):
    y = _dot(ya_ref[...].astype(BF16), w_ref[0:HB_W, :])
    y += _dot(yb_ref[...].astype(BF16), w_ref[HB_W:HB_W + ATT_W, :])
    y += _dot(yc_ref[...].astype(BF16), w_ref[HB_W + ATT_W:, :])
    x = _layer_norm(alpha * x_ref[...] + y, g1_ref[...], b1_ref[...])
    o_ref[...] = _swiglu_ln(x, wi_ref, wo_ref, g2_ref, b2_ref, acc_ref, alpha, d_ff, tf)


def _out_proj_ffn(ya, yb, yc, x, w, g1, b1, wi, wo, g2, b2, alpha):
    n, d = x.shape
    d_ff = wo.shape[0]
    tm = _tile(n, 512)
    tf = _tile(d_ff, 256, 128)
    row = lambda width: pl.BlockSpec((tm, width), lambda i: (i, 0))
    vec = _const_spec((1, d))
    return pl.pallas_call(
        functools.partial(_out_proj_ffn_body, alpha=alpha, d_ff=d_ff, tf=tf),
        grid=(n // tm,),
        in_specs=[row(HB_W), row(ATT_W), row(HB_W), row(d), _const_spec(w.shape), vec, vec,
                  _const_spec(wi.shape), _const_spec(wo.shape), vec, vec],
        out_specs=row(d),
        out_shape=jax.ShapeDtypeStruct((n, d), F32),
        scratch_shapes=[pltpu.VMEM((tm, d), F32)],
        compiler_params=_cparams(("parallel",)),
        name="out_proj_ffn",
    )(ya, yb, yc, x, w, g1.reshape(1, d), b1.reshape(1, d), wi, wo, g2.reshape(1, d), b2.reshape(1, d))


def _mlstm_body(qk_ref, v_ref, o_ref, gi_ref, gf_ref, cw_ref, cb_ref, conv0_ref, c0_ref, n0_ref, m0_ref,
                tri_ref, eblk_ref,
                ya_ref, c1_ref, n1_ref, m1_ref, convn_ref,
                up_ref, c_ref, n_ref, m_ref, *, rows, streams):
    step = pl.program_id(1)
    low = lax.broadcasted_iota(jnp.int32, (1, 2 * HEAD_DIM), 1) < HEAD_DIM
    pw = 2 * HEAD_DIM

    @pl.when(step == 0)
    def _():
        for g in range(streams):
            up_ref[g, 0:8, :] = conv0_ref[g]
            c_ref[g] = jnp.zeros(c_ref.shape[1:], F32)
            for i in range(N_HEADS // 2):
                pair = c0_ref[g, i]
                c_ref[g, i * pw:i * pw + HEAD_DIM, i * pw:(i + 1) * pw] = jnp.where(low, pair, 0.0)
                c_ref[g, i * pw + HEAD_DIM:(i + 1) * pw, i * pw:(i + 1) * pw] = jnp.where(low, 0.0, pair)
        n_ref[...] = n0_ref[...]
        m_ref[...] = m0_ref[...]

    new_states = _round_robin([_mlstm_stream(
        qk_ref.at[g], v_ref.at[g], o_ref.at[g], gi_ref.at[g], gf_ref.at[g], cw_ref, cb_ref, tri_ref, eblk_ref,
        ya_ref.at[g], convn_ref.at[g], up_ref.at[g], c_ref.at[g], n_ref.at[g], m_ref.at[g], rows=rows)
        for g in range(streams)])

    @pl.when(step == pl.num_programs(1) - 1)
    def _():
        for g, (c_new, n_new, m_new) in enumerate(new_states):
            for i in range(N_HEADS // 2):
                c1_ref[g, i] = jnp.where(low, c_new[i * pw:i * pw + HEAD_DIM, i * pw:(i + 1) * pw],
                                         c_new[i * pw + HEAD_DIM:(i + 1) * pw, i * pw:(i + 1) * pw])
            n1_ref[g] = n_new
            m1_ref[g] = m_new


def _mlstm_stream(qk_ref, v_ref, o_ref, gi_ref, gf_ref, cw_ref, cb_ref, tri_ref, eblk_ref,
                  ya_ref, convn_ref, up_ref, c_ref, n_ref, m_ref, *, rows):
    L = SCAN_ROWS

    gi = _pad_rows(gi_ref[...], L)
    flog = jax.nn.log_sigmoid(_pad_rows(gf_ref[...], L))
    if rows < L:
        live = lax.broadcasted_iota(jnp.int32, (L, 1), 0) < rows
        gi = jnp.where(live, gi, NEG_BIG)
        flog = jnp.where(live, flog, 0.0)
    b = _dot_exact_left(tri_ref[...], flog)
    b_t = b.T
    gi_t = gi.T
    yield

    u = _pad_rows(qk_ref[...], L)
    up_ref[8:8 + L, :] = u
    y = cb_ref[...]
    for j in range(CONV_W):
        y = y + cw_ref[j:j + 1, :] * up_ref[5 + j:5 + j + L, :]
    convn_ref[...] = up_ref[rows:rows + 8, :]
    up_ref[0:8, :] = up_ref[L:L + 8, :]
    qk = y * jax.nn.sigmoid(y)
    q = qk[:, :HB_W] * (HEAD_DIM ** -0.5)
    k = qk[:, HB_W:]
    v = _pad_rows(v_ref[...], L)
    kb = k.astype(BF16)
    vb = v.astype(BF16)
    qb = q.astype(BF16)

    lane_head = lax.broadcasted_iota(jnp.int32, (1, HB_W), 1) // HEAD_DIM
    qk_scores = [_dot_nt(jnp.where(lane_head == h, q, 0.0).astype(BF16), kb) for h in range(N_HEADS)]
    c_prev = c_ref[...]
    n_prev = n_ref[...]
    eblk = eblk_ref[...]
    num_state = _dot(qb, c_prev.astype(BF16))
    den_state = _dot((q * n_prev).astype(BF16), eblk)
    yield

    m_prev = m_ref[...]
    g = b + m_prev

    t_idx = lax.broadcasted_iota(jnp.int32, (L, L), 0)
    s_idx = lax.broadcasted_iota(jnp.int32, (L, L), 1)
    causal = s_idx <= t_idx

    mrow_hb = jnp.zeros((L, HB_W), F32)
    wg_hb = jnp.zeros((L, HB_W), F32)
    den_hb = jnp.zeros((L, HB_W), F32)
    num_hb = jnp.zeros((L, HB_W), F32)
    for h in range(N_HEADS):
        sel = lane_head == h
        c0 = h * HEAD_DIM
        dmat = b[:, c0:c0 + 1] - b_t[c0:c0 + 1, :] + gi_t[c0:c0 + 1, :]
        dmat = jnp.where(causal, dmat, NEG_BIG)
        gcol = g[:, c0:c0 + 1]
        mrow = jnp.maximum(gcol, jnp.max(dmat, axis=1, keepdims=True))
        wd = jnp.exp(dmat - mrow)
        wg = jnp.exp(gcol - mrow)
        qkw = qk_scores[h] * wd
        den = jnp.sum(qkw, axis=1, keepdims=True)
        num = _dot(qkw.astype(BF16), vb)
        mrow_hb = jnp.where(sel, mrow, mrow_hb)
        wg_hb = jnp.where(sel, wg, wg_hb)
        den_hb = jnp.where(sel, den, den_hb)
        num_hb = jnp.where(sel, num, num_hb)

    num_hb = wg_hb * num_state + num_hb
    den_hb = wg_hb * den_state + den_hb
    hout = num_hb / jnp.maximum(jnp.abs(den_hb), jnp.exp(-mrow_hb))
    ya = jax.nn.sigmoid(_pad_rows(o_ref[...], L)) * hout
    ya_ref[...] = ya[:rows]
    yield

    m_new = mrow_hb[L - 1:L, :]
    b_last = b[L - 1:L, :]
    decay = jnp.exp(b_last + m_prev - m_new)
    ws = jnp.exp(b_last - b + gi - m_new)
    kw = k * ws
    upd = _dot(kw.T.astype(BF16), vb) * eblk.astype(F32)
    c_new = decay * c_prev + upd
    n_new = decay * n_prev + jnp.sum(kw, axis=0, keepdims=True)
    c_ref[...] = c_new
    n_ref[...] = n_new
    m_ref[...] = m_new
    yield c_new, n_new, m_new


def _mlstm(proj, conv_w, conv_b, conv0, c0_pk, n0, m0_hb, tri, eblk, batch, t_len):
    rows = min(SCAN_ROWS, t_len)
    steps = t_len // rows
    g = SCAN_STREAMS
    tok = lambda width: pl.BlockSpec((g, rows, width), lambda bi, ci: (bi, ci, 0))
    per_b = lambda shape: pl.BlockSpec((g,) + shape, lambda bi, ci: (bi,) + (0,) * len(shape))
    seq = lambda a: a.reshape(batch, t_len, a.shape[-1])
    outs = pl.pallas_call(
        functools.partial(_mlstm_body, rows=rows, streams=g),
        grid=(batch // g, steps),
        in_specs=[tok(2 * HB_W), tok(HB_W), tok(HB_W), tok(HB_W), tok(HB_W),
                  _const_spec(conv_w.shape), _const_spec(conv_b.shape),
                  per_b((8, 2 * HB_W)), per_b((N_HEADS // 2, HEAD_DIM, 2 * HEAD_DIM)), per_b((1, HB_W)), per_b((1, HB_W)),
                  _const_spec(tri.shape), _const_spec(eblk.shape)],
        out_specs=[tok(HB_W), per_b((N_HEADS // 2, HEAD_DIM, 2 * HEAD_DIM)), per_b((1, HB_W)), per_b((1, HB_W)),
                   per_b((8, 2 * HB_W))],
        out_shape=[jax.ShapeDtypeStruct((batch, t_len, HB_W), F32),
                   jax.ShapeDtypeStruct((batch, N_HEADS // 2, HEAD_DIM, 2 * HEAD_DIM), F32),
                   jax.ShapeDtypeStruct((batch, 1, HB_W), F32),
                   jax.ShapeDtypeStruct((batch, 1, HB_W), F32),
                   jax.ShapeDtypeStruct((batch, 8, 2 * HB_W), F32)],
        scratch_shapes=[pltpu.VMEM((g, SCAN_ROWS + 8, 2 * HB_W), F32),
                        pltpu.VMEM((g, HB_W, HB_W), F32),
                        pltpu.VMEM((g, 1, HB_W), F32),
                        pltpu.VMEM((g, 1, HB_W), F32)],
        compiler_params=_cparams(("parallel", "arbitrary")),
        name="mlstm",
    )(seq(proj["aqk"]), seq(proj["av"]), seq(proj["ao"]), seq(proj["gi"]), seq(proj["gf"]), conv_w, conv_b,
      conv0, c0_pk, n0, m0_hb, tri, eblk)
    return (outs[0].reshape(batch * t_len, HB_W),) + tuple(outs[1:])


def _hgrn_body(c_ref, lb_ref, hng_ref, s0_ref, tri16_ref, eblk_ref, esel_ref, erep_ref, yc_ref, s1_ref, s_ref,
               *, rows):
    step = pl.program_id(1)
    L = SCAN_ROWS

    @pl.when(step == 0)
    def _():
        s_ref[...] = s0_ref[...]

    cin = _pad_rows(c_ref[...], L)
    q = cin[:, 0:HB_W]
    f_pre = cin[:, HB_W:2 * HB_W]
    v = cin[:, 2 * HB_W:3 * HB_W]
    gate = cin[:, 3 * HB_W:]
    lb = lb_ref[...]
    forget = lb + (1.0 - lb) * jax.nn.sigmoid(f_pre)
    logf = jnp.log(forget)
    key = 1.0 - forget
    if rows < L:
        live = lax.broadcasted_iota(jnp.int32, (L, 1), 0) < rows
        logf = jnp.where(live, logf, 0.0)
        key = jnp.where(live, key, 0.0)
    bl = _dot_exact_left(tri16_ref[...], logf) * LOG2E
    eblk = eblk_ref[...]
    groups = L // SUB
    t_loc = lax.broadcasted_iota(jnp.int32, (L, 1), 0) % SUB
    vb = v.astype(BF16)

    def from_source(x, s):
        picked = x.reshape(groups, SUB, HB_W)[:, s:s + 1, :]
        return jnp.broadcast_to(picked, (groups, SUB, HB_W)).reshape(L, HB_W)

    n_sub = max(rows // SUB, 1)
    qd = (q * jnp.exp2(bl)).astype(BF16)
    bl_t = bl.T
    low = lax.broadcasted_iota(jnp.int32, (1, 2 * HEAD_DIM), 1) < HEAD_DIM
    zero_b = jnp.zeros((HEAD_DIM, 2 * HEAD_DIM), BF16)

    def contribution(j):
        r0 = j * SUB
        b_last = bl[r0 + SUB - 1:r0 + SUB, :]
        kw_j = (key[r0:r0 + SUB] * jnp.exp2(b_last - bl[r0:r0 + SUB])).astype(BF16)
        upd = lax.dot_general(kw_j, vb[r0:r0 + SUB], (((0,), (0,)), ((), ())),
                              preferred_element_type=F32)
        dec = jnp.exp2(bl_t[:, r0 + SUB - 1:r0 + SUB])
        pair = []
        for i in range(2):
            rk = 2 * i * HEAD_DIM
            u = jnp.where(low, upd[rk:rk + HEAD_DIM, rk:rk + 2 * HEAD_DIM],
                          upd[rk + HEAD_DIM:rk + 2 * HEAD_DIM, rk:rk + 2 * HEAD_DIM])
            d = jnp.where(low, dec[rk:rk + HEAD_DIM], dec[rk + HEAD_DIM:rk + 2 * HEAD_DIM])
            pair.append((d, u))
        return pair

    contrib = []
    att = jnp.zeros((L, 2 * HEAD_DIM), F32)
    for s in range(SUB):
        w = jnp.exp2(bl - from_source(bl, s))
        p = jnp.where(t_loc >= s, q * from_source(key, s) * w, 0.0)
        att = att + _dot(p.astype(BF16), esel_ref[s])
        if s % 2 == 1 and s // 2 < n_sub:
            contrib.append(contribution(s // 2))
    attb = att.astype(BF16)
    same_sub = (lax.broadcasted_iota(jnp.int32, (L, L), 0) // SUB
                == lax.broadcasted_iota(jnp.int32, (L, L), 1) // SUB)
    lane_head = lax.broadcasted_iota(jnp.int32, (1, HB_W), 1) // HEAD_DIM
    o = jnp.zeros((L, HB_W), F32)
    for h in range(N_HEADS):
        full = jnp.where(same_sub, _dot(attb, erep_ref[h]), 0.0)
        o = jnp.where(lane_head == h, _dot(full.astype(BF16), vb), o)

    pk = [s_ref[0], s_ref[1]]
    o_parts = []
    for j in range(n_sub):
        r0 = j * SUB
        pb = [x.astype(BF16) for x in pk]
        s_bd = jnp.concatenate([
            jnp.concatenate([jnp.where(low, pb[0], 0.0).astype(BF16), zero_b], axis=1),
            jnp.concatenate([jnp.where(low, 0.0, pb[0]).astype(BF16), zero_b], axis=1),
            jnp.concatenate([zero_b, jnp.where(low, pb[1], 0.0).astype(BF16)], axis=1),
            jnp.concatenate([zero_b, jnp.where(low, 0.0, pb[1]).astype(BF16)], axis=1)], axis=0)
        o_parts.append(_dot(qd[r0:r0 + SUB], s_bd))
        pk = [d * p + u for p, (d, u) in zip(pk, contrib[j])]
    s_ref[0] = pk[0]
    s_ref[1] = pk[1]
    o_inter = jnp.concatenate(o_parts, axis=0)
    if rows < L:
        o_inter = _pad_rows(o_inter, L)
    o = o + o_inter

    ms = _dot_exact_right(o * o, eblk) * (1.0 / HEAD_DIM)
    yc = o * lax.rsqrt(ms + RMS_EPS) * hng_ref[...] * (gate * jax.nn.sigmoid(gate))
    yc_ref[...] = yc[:rows]

    @pl.when(step == pl.num_programs(1) - 1)
    def _():
        s1_ref[0] = pk[0]
        s1_ref[1] = pk[1]


def _hgrn(c_all, lb, hn_g, s0_pk, tri16, eblk, esel, erep, batch, t_len):
    rows = min(SCAN_ROWS, t_len)
    steps = t_len // rows
    n = batch * t_len
    pk_spec = pl.BlockSpec((None, 2, HEAD_DIM, 2 * HEAD_DIM), lambda bi, ci: (bi, 0, 0, 0))
    return pl.pallas_call(
        functools.partial(_hgrn_body, rows=rows),
        grid=(batch, steps),
        in_specs=[pl.BlockSpec((rows, 4 * HB_W), lambda bi, ci: (bi * steps + ci, 0)),
                  _const_spec((1, HB_W)), _const_spec((1, HB_W)), pk_spec,
                  _const_spec(tri16.shape), _const_spec(eblk.shape), _const_spec(esel.shape),
                  _const_spec(erep.shape)],
        out_specs=[pl.BlockSpec((rows, HB_W), lambda bi, ci: (bi * steps + ci, 0)), pk_spec],
        out_shape=[jax.ShapeDtypeStruct((n, HB_W), F32),
                   jax.ShapeDtypeStruct((batch, 2, HEAD_DIM, 2 * HEAD_DIM), F32)],
        scratch_shapes=[pltpu.VMEM((2, HEAD_DIM, 2 * HEAD_DIM), F32)],
        compiler_params=_cparams(("parallel", "arbitrary")),
        name="hgrn",
    )(c_all, lb, hn_g, s0_pk, tri16, eblk, esel, erep)


def _attn_prompt_body(qi_tab, ki_tab, lam_ref, q_ref, k_ref, vt_ref, g_ref, o_ref, qx_ref, m_ref, acc_ref,
                      s_ref, bmax_ref, *, tq, out_scale):
    pair = pl.program_id(2)
    qi = qi_tab[pair]
    ki = ki_tab[pair]
    ns = tq // Q_STRIP
    dv = 2 * HEAD_DIM

    @pl.when(ki == 0)
    def _():
        qt = (q_ref[...] * (HEAD_DIM ** -0.5 * LOG2E)).T
        first = lax.broadcasted_iota(jnp.int32, (dv, 1), 0) < HEAD_DIM
        qa = jnp.where(first, qt, 0.0).astype(BF16)
        qb = jnp.where(first, 0.0, qt).astype(BF16)
        for st in range(ns):
            qx_ref[0, st] = qa[:, st * Q_STRIP:(st + 1) * Q_STRIP]
            qx_ref[1, st] = qb[:, st * Q_STRIP:(st + 1) * Q_STRIP]
        m_ref[...] = jnp.full(m_ref.shape, NEG_BIG, F32)
        acc_ref[...] = jnp.zeros(acc_ref.shape, F32)

    def scores(kb, st, diag):
        k = k_ref[kb * Q_STRIP:(kb + 1) * Q_STRIP, :]
        for mp in range(2):
            s = _dot(k, qx_ref[mp, st])
            if diag and st == kb:
                k_chunk = lax.broadcasted_iota(jnp.int32, (Q_STRIP, 1), 0) // CHUNK
                q_chunk = lax.broadcasted_iota(jnp.int32, (1, Q_STRIP), 1) // CHUNK
                s = jnp.where(k_chunk <= q_chunk, s, NEG_BIG)
            s_ref[kb % 2, mp, st] = s
            bmax_ref[kb % 2, mp, st] = jnp.max(s, axis=0, keepdims=True)

    def accumulate(kb, st):
        vt = vt_ref[kb]
        for mp in range(2):
            m_old = m_ref[mp, st]
            m_new = jnp.maximum(m_old, bmax_ref[kb % 2, mp, st])
            p = jnp.exp2(s_ref[kb % 2, mp, st] - m_new).astype(BF16)
            corr = jnp.exp2(m_old - m_new)
            acc_ref[mp, st] = corr * acc_ref[mp, st] + _dot(vt, p)
            m_ref[mp, st] = m_new

    def key_tile(diag):
        first_strip = (lambda kb: kb) if diag else (lambda kb: 0)
        for st in range(ns):
            scores(0, st, diag)
        for kb in range(ns):
            for st in range(first_strip(kb), ns):
                accumulate(kb, st)
                if kb + 1 < ns and st >= first_strip(kb + 1):
                    scores(kb + 1, st, diag)

    @pl.when(ki < qi)
    def _():
        key_tile(False)

    @pl.when(ki == qi)
    def _():
        key_tile(True)
        lam = lam_ref[0, 0]
        for st in range(ns):
            a1 = acc_ref[0, st]
            a2 = acc_ref[1, st]
            o = a1[:dv] / a1[dv:dv + 1] - lam * (a2[:dv] / a2[dv:dv + 1])
            ms = jnp.mean(o * o, axis=0, keepdims=True)
            o = o * lax.rsqrt(ms + RMS_EPS) * g_ref[...] * out_scale
            o_ref[st * Q_STRIP:(st + 1) * Q_STRIP, :] = o.T


def _attn_prompt(lam, bq, bkb, vt, dn_g, batch, t_len, out_scale):
    tq = _tile(t_len, 1024, Q_STRIP)
    nq = t_len // tq
    ns = tq // Q_STRIP
    n = batch * t_len
    w = 2 * HEAD_DIM
    pairs = [(qi, ki) for qi in range(nq) for ki in range(qi + 1)]
    qi_tab = jnp.asarray([p[0] for p in pairs], jnp.int32)
    ki_tab = jnp.asarray([p[1] for p in pairs], jnp.int32)
    qmap = lambda b, h, p, qt, kt: (b * nq + qt[p], h)
    kmap = lambda b, h, p, qt, kt: (b * nq + kt[p], h)
    vmap_ = lambda b, h, p, qt, kt: (h, b * nq + kt[p], 0, 0)
    grid_spec = pltpu.PrefetchScalarGridSpec(
        num_scalar_prefetch=2,
        grid=(batch, N_HEADS, len(pairs)),
        in_specs=[pl.BlockSpec(memory_space=pltpu.SMEM),
                  pl.BlockSpec((tq, w), qmap), pl.BlockSpec((tq, w), kmap),
                  pl.BlockSpec((None, ns, VT_ROWS, Q_STRIP), vmap_),
                  pl.BlockSpec((w, 1), lambda b, h, p, qt, kt: (0, 0))],
        out_specs=pl.BlockSpec((tq, w), qmap),
        scratch_shapes=[pltpu.VMEM((2, ns, w, Q_STRIP), BF16),
                        pltpu.VMEM((2, ns, 1, Q_STRIP), F32),
                        pltpu.VMEM((2, ns, VT_ROWS, Q_STRIP), F32),
                        pltpu.VMEM((2, 2, ns, Q_STRIP, Q_STRIP), F32),
                        pltpu.VMEM((2, 2, ns, 1, Q_STRIP), F32)])
    return pl.pallas_call(
        functools.partial(_attn_prompt_body, tq=tq, out_scale=out_scale),
        grid_spec=grid_spec,
        out_shape=jax.ShapeDtypeStruct((n, ATT_W), F32),
        compiler_params=_cparams(("parallel", "parallel", "arbitrary")),
        name="attn_prompt",
    )(qi_tab, ki_tab, lam, bq, bkb, vt, dn_g.reshape(w, 1))


def _attn_sample_body(lam_ref, q_ref, kn_ref, vn_ref, kp_ref, vp_ref, g_ref, o_ref, *, t_new, past, out_scale):
    w = 2 * HEAD_DIM
    first = lax.broadcasted_iota(jnp.int32, (1, w), 1) < HEAD_DIM
    for h in range(N_HEADS):
        cols = slice(h * w, (h + 1) * w)
        q = q_ref[:, cols] * (HEAD_DIM ** -0.5)
        q2 = jnp.concatenate([jnp.where(first, q, 0.0), jnp.where(first, 0.0, q)], axis=0).astype(BF16)
        kp = kp_ref[pl.ds(h, past, stride=N_HEADS), :].astype(BF16)
        vp = vp_ref[pl.ds(h, past, stride=N_HEADS), :].astype(BF16)
        s_p = _dot_nt(q2, kp)
        kn = kn_ref[pl.ds(h, t_new, stride=N_HEADS), :].astype(BF16)
        vn = vn_ref[pl.ds(h, t_new, stride=N_HEADS), :].astype(BF16)
        s_n = _dot_nt(q2, kn)
        m = jnp.maximum(jnp.max(s_p, axis=1, keepdims=True), jnp.max(s_n, axis=1, keepdims=True))
        p_p = jnp.exp(s_p - m)
        p_n = jnp.exp(s_n - m)
        l = jnp.sum(p_p, axis=1, keepdims=True) + jnp.sum(p_n, axis=1, keepdims=True)
        acc = _dot(p_p.astype(BF16), vp) + _dot(p_n.astype(BF16), vn)
        o2 = acc / l
        o = o2[:t_new] - lam_ref[0, 0] * o2[t_new:]
        ms = jnp.mean(o * o, axis=1, keepdims=True)
        o_ref[:, cols] = o * lax.rsqrt(ms + RMS_EPS) * g_ref[...] * out_scale


def _attn_sample(lam, bq, bk, bv, cache_k, cache_v, layer, dn_g, batch, t_new, out_scale):
    w = 2 * HEAD_DIM
    depth, _, past = cache_k.shape[:3]
    rows = past * N_HEADS
    new = pl.BlockSpec((t_new, ATT_W), lambda b: (b, 0))
    new_kv = pl.BlockSpec((None, t_new * N_HEADS, w), lambda b: (layer, b, 0))
    old = pl.BlockSpec((None, None, rows, w), lambda b: (layer, b, 0, 0))
    return pl.pallas_call(
        functools.partial(_attn_sample_body, t_new=t_new, past=past, out_scale=out_scale),
        grid=(batch,),
        in_specs=[pl.BlockSpec(memory_space=pltpu.SMEM), new, new_kv, new_kv, old, old, _const_spec((1, w))],
        out_specs=new,
        out_shape=jax.ShapeDtypeStruct((batch * t_new, ATT_W), F32),
        compiler_params=_cparams(("parallel",)),
        name="attn_sample",
    )(lam, bq, bk, bv, cache_k.reshape(depth, batch, rows, w), cache_v.reshape(depth, batch, rows, w),
      dn_g.reshape(1, w))


def _to_head_pairs(s):
    b = s.shape[0]
    r = s.reshape(b, N_HEADS // 2, 2, HEAD_DIM, HEAD_DIM)
    return jnp.swapaxes(r, 2, 3).reshape(b, N_HEADS // 2, HEAD_DIM, 2 * HEAD_DIM)


def _from_head_pairs(p):
    b = p.shape[0]
    r = p.reshape(b, N_HEADS // 2, HEAD_DIM, 2, HEAD_DIM)
    return jnp.swapaxes(r, 2, 3).reshape(b, N_HEADS, HEAD_DIM, HEAD_DIM)


def _rearrange_proj(a):
    gates = lambda c0: jnp.repeat(a[..., c0:c0 + N_HEADS], HEAD_DIM, axis=-1)
    out = jnp.concatenate([a[..., A_QK0:A_I0], a[..., A_O0:B_Q0], gates(A_I0), gates(A_F0), a[..., B_Q0:N_IN]],
                          axis=-1)
    assert out.shape[-1] == N_CAT
    return out


def _consts():
    r = jnp.arange(SCAN_ROWS)
    tri = (r[None, :] <= r[:, None]).astype(BF16)
    tri16 = ((r[None, :] <= r[:, None]) & (r[None, :] // SUB == r[:, None] // SUB)).astype(BF16)
    hh = jnp.arange(HB_W) // HEAD_DIM
    eblk = (hh[None, :] == hh[:, None]).astype(BF16)
    lane = jnp.arange(2 * HEAD_DIM)
    src = jnp.arange(SUB)
    esel = ((hh[None, :, None] * SUB + src[:, None, None]) == lane[None, None, :]).astype(BF16)
    erep = ((lane[None, :, None] // SUB == jnp.arange(N_HEADS)[:, None, None])
            & (lane[None, :, None] % SUB == r[None, None, :] % SUB)).astype(BF16)
    return tri, tri16, eblk, esel, erep


def _group(x, depth, layer_w, states, cache, batch, t_len, alpha, consts):
    tri, tri16, eblk, esel, erep = consts
    prompt = cache is None
    outs = []
    kv_bufs = tuple(jnp.zeros((depth, batch * t_len * N_HEADS, 2 * HEAD_DIM), F32) for _ in range(2))
    for l in range(depth):
        w = layer_w[l]
        conv0, c0, n0, m0, s0 = states(l)
        x = _ffn_ln(x, w["f_in0"], w["f_out0"], w["ln_g"][0], w["ln_b"][0], alpha)
        res = _in_proj(x, w["w_cat"], w["b_cat"], w["w_vt"], w["b_vt"], prompt, l, depth, kv_bufs)
        proj = dict(zip(_PROJ_OUTS, res[:len(_PROJ_OUTS)]))
        bkb = res[len(_PROJ_OUTS)]
        kv_bufs = (proj["bk"], proj["bv"])

        conv0_p = jnp.pad(conv0, ((0, 0), (8 - (CONV_W - 1), 0), (0, 0)))
        ya, c1, n1, m1, convn = _mlstm(proj, w["conv_w"], w["conv_b"], conv0_p, _to_head_pairs(c0),
                                       n0.reshape(batch, 1, HB_W),
                                       jnp.repeat(m0, HEAD_DIM, axis=-1).reshape(batch, 1, HB_W),
                                       tri, eblk, batch, t_len)
        lam_init = 0.8 - 0.6 * math.exp(-0.3 * l)
        if prompt:
            yb = _attn_prompt(w["lam"], proj["bq"], bkb, res[len(_PROJ_OUTS) + 1], w["dn_g"], batch, t_len,
                              1.0 - lam_init)
        else:
            yb = _attn_sample(w["lam"], proj["bq"], proj["bk"], proj["bv"], cache[0], cache[1], l, w["dn_g"],
                              batch, t_len, 1.0 - lam_init)
        yc, s1 = _hgrn(proj["c"], w["lb"], w["hn_g"], _to_head_pairs(s0), tri16, eblk, esel, erep, batch, t_len)
        x = _out_proj_ffn(ya, yb, yc, x, w["w_out"], w["ln_g"][1], w["ln_b"][1],
                          w["f_in1"], w["f_out1"], w["ln_g"][2], w["ln_b"][2], alpha)
        outs.append((_from_head_pairs(c1),
                     n1.reshape(batch, N_HEADS, HEAD_DIM),
                     m1.reshape(batch, N_HEADS, HEAD_DIM)[:, :, 0],
                     convn[:, 8 - (CONV_W - 1):, :],
                     _from_head_pairs(s1)))
    kv = tuple(buf.reshape(depth, batch, t_len, N_HEADS, 2 * HEAD_DIM) for buf in kv_bufs)
    return x, kv + tuple(jnp.stack(a) for a in zip(*outs))


def kernel(x_prompt, x_sample, cache_diff_k, cache_diff_v, state_mlstm_c, state_mlstm_n, state_mlstm_m,
           state_mlstm_conv, state_hgrn_s, w_in, b_in, w_out, mlstm_conv_w, mlstm_conv_b, diff_lambda,
           diff_norm_g, hgrn_lb_logits, hgrn_norm_g, ffn_w_in, ffn_w_out, ln_g, ln_b):
    depth = w_in.shape[0]
    alpha = (2.0 * depth) ** 0.25
    batch, seq, d_model = x_prompt.shape
    dec_batch, dec_seq, _ = x_sample.shape
    past = cache_diff_k.shape[2]

    p_lb = jax.nn.softmax(hgrn_lb_logits.astype(F32), axis=0)
    lower_bounds = jnp.cumsum(p_lb, axis=0) - p_lb[0]
    w_cat = _rearrange_proj(w_in.astype(BF16))
    b_cat = _rearrange_proj(b_in).reshape(depth, 1, N_CAT)
    w_vt = jnp.swapaxes(w_in[:, :, B_V0:C_Q0], 1, 2).reshape(depth, N_HEADS, 2 * HEAD_DIM, d_model)
    w_vt = jnp.pad(w_vt, ((0, 0), (0, 0), (0, VT_ROWS - 2 * HEAD_DIM), (0, 0)))
    w_vt = w_vt.reshape(depth, N_HEADS * VT_ROWS, d_model).astype(BF16)
    b_vt = jnp.pad(b_in[:, B_V0:C_Q0].reshape(depth, N_HEADS, 2 * HEAD_DIM),
                   ((0, 0), (0, 0), (0, VT_ROWS - 2 * HEAD_DIM)))
    b_vt = b_vt.at[:, :, 2 * HEAD_DIM].set(1.0).reshape(depth, N_HEADS * VT_ROWS, 1)
    lp = diff_lambda.astype(F32)
    lam_init = jnp.asarray([0.8 - 0.6 * math.exp(-0.3 * l) for l in range(depth)], F32)
    lam = (jnp.exp(jnp.sum(lp[:, 0] * lp[:, 1], axis=-1)) - jnp.exp(jnp.sum(lp[:, 2] * lp[:, 3], axis=-1))
           + lam_init)
    f_in = ffn_w_in.astype(BF16)
    f_out = ffn_w_out.astype(BF16)
    w_out_b = w_out.astype(BF16)
    layer_w = [dict(w_cat=w_cat[l], b_cat=b_cat[l], w_vt=w_vt[l], b_vt=b_vt[l],
                    conv_w=mlstm_conv_w[l], conv_b=mlstm_conv_b[l].reshape(1, 2 * HB_W),
                    lam=lam[l].reshape(1, 1), dn_g=diff_norm_g[l],
                    lb=lower_bounds[l].reshape(1, HB_W),
                    hn_g=jnp.tile(hgrn_norm_g[l], N_HEADS).reshape(1, HB_W),
                    w_out=w_out_b[l], f_in0=f_in[l, 0], f_out0=f_out[l, 0], f_in1=f_in[l, 1], f_out1=f_out[l, 1],
                    ln_g=ln_g[l], ln_b=ln_b[l]) for l in range(depth)]
    consts = _consts()

    def zero_states(_):
        return (jnp.zeros((batch, CONV_W - 1, 2 * HB_W), F32), jnp.zeros((batch, N_HEADS, HEAD_DIM, HEAD_DIM), F32),
                jnp.zeros((batch, N_HEADS, HEAD_DIM), F32), jnp.zeros((batch, N_HEADS), F32),
                jnp.zeros((batch, N_HEADS, HEAD_DIM, HEAD_DIM), F32))

    def carried_states(l):
        return (state_mlstm_conv[l], state_mlstm_c[l], state_mlstm_n[l], state_mlstm_m[l], state_hgrn_s[l])

    y_p, p_out = _group(x_prompt.reshape(batch * seq, d_model), depth, layer_w, zero_states, None,
                        batch, seq, alpha, consts)
    cache = (cache_diff_k, cache_diff_v)
    y_s, s_out = _group(x_sample.reshape(dec_batch * dec_seq, d_model), depth, layer_w, carried_states, cache,
                        dec_batch, dec_seq, alpha, consts)
    return (y_p.reshape(batch, seq, d_model), y_s.reshape(dec_batch, dec_seq, d_model)) + p_out + s_out
```

```python
import functools
import math

import jax
import jax.numpy as jnp
from jax import lax
from jax.experimental import pallas as pl
from jax.experimental.pallas import tpu as pltpu

F32 = jnp.float32
BF16 = jnp.bfloat16

HEAD_DIM = 64
N_HEADS = 4
CONV_W = 4
CHUNK = 64
HB_W = N_HEADS * HEAD_DIM
ATT_W = N_HEADS * 2 * HEAD_DIM
LN_EPS = 1e-5
RMS_EPS = 1e-6
NEG_BIG = -1e30
SCAN_ROWS = 128
HGRN_ROWS = 256
SCAN_STREAMS = 1
SUB = 16
LOG2E = 1.4426950408889634
VT_ROWS = 2 * HEAD_DIM + 16
Q_STRIP = 256
ATT_TILE = 2048
V7X_VMEM_LIMIT = 56 * 1024 * 1024

A_QK0 = 0
A_V0 = A_QK0 + 2 * HB_W
A_I0 = A_V0 + HB_W
A_F0 = A_I0 + N_HEADS
A_O0 = A_F0 + N_HEADS
B_Q0 = A_O0 + HB_W
B_K0 = B_Q0 + ATT_W
B_V0 = B_K0 + ATT_W
C_Q0 = B_V0 + ATT_W
N_IN = C_Q0 + 4 * HB_W

_SEGS = {}
_off = 0
for _name, _w in (("aqk", 2 * HB_W), ("av", HB_W), ("ao", HB_W), ("gi", HB_W), ("gf", HB_W),
                  ("bq", ATT_W), ("bk", ATT_W), ("bv", ATT_W), ("c", 4 * HB_W)):
    _SEGS[_name] = (_off, _w)
    _off += _w
N_CAT = _off


def _tile(n, pref, mult=8):
    t = min(pref, n)
    while t > mult and (n % t or t % mult):
        t -= mult
    assert n % t == 0 and t % mult == 0, (n, pref, mult)
    return t


def _cparams(sem):
    return pltpu.CompilerParams(dimension_semantics=sem, vmem_limit_bytes=V7X_VMEM_LIMIT)


def _const_spec(shape):
    nd = len(shape)
    return pl.BlockSpec(shape, lambda *_: (0,) * nd)


def _dot(a, b):
    return jnp.dot(a, b, preferred_element_type=F32)


def _dot_nt(a, b):
    return lax.dot_general(a, b, (((1,), (1,)), ((), ())), preferred_element_type=F32)


def _split3(x):
    a = x.astype(BF16)
    r = x - a.astype(F32)
    b = r.astype(BF16)
    c = (r - b.astype(F32)).astype(BF16)
    return a, b, c


def _dot_exact_left(mat01, x):
    a, b, c = _split3(x)
    return _dot(mat01, a) + _dot(mat01, b) + _dot(mat01, c)


def _dot_exact_right(x, mat01):
    a, b, c = _split3(x)
    return _dot(a, mat01) + _dot(b, mat01) + _dot(c, mat01)


def _layer_norm(y, g, b):
    mu = jnp.mean(y, axis=-1, keepdims=True)
    d = y - mu
    var = jnp.mean(d * d, axis=-1, keepdims=True)
    return d * lax.rsqrt(var + LN_EPS) * g + b


def _round_robin(phased):
    last = [None] * len(phased)
    live = list(range(len(phased)))
    while live:
        for g in list(live):
            try:
                out = next(phased[g])
            except StopIteration:
                live.remove(g)
            else:
                if out is not None:
                    last[g] = out
    return last


def _pad_rows(x, rows):
    if x.shape[0] == rows:
        return x
    return jnp.concatenate([x, jnp.zeros((rows - x.shape[0],) + x.shape[1:], x.dtype)], axis=0)


def _swiglu_ln(x, wi_ref, wo_ref, g_ref, b_ref, acc_ref, alpha, d_ff, tf):
    xb = x.astype(BF16)
    for c in range(d_ff // tf):
        gate = _dot(xb, wi_ref[:, c * tf:(c + 1) * tf])
        up = _dot(xb, wi_ref[:, d_ff + c * tf:d_ff + (c + 1) * tf])
        h = (gate * jax.nn.sigmoid(gate) * up).astype(BF16)
        part = _dot(h, wo_ref[c * tf:(c + 1) * tf, :])
        if c == 0:
            acc_ref[...] = part
        else:
            acc_ref[...] += part
    y = alpha * x + 0.5 * acc_ref[...]
    return _layer_norm(y, g_ref[...], b_ref[...])


def _ffn_ln_body(x_ref, wi_ref, wo_ref, g_ref, b_ref, o_ref, acc_ref, *, alpha, d_ff, tf):
    o_ref[...] = _swiglu_ln(x_ref[...], wi_ref, wo_ref, g_ref, b_ref, acc_ref, alpha, d_ff, tf)


def _ffn_ln(x, wi, wo, g, b, alpha):
    n, d = x.shape
    d_ff = wo.shape[0]
    tm = _tile(n, 512)
    tf = _tile(d_ff, 256, 128)
    return pl.pallas_call(
        functools.partial(_ffn_ln_body, alpha=alpha, d_ff=d_ff, tf=tf),
        grid=(n // tm,),
        in_specs=[pl.BlockSpec((tm, d), lambda i: (i, 0)),
                  _const_spec(wi.shape), _const_spec(wo.shape),
                  _const_spec((1, d)), _const_spec((1, d))],
        out_specs=pl.BlockSpec((tm, d), lambda i: (i, 0)),
        out_shape=jax.ShapeDtypeStruct((n, d), F32),
        scratch_shapes=[pltpu.VMEM((tm, d), F32)],
        compiler_params=_cparams(("parallel",)),
        name="ffn_ln",
    )(x, wi, wo, g.reshape(1, d), b.reshape(1, d))


_PROJ_OUTS = ("aqk", "av", "ao", "gi", "gf", "bq", "bk", "bv", "c")


def _in_proj_body(*refs, emit_vt, n_alias):
    x_ref, w_ref, bias_ref, wvt_ref, bvt_ref = refs[:5]
    out_refs = refs[5 + n_alias:]
    xb = x_ref[...].astype(BF16)
    tm = xb.shape[0]
    outs = dict(zip(_PROJ_OUTS, out_refs))
    for name in _PROJ_OUTS:
        c0, w = _SEGS[name]
        for s in range(0, w, 2 * HB_W):
            e = min(s + 2 * HB_W, w)
            val = _dot(xb, w_ref[:, c0 + s:c0 + e]) + bias_ref[:, c0 + s:c0 + e]
            if name in ("bk", "bv"):
                for h in range(N_HEADS):
                    outs[name][pl.ds(h, tm, stride=N_HEADS), :] = val[:, h * 2 * HEAD_DIM:(h + 1) * 2 * HEAD_DIM]
            else:
                outs[name][:, s:e] = val
            if name == "bk":
                out_refs[len(_PROJ_OUTS)][:, s:e] = val.astype(BF16)
    if emit_vt:
        vt = (_dot_nt(wvt_ref[...], xb) + bvt_ref[...]).astype(BF16)
        for h in range(N_HEADS):
            for j in range(vt.shape[1] // Q_STRIP):
                out_refs[len(_PROJ_OUTS) + 1][h, j] = vt[h * VT_ROWS:(h + 1) * VT_ROWS, j * Q_STRIP:(j + 1) * Q_STRIP]


def _in_proj(x, w_cat, b_cat, w_vt, b_vt, emit_vt, layer, depth, kv_bufs):
    n, d = x.shape
    tm = _tile(n, 512, Q_STRIP if emit_vt else 8)
    out_shape, out_specs = [], []
    for k in _PROJ_OUTS:
        if k in ("bk", "bv"):
            out_shape.append(jax.ShapeDtypeStruct((depth, n * N_HEADS, 2 * HEAD_DIM), F32))
            out_specs.append(pl.BlockSpec((None, tm * N_HEADS, 2 * HEAD_DIM), lambda i: (layer, i, 0)))
        else:
            out_shape.append(jax.ShapeDtypeStruct((n, _SEGS[k][1]), F32))
            out_specs.append(pl.BlockSpec((tm, _SEGS[k][1]), lambda i: (i, 0)))
    out_shape.append(jax.ShapeDtypeStruct((n, ATT_W), BF16))
    out_specs.append(pl.BlockSpec((tm, ATT_W), lambda i: (i, 0)))
    if emit_vt:
        out_shape.append(jax.ShapeDtypeStruct((N_HEADS, n // Q_STRIP, VT_ROWS, Q_STRIP), BF16))
        out_specs.append(pl.BlockSpec((N_HEADS, tm // Q_STRIP, VT_ROWS, Q_STRIP), lambda i: (0, i, 0, 0)))
    in_specs = [pl.BlockSpec((tm, d), lambda i: (i, 0)),
                _const_spec(w_cat.shape), _const_spec(b_cat.shape),
                _const_spec(w_vt.shape), _const_spec(b_vt.shape)]
    args = [x, w_cat, b_cat, w_vt, b_vt]
    aliases = {}
    for buf, name in zip(kv_bufs, ("bk", "bv")):
        aliases[len(args)] = _PROJ_OUTS.index(name)
        in_specs.append(pl.BlockSpec(memory_space=pl.ANY))
        args.append(buf)
    return pl.pallas_call(
        functools.partial(_in_proj_body, emit_vt=emit_vt, n_alias=len(aliases)),
        grid=(n // tm,),
        in_specs=in_specs,
        out_specs=out_specs,
        out_shape=out_shape,
        input_output_aliases=aliases,
        compiler_params=_cparams(("parallel",)),
        name="in_proj",
    )(*args)


def _out_proj_ffn_body(ya_ref, yb_ref, yc_ref, x_ref, w_ref, g1_ref, b1_ref, wi_ref, wo_ref, g2_ref, b2_ref,
                       o_ref, acc_ref, *, alpha, d_ff, tf):
    y = _dot(ya_ref[...].astype(BF16), w_ref[0:HB_W, :])
    y += _dot(yb_ref[...].astype(BF16), w_ref[HB_W:HB_W + ATT_W, :])
    y += _dot(yc_ref[...].astype(BF16), w_ref[HB_W + ATT_W:, :])
    x = _layer_norm(alpha * x_ref[...] + y, g1_ref[...], b1_ref[...])
    o_ref[...] = _swiglu_ln(x, wi_ref, wo_ref, g2_ref, b2_ref, acc_ref, alpha, d_ff, tf)


def _out_proj_ffn(ya, yb, yc, x, w, g1, b1, wi, wo, g2, b2, alpha):
    n, d = x.shape
    d_ff = wo.shape[0]
    tm = _tile(n, 512)
    tf = _tile(d_ff, 256, 128)
    row = lambda width: pl.BlockSpec((tm, width), lambda i: (i, 0))
    vec = _const_spec((1, d))
    return pl.pallas_call(
        functools.partial(_out_proj_ffn_body, alpha=alpha, d_ff=d_ff, tf=tf),
        grid=(n // tm,),
        in_specs=[row(HB_W), row(ATT_W), row(HB_W), row(d), _const_spec(w.shape), vec, vec,
                  _const_spec(wi.shape), _const_spec(wo.shape), vec, vec],
        out_specs=row(d),
        out_shape=jax.ShapeDtypeStruct((n, d), F32),
        scratch_shapes=[pltpu.VMEM((tm, d), F32)],
        compiler_params=_cparams(("parallel",)),
        name="out_proj_ffn",
    )(ya, yb, yc, x, w, g1.reshape(1, d), b1.reshape(1, d), wi, wo, g2.reshape(1, d), b2.reshape(1, d))


def _mlstm_body(qk_ref, v_ref, o_ref, gi_ref, gf_ref, cw_ref, cb_ref, conv0_ref, c0_ref, n0_ref, m0_ref,
                tri_ref, eblk_ref,
                ya_ref, c1_ref, n1_ref, m1_ref, convn_ref,
                up_ref, c_ref, n_ref, m_ref, *, rows, streams):
    step = pl.program_id(1)
    low = lax.broadcasted_iota(jnp.int32, (1, 2 * HEAD_DIM), 1) < HEAD_DIM
    pw = 2 * HEAD_DIM

    @pl.when(step == 0)
    def _():
        for g in range(streams):
            up_ref[g, 0:8, :] = conv0_ref[g]
            c_ref[g] = jnp.zeros(c_ref.shape[1:], F32)
            for i in range(N_HEADS // 2):
                pair = c0_ref[g, i]
                c_ref[g, i * pw:i * pw + HEAD_DIM, i * pw:(i + 1) * pw] = jnp.where(low, pair, 0.0)
                c_ref[g, i * pw + HEAD_DIM:(i + 1) * pw, i * pw:(i + 1) * pw] = jnp.where(low, 0.0, pair)
        n_ref[...] = n0_ref[...]
        m_ref[...] = m0_ref[...]

    new_states = _round_robin([_mlstm_stream(
        qk_ref.at[g], v_ref.at[g], o_ref.at[g], gi_ref.at[g], gf_ref.at[g], cw_ref, cb_ref, tri_ref, eblk_ref,
        ya_ref.at[g], convn_ref.at[g], up_ref.at[g], c_ref.at[g], n_ref.at[g], m_ref.at[g], rows=rows)
        for g in range(streams)])

    @pl.when(step == pl.num_programs(1) - 1)
    def _():
        for g, (c_new, n_new, m_new) in enumerate(new_states):
            for i in range(N_HEADS // 2):
                c1_ref[g, i] = jnp.where(low, c_new[i * pw:i * pw + HEAD_DIM, i * pw:(i + 1) * pw],
                                         c_new[i * pw + HEAD_DIM:(i + 1) * pw, i * pw:(i + 1) * pw])
            n1_ref[g] = n_new
            m1_ref[g] = m_new


def _mlstm_stream(qk_ref, v_ref, o_ref, gi_ref, gf_ref, cw_ref, cb_ref, tri_ref, eblk_ref,
                  ya_ref, convn_ref, up_ref, c_ref, n_ref, m_ref, *, rows):
    L = SCAN_ROWS

    gi = _pad_rows(gi_ref[...], L)
    flog = jax.nn.log_sigmoid(_pad_rows(gf_ref[...], L))
    if rows < L:
        live = lax.broadcasted_iota(jnp.int32, (L, 1), 0) < rows
        gi = jnp.where(live, gi, NEG_BIG)
        flog = jnp.where(live, flog, 0.0)
    b = _dot_exact_left(tri_ref[...], flog)
    b_t = b.T
    gi_t = gi.T
    yield

    u = _pad_rows(qk_ref[...], L)
    up_ref[8:8 + L, :] = u
    y = cb_ref[...]
    for j in range(CONV_W):
        y = y + cw_ref[j:j + 1, :] * up_ref[5 + j:5 + j + L, :]
    convn_ref[...] = up_ref[rows:rows + 8, :]
    up_ref[0:8, :] = up_ref[L:L + 8, :]
    qk = y * jax.nn.sigmoid(y)
    q = qk[:, :HB_W] * (HEAD_DIM ** -0.5)
    k = qk[:, HB_W:]
    v = _pad_rows(v_ref[...], L)
    kb = k.astype(BF16)
    vb = v.astype(BF16)
    qb = q.astype(BF16)

    lane_head = lax.broadcasted_iota(jnp.int32, (1, HB_W), 1) // HEAD_DIM
    qk_scores = [_dot_nt(jnp.where(lane_head == h, q, 0.0).astype(BF16), kb) for h in range(N_HEADS)]
    c_prev = c_ref[...]
    n_prev = n_ref[...]
    eblk = eblk_ref[...]
    num_state = _dot(qb, c_prev.astype(BF16))
    den_state = _dot((q * n_prev).astype(BF16), eblk)
    yield

    m_prev = m_ref[...]
    g = b + m_prev

    t_idx = lax.broadcasted_iota(jnp.int32, (L, L), 0)
    s_idx = lax.broadcasted_iota(jnp.int32, (L, L), 1)
    causal = s_idx <= t_idx

    mrow_hb = jnp.zeros((L, HB_W), F32)
    wg_hb = jnp.zeros((L, HB_W), F32)
    den_hb = jnp.zeros((L, HB_W), F32)
    num_hb = jnp.zeros((L, HB_W), F32)
    for h in range(N_HEADS):
        sel = lane_head == h
        c0 = h * HEAD_DIM
        dmat = b[:, c0:c0 + 1] - b_t[c0:c0 + 1, :] + gi_t[c0:c0 + 1, :]
        dmat = jnp.where(causal, dmat, NEG_BIG)
        gcol = g[:, c0:c0 + 1]
        mrow = jnp.maximum(gcol, jnp.max(dmat, axis=1, keepdims=True))
        wd = jnp.exp(dmat - mrow)
        wg = jnp.exp(gcol - mrow)
        qkw = qk_scores[h] * wd
        den = jnp.sum(qkw, axis=1, keepdims=True)
        num = _dot(qkw.astype(BF16), vb)
        mrow_hb = jnp.where(sel, mrow, mrow_hb)
        wg_hb = jnp.where(sel, wg, wg_hb)
        den_hb = jnp.where(sel, den, den_hb)
        num_hb = jnp.where(sel, num, num_hb)

    num_hb = wg_hb * num_state + num_hb
    den_hb = wg_hb * den_state + den_hb
    hout = num_hb / jnp.maximum(jnp.abs(den_hb), jnp.exp(-mrow_hb))
    ya = jax.nn.sigmoid(_pad_rows(o_ref[...], L)) * hout
    ya_ref[...] = ya[:rows]
    yield

    m_new = mrow_hb[L - 1:L, :]
    b_last = b[L - 1:L, :]
    decay = jnp.exp(b_last + m_prev - m_new)
    ws = jnp.exp(b_last - b + gi - m_new)
    kw = k * ws
    upd = _dot(kw.T.astype(BF16), vb) * eblk.astype(F32)
    c_new = decay * c_prev + upd
    n_new = decay * n_prev + jnp.sum(kw, axis=0, keepdims=True)
    c_ref[...] = c_new
    n_ref[...] = n_new
    m_ref[...] = m_new
    yield c_new, n_new, m_new


def _mlstm(proj, conv_w, conv_b, conv0, c0_pk, n0, m0_hb, tri, eblk, batch, t_len):
    rows = min(SCAN_ROWS, t_len)
    steps = t_len // rows
    g = SCAN_STREAMS
    tok = lambda width: pl.BlockSpec((g, rows, width), lambda bi, ci: (bi, ci, 0))
    per_b = lambda shape: pl.BlockSpec((g,) + shape, lambda bi, ci: (bi,) + (0,) * len(shape))
    seq = lambda a: a.reshape(batch, t_len, a.shape[-1])
    outs = pl.pallas_call(
        functools.partial(_mlstm_body, rows=rows, streams=g),
        grid=(batch // g, steps),
        in_specs=[tok(2 * HB_W), tok(HB_W), tok(HB_W), tok(HB_W), tok(HB_W),
                  _const_spec(conv_w.shape), _const_spec(conv_b.shape),
                  per_b((8, 2 * HB_W)), per_b((N_HEADS // 2, HEAD_DIM, 2 * HEAD_DIM)), per_b((1, HB_W)), per_b((1, HB_W)),
                  _const_spec(tri.shape), _const_spec(eblk.shape)],
        out_specs=[tok(HB_W), per_b((N_HEADS // 2, HEAD_DIM, 2 * HEAD_DIM)), per_b((1, HB_W)), per_b((1, HB_W)),
                   per_b((8, 2 * HB_W))],
        out_shape=[jax.ShapeDtypeStruct((batch, t_len, HB_W), F32),
                   jax.ShapeDtypeStruct((batch, N_HEADS // 2, HEAD_DIM, 2 * HEAD_DIM), F32),
                   jax.ShapeDtypeStruct((batch, 1, HB_W), F32),
                   jax.ShapeDtypeStruct((batch, 1, HB_W), F32),
                   jax.ShapeDtypeStruct((batch, 8, 2 * HB_W), F32)],
        scratch_shapes=[pltpu.VMEM((g, SCAN_ROWS + 8, 2 * HB_W), F32),
                        pltpu.VMEM((g, HB_W, HB_W), F32),
                        pltpu.VMEM((g, 1, HB_W), F32),
                        pltpu.VMEM((g, 1, HB_W), F32)],
        compiler_params=_cparams(("parallel", "arbitrary")),
        name="mlstm",
    )(seq(proj["aqk"]), seq(proj["av"]), seq(proj["ao"]), seq(proj["gi"]), seq(proj["gf"]), conv_w, conv_b,
      conv0, c0_pk, n0, m0_hb, tri, eblk)
    return (outs[0].reshape(batch * t_len, HB_W),) + tuple(outs[1:])


def _hgrn_body(c_ref, lb_ref, hng_ref, s0_ref, tri16_ref, eblk_ref, esel_ref, erep_ref, yc_ref, s1_ref, s_ref,
               *, rows, length):
    step = pl.program_id(1)
    L = length

    @pl.when(step == 0)
    def _():
        s_ref[...] = s0_ref[...]

    cin = _pad_rows(c_ref[...], L)
    q = cin[:, 0:HB_W]
    f_pre = cin[:, HB_W:2 * HB_W]
    v = cin[:, 2 * HB_W:3 * HB_W]
    gate = cin[:, 3 * HB_W:]
    lb = lb_ref[...]
    forget = lb + (1.0 - lb) * jax.nn.sigmoid(f_pre)
    logf = jnp.log(forget)
    key = 1.0 - forget
    if rows < L:
        live = lax.broadcasted_iota(jnp.int32, (L, 1), 0) < rows
        logf = jnp.where(live, logf, 0.0)
        key = jnp.where(live, key, 0.0)
    bl = _dot_exact_left(tri16_ref[...], logf) * LOG2E
    eblk = eblk_ref[...]
    groups = L // SUB
    t_loc = lax.broadcasted_iota(jnp.int32, (L, 1), 0) % SUB
    vb = v.astype(BF16)

    def from_source(x, s):
        picked = x.reshape(groups, SUB, HB_W)[:, s:s + 1, :]
        return jnp.broadcast_to(picked, (groups, SUB, HB_W)).reshape(L, HB_W)

    n_sub = max(rows // SUB, 1)
    qd = (q * jnp.exp2(bl)).astype(BF16)
    bl_t = bl.T
    low = lax.broadcasted_iota(jnp.int32, (1, 2 * HEAD_DIM), 1) < HEAD_DIM
    zero_b = jnp.zeros((HEAD_DIM, 2 * HEAD_DIM), BF16)

    def contribution(j):
        r0 = j * SUB
        b_last = bl[r0 + SUB - 1:r0 + SUB, :]
        kw_j = (key[r0:r0 + SUB] * jnp.exp2(b_last - bl[r0:r0 + SUB])).astype(BF16)
        upd = lax.dot_general(kw_j, vb[r0:r0 + SUB], (((0,), (0,)), ((), ())),
                              preferred_element_type=F32)
        dec = jnp.exp2(bl_t[:, r0 + SUB - 1:r0 + SUB])
        pair = []
        for i in range(2):
            rk = 2 * i * HEAD_DIM
            u = jnp.where(low, upd[rk:rk + HEAD_DIM, rk:rk + 2 * HEAD_DIM],
                          upd[rk + HEAD_DIM:rk + 2 * HEAD_DIM, rk:rk + 2 * HEAD_DIM])
            d = jnp.where(low, dec[rk:rk + HEAD_DIM], dec[rk + HEAD_DIM:rk + 2 * HEAD_DIM])
            pair.append((d, u))
        return pair

    pk = [s_ref[0], s_ref[1]]
    o_parts = []

    def advance(j):
        r0 = j * SUB
        pb = [x.astype(BF16) for x in pk]
        s_bd = jnp.concatenate([
            jnp.concatenate([jnp.where(low, pb[0], 0.0).astype(BF16), zero_b], axis=1),
            jnp.concatenate([jnp.where(low, 0.0, pb[0]).astype(BF16), zero_b], axis=1),
            jnp.concatenate([zero_b, jnp.where(low, pb[1], 0.0).astype(BF16)], axis=1),
            jnp.concatenate([zero_b, jnp.where(low, 0.0, pb[1]).astype(BF16)], axis=1)], axis=0)
        o_parts.append(_dot(qd[r0:r0 + SUB], s_bd))
        pk[:] = [d * p + u for p, (d, u) in zip(pk, contrib[j])]

    contrib = []
    att = jnp.zeros((L, 2 * HEAD_DIM), F32)
    for s in range(SUB):
        w = jnp.exp2(bl - from_source(bl, s))
        p = jnp.where(t_loc >= s, q * from_source(key, s) * w, 0.0)
        att = att + _dot(p.astype(BF16), esel_ref[s])
        while len(contrib) < ((s + 1) * n_sub) // SUB:
            contrib.append(contribution(len(contrib)))
    for j in range(n_sub):
        advance(j)
    s_ref[0] = pk[0]
    s_ref[1] = pk[1]
    attb = att.astype(BF16)
    same_sub = (lax.broadcasted_iota(jnp.int32, (L, L), 0) // SUB
                == lax.broadcasted_iota(jnp.int32, (L, L), 1) // SUB)
    lane_head = lax.broadcasted_iota(jnp.int32, (1, HB_W), 1) // HEAD_DIM
    o = jnp.zeros((L, HB_W), F32)
    for h in range(N_HEADS):
        full = jnp.where(same_sub, _dot(attb, erep_ref[h]), 0.0)
        o = jnp.where(lane_head == h, _dot(full.astype(BF16), vb), o)

    o_inter = jnp.concatenate(o_parts, axis=0)
    if rows < L:
        o_inter = _pad_rows(o_inter, L)
    o = o + o_inter

    ms = _dot_exact_right(o * o, eblk) * (1.0 / HEAD_DIM)
    yc = o * lax.rsqrt(ms + RMS_EPS) * hng_ref[...] * (gate * jax.nn.sigmoid(gate))
    yc_ref[...] = yc[:rows]

    @pl.when(step == pl.num_programs(1) - 1)
    def _():
        s1_ref[0] = pk[0]
        s1_ref[1] = pk[1]


def _hgrn_rows(t_len):
    rows = _tile(t_len, HGRN_ROWS, SUB)
    return rows, max(rows, SCAN_ROWS)


def _hgrn(c_all, lb, hn_g, s0_pk, tri16, eblk, esel, erep, batch, t_len):
    rows, length = _hgrn_rows(t_len)
    steps = t_len // rows
    n = batch * t_len
    pk_spec = pl.BlockSpec((None, 2, HEAD_DIM, 2 * HEAD_DIM), lambda bi, ci: (bi, 0, 0, 0))
    return pl.pallas_call(
        functools.partial(_hgrn_body, rows=rows, length=length),
        grid=(batch, steps),
        in_specs=[pl.BlockSpec((rows, 4 * HB_W), lambda bi, ci: (bi * steps + ci, 0)),
                  _const_spec((1, HB_W)), _const_spec((1, HB_W)), pk_spec,
                  _const_spec(tri16.shape), _const_spec(eblk.shape), _const_spec(esel.shape),
                  _const_spec(erep.shape)],
        out_specs=[pl.BlockSpec((rows, HB_W), lambda bi, ci: (bi * steps + ci, 0)), pk_spec],
        out_shape=[jax.ShapeDtypeStruct((n, HB_W), F32),
                   jax.ShapeDtypeStruct((batch, 2, HEAD_DIM, 2 * HEAD_DIM), F32)],
        scratch_shapes=[pltpu.VMEM((2, HEAD_DIM, 2 * HEAD_DIM), F32)],
        compiler_params=_cparams(("parallel", "arbitrary")),
        name="hgrn",
    )(c_all, lb, hn_g, s0_pk, tri16, eblk, esel, erep)


def _attn_prompt_body(qi_tab, ki_tab, lam_ref, q_ref, k_ref, vt_ref, g_ref, o_ref, qx_ref, m_ref, acc_ref,
                      s_ref, bmax_ref, *, tq, out_scale):
    pair = pl.program_id(2)
    qi = qi_tab[pair]
    ki = ki_tab[pair]
    ns = tq // Q_STRIP
    dv = 2 * HEAD_DIM

    @pl.when(ki == 0)
    def _():
        qt = (q_ref[...] * (HEAD_DIM ** -0.5 * LOG2E)).T
        first = lax.broadcasted_iota(jnp.int32, (dv, 1), 0) < HEAD_DIM
        qa = jnp.where(first, qt, 0.0).astype(BF16)
        qb = jnp.where(first, 0.0, qt).astype(BF16)
        for st in range(ns):
            qx_ref[0, st] = qa[:, st * Q_STRIP:(st + 1) * Q_STRIP]
            qx_ref[1, st] = qb[:, st * Q_STRIP:(st + 1) * Q_STRIP]
        m_ref[...] = jnp.full(m_ref.shape, NEG_BIG, F32)
        acc_ref[...] = jnp.zeros(acc_ref.shape, F32)

    def scores(kb, st, diag):
        k = k_ref[kb * Q_STRIP:(kb + 1) * Q_STRIP, :]
        for mp in range(2):
            s = _dot(k, qx_ref[mp, st])
            if diag and st == kb:
                k_chunk = lax.broadcasted_iota(jnp.int32, (Q_STRIP, 1), 0) // CHUNK
                q_chunk = lax.broadcasted_iota(jnp.int32, (1, Q_STRIP), 1) // CHUNK
                s = jnp.where(k_chunk <= q_chunk, s, NEG_BIG)
            s_ref[kb % 2, mp, st] = s
            bmax_ref[kb % 2, mp, st] = jnp.max(s, axis=0, keepdims=True)

    def accumulate(kb, st):
        vt = vt_ref[kb]
        for mp in range(2):
            m_old = m_ref[mp, st]
            m_new = jnp.maximum(m_old, bmax_ref[kb % 2, mp, st])
            p = jnp.exp2(s_ref[kb % 2, mp, st] - m_new).astype(BF16)
            corr = jnp.exp2(m_old - m_new)
            acc_ref[mp, st] = corr * acc_ref[mp, st] + _dot(vt, p)
            m_ref[mp, st] = m_new

    def key_tile(diag):
        first_strip = (lambda kb: kb) if diag else (lambda kb: 0)
        for st in range(ns):
            scores(0, st, diag)
        for kb in range(ns):
            for st in range(first_strip(kb), ns):
                accumulate(kb, st)
                if kb + 1 < ns and st >= first_strip(kb + 1):
                    scores(kb + 1, st, diag)

    @pl.when(ki < qi)
    def _():
        key_tile(False)

    @pl.when(ki == qi)
    def _():
        key_tile(True)
        lam = lam_ref[0, 0]
        for st in range(ns):
            a1 = acc_ref[0, st]
            a2 = acc_ref[1, st]
            o = a1[:dv] / a1[dv:dv + 1] - lam * (a2[:dv] / a2[dv:dv + 1])
            ms = jnp.mean(o * o, axis=0, keepdims=True)
            o = o * lax.rsqrt(ms + RMS_EPS) * g_ref[...] * out_scale
            o_ref[st * Q_STRIP:(st + 1) * Q_STRIP, :] = o.T


def _attn_prompt(lam, bq, bkb, vt, dn_g, batch, t_len, out_scale):
    tq = _tile(t_len, ATT_TILE, Q_STRIP)
    nq = t_len // tq
    ns = tq // Q_STRIP
    n = batch * t_len
    w = 2 * HEAD_DIM
    pairs = [(qi, ki) for qi in range(nq) for ki in range(qi + 1)]
    qi_tab = jnp.asarray([p[0] for p in pairs], jnp.int32)
    ki_tab = jnp.asarray([p[1] for p in pairs], jnp.int32)
    qmap = lambda b, h, p, qt, kt: (b * nq + qt[p], h)
    kmap = lambda b, h, p, qt, kt: (b * nq + kt[p], h)
    vmap_ = lambda b, h, p, qt, kt: (h, b * nq + kt[p], 0, 0)
    grid_spec = pltpu.PrefetchScalarGridSpec(
        num_scalar_prefetch=2,
        grid=(batch, N_HEADS, len(pairs)),
        in_specs=[pl.BlockSpec(memory_space=pltpu.SMEM),
                  pl.BlockSpec((tq, w), qmap), pl.BlockSpec((tq, w), kmap),
                  pl.BlockSpec((None, ns, VT_ROWS, Q_STRIP), vmap_),
                  pl.BlockSpec((w, 1), lambda b, h, p, qt, kt: (0, 0))],
        out_specs=pl.BlockSpec((tq, w), qmap),
        scratch_shapes=[pltpu.VMEM((2, ns, w, Q_STRIP), BF16),
                        pltpu.VMEM((2, ns, 1, Q_STRIP), F32),
                        pltpu.VMEM((2, ns, VT_ROWS, Q_STRIP), F32),
                        pltpu.VMEM((2, 2, ns, Q_STRIP, Q_STRIP), F32),
                        pltpu.VMEM((2, 2, ns, 1, Q_STRIP), F32)])
    return pl.pallas_call(
        functools.partial(_attn_prompt_body, tq=tq, out_scale=out_scale),
        grid_spec=grid_spec,
        out_shape=jax.ShapeDtypeStruct((n, ATT_W), F32),
        compiler_params=_cparams(("parallel", "parallel", "arbitrary")),
        name="attn_prompt",
    )(qi_tab, ki_tab, lam, bq, bkb, vt, dn_g.reshape(w, 1))


def _attn_sample_body(lam_ref, q_ref, kn_ref, vn_ref, kp_ref, vp_ref, g_ref, o_ref, *, t_new, past, out_scale):
    w = 2 * HEAD_DIM
    first = lax.broadcasted_iota(jnp.int32, (1, w), 1) < HEAD_DIM
    for h in range(N_HEADS):
        cols = slice(h * w, (h + 1) * w)
        q = q_ref[:, cols] * (HEAD_DIM ** -0.5)
        q2 = jnp.concatenate([jnp.where(first, q, 0.0), jnp.where(first, 0.0, q)], axis=0).astype(BF16)
        kp = kp_ref[pl.ds(h, past, stride=N_HEADS), :].astype(BF16)
        vp = vp_ref[pl.ds(h, past, stride=N_HEADS), :].astype(BF16)
        s_p = _dot_nt(q2, kp)
        kn = kn_ref[pl.ds(h, t_new, stride=N_HEADS), :].astype(BF16)
        vn = vn_ref[pl.ds(h, t_new, stride=N_HEADS), :].astype(BF16)
        s_n = _dot_nt(q2, kn)
        m = jnp.maximum(jnp.max(s_p, axis=1, keepdims=True), jnp.max(s_n, axis=1, keepdims=True))
        p_p = jnp.exp(s_p - m)
        p_n = jnp.exp(s_n - m)
        l = jnp.sum(p_p, axis=1, keepdims=True) + jnp.sum(p_n, axis=1, keepdims=True)
        acc = _dot(p_p.astype(BF16), vp) + _dot(p_n.astype(BF16), vn)
        o2 = acc / l
        o = o2[:t_new] - lam_ref[0, 0] * o2[t_new:]
        ms = jnp.mean(o * o, axis=1, keepdims=True)
        o_ref[:, cols] = o * lax.rsqrt(ms + RMS_EPS) * g_ref[...] * out_scale


def _attn_sample(lam, bq, bk, bv, cache_k, cache_v, layer, dn_g, batch, t_new, out_scale):
    w = 2 * HEAD_DIM
    depth, _, past = cache_k.shape[:3]
    rows = past * N_HEADS
    new = pl.BlockSpec((t_new, ATT_W), lambda b: (b, 0))
    new_kv = pl.BlockSpec((None, t_new * N_HEADS, w), lambda b: (layer, b, 0))
    old = pl.BlockSpec((None, None, rows, w), lambda b: (layer, b, 0, 0))
    return pl.pallas_call(
        functools.partial(_attn_sample_body, t_new=t_new, past=past, out_scale=out_scale),
        grid=(batch,),
        in_specs=[pl.BlockSpec(memory_space=pltpu.SMEM), new, new_kv, new_kv, old, old, _const_spec((1, w))],
        out_specs=new,
        out_shape=jax.ShapeDtypeStruct((batch * t_new, ATT_W), F32),
        compiler_params=_cparams(("parallel",)),
        name="attn_sample",
    )(lam, bq, bk, bv, cache_k.reshape(depth, batch, rows, w), cache_v.reshape(depth, batch, rows, w),
      dn_g.reshape(1, w))


def _to_head_pairs(s):
    b = s.shape[0]
    r = s.reshape(b, N_HEADS // 2, 2, HEAD_DIM, HEAD_DIM)
    return jnp.swapaxes(r, 2, 3).reshape(b, N_HEADS // 2, HEAD_DIM, 2 * HEAD_DIM)


def _from_head_pairs(p):
    b = p.shape[0]
    r = p.reshape(b, N_HEADS // 2, HEAD_DIM, 2, HEAD_DIM)
    return jnp.swapaxes(r, 2, 3).reshape(b, N_HEADS, HEAD_DIM, HEAD_DIM)


def _rearrange_proj(a):
    gates = lambda c0: jnp.repeat(a[..., c0:c0 + N_HEADS], HEAD_DIM, axis=-1)
    out = jnp.concatenate([a[..., A_QK0:A_I0], a[..., A_O0:B_Q0], gates(A_I0), gates(A_F0), a[..., B_Q0:N_IN]],
                          axis=-1)
    assert out.shape[-1] == N_CAT
    return out


def _consts(hgrn_len):
    r = jnp.arange(SCAN_ROWS)
    tri = (r[None, :] <= r[:, None]).astype(BF16)
    r = jnp.arange(hgrn_len)
    tri16 = ((r[None, :] <= r[:, None]) & (r[None, :] // SUB == r[:, None] // SUB)).astype(BF16)
    hh = jnp.arange(HB_W) // HEAD_DIM
    eblk = (hh[None, :] == hh[:, None]).astype(BF16)
    lane = jnp.arange(2 * HEAD_DIM)
    src = jnp.arange(SUB)
    esel = ((hh[None, :, None] * SUB + src[:, None, None]) == lane[None, None, :]).astype(BF16)
    erep = ((lane[None, :, None] // SUB == jnp.arange(N_HEADS)[:, None, None])
            & (lane[None, :, None] % SUB == r[None, None, :] % SUB)).astype(BF16)
    return tri, tri16, eblk, esel, erep


def _group(x, depth, layer_w, states, cache, batch, t_len, alpha, consts):
    tri, tri16, eblk, esel, erep = consts
    prompt = cache is None
    outs = []
    kv_bufs = tuple(jnp.zeros((depth, batch * t_len * N_HEADS, 2 * HEAD_DIM), F32) for _ in range(2))
    for l in range(depth):
        w = layer_w[l]
        conv0, c0, n0, m0, s0 = states(l)
        x = _ffn_ln(x, w["f_in0"], w["f_out0"], w["ln_g"][0], w["ln_b"][0], alpha)
        res = _in_proj(x, w["w_cat"], w["b_cat"], w["w_vt"], w["b_vt"], prompt, l, depth, kv_bufs)
        proj = dict(zip(_PROJ_OUTS, res[:len(_PROJ_OUTS)]))
        bkb = res[len(_PROJ_OUTS)]
        kv_bufs = (proj["bk"], proj["bv"])

        conv0_p = jnp.pad(conv0, ((0, 0), (8 - (CONV_W - 1), 0), (0, 0)))
        ya, c1, n1, m1, convn = _mlstm(proj, w["conv_w"], w["conv_b"], conv0_p, _to_head_pairs(c0),
                                       n0.reshape(batch, 1, HB_W),
                                       jnp.repeat(m0, HEAD_DIM, axis=-1).reshape(batch, 1, HB_W),
                                       tri, eblk, batch, t_len)
        lam_init = 0.8 - 0.6 * math.exp(-0.3 * l)
        if prompt:
            yb = _attn_prompt(w["lam"], proj["bq"], bkb, res[len(_PROJ_OUTS) + 1], w["dn_g"], batch, t_len,
                              1.0 - lam_init)
        else:
            yb = _attn_sample(w["lam"], proj["bq"], proj["bk"], proj["bv"], cache[0], cache[1], l, w["dn_g"],
                              batch, t_len, 1.0 - lam_init)
        yc, s1 = _hgrn(proj["c"], w["lb"], w["hn_g"], _to_head_pairs(s0), tri16, eblk, esel, erep, batch, t_len)
        x = _out_proj_ffn(ya, yb, yc, x, w["w_out"], w["ln_g"][1], w["ln_b"][1],
                          w["f_in1"], w["f_out1"], w["ln_g"][2], w["ln_b"][2], alpha)
        outs.append((_from_head_pairs(c1),
                     n1.reshape(batch, N_HEADS, HEAD_DIM),
                     m1.reshape(batch, N_HEADS, HEAD_DIM)[:, :, 0],
                     convn[:, 8 - (CONV_W - 1):, :],
                     _from_head_pairs(s1)))
    kv = tuple(buf.reshape(depth, batch, t_len, N_HEADS, 2 * HEAD_DIM) for buf in kv_bufs)
    return x, kv + tuple(jnp.stack(a) for a in zip(*outs))


def kernel(x_prompt, x_sample, cache_diff_k, cache_diff_v, state_mlstm_c, state_mlstm_n, state_mlstm_m,
           state_mlstm_conv, state_hgrn_s, w_in, b_in, w_out, mlstm_conv_w, mlstm_conv_b, diff_lambda,
           diff_norm_g, hgrn_lb_logits, hgrn_norm_g, ffn_w_in, ffn_w_out, ln_g, ln_b):
    depth = w_in.shape[0]
    alpha = (2.0 * depth) ** 0.25
    batch, seq, d_model = x_prompt.shape
    dec_batch, dec_seq, _ = x_sample.shape
    past = cache_diff_k.shape[2]

    p_lb = jax.nn.softmax(hgrn_lb_logits.astype(F32), axis=0)
    lower_bounds = jnp.cumsum(p_lb, axis=0) - p_lb[0]
    w_cat = _rearrange_proj(w_in.astype(BF16))
    b_cat = _rearrange_proj(b_in).reshape(depth, 1, N_CAT)
    w_vt = jnp.swapaxes(w_in[:, :, B_V0:C_Q0], 1, 2).reshape(depth, N_HEADS, 2 * HEAD_DIM, d_model)
    w_vt = jnp.pad(w_vt, ((0, 0), (0, 0), (0, VT_ROWS - 2 * HEAD_DIM), (0, 0)))
    w_vt = w_vt.reshape(depth, N_HEADS * VT_ROWS, d_model).astype(BF16)
    b_vt = jnp.pad(b_in[:, B_V0:C_Q0].reshape(depth, N_HEADS, 2 * HEAD_DIM),
                   ((0, 0), (0, 0), (0, VT_ROWS - 2 * HEAD_DIM)))
    b_vt = b_vt.at[:, :, 2 * HEAD_DIM].set(1.0).reshape(depth, N_HEADS * VT_ROWS, 1)
    lp = diff_lambda.astype(F32)
    lam_init = jnp.asarray([0.8 - 0.6 * math.exp(-0.3 * l) for l in range(depth)], F32)
    lam = (jnp.exp(jnp.sum(lp[:, 0] * lp[:, 1], axis=-1)) - jnp.exp(jnp.sum(lp[:, 2] * lp[:, 3], axis=-1))
           + lam_init)
    f_in = ffn_w_in.astype(BF16)
    f_out = ffn_w_out.astype(BF16)
    w_out_b = w_out.astype(BF16)
    layer_w = [dict(w_cat=w_cat[l], b_cat=b_cat[l], w_vt=w_vt[l], b_vt=b_vt[l],
                    conv_w=mlstm_conv_w[l], conv_b=mlstm_conv_b[l].reshape(1, 2 * HB_W),
                    lam=lam[l].reshape(1, 1), dn_g=diff_norm_g[l],
                    lb=lower_bounds[l].reshape(1, HB_W),
                    hn_g=jnp.tile(hgrn_norm_g[l], N_HEADS).reshape(1, HB_W),
                    w_out=w_out_b[l], f_in0=f_in[l, 0], f_out0=f_out[l, 0], f_in1=f_in[l, 1], f_out1=f_out[l, 1],
                    ln_g=ln_g[l], ln_b=ln_b[l]) for l in range(depth)]
    def zero_states(_):
        return (jnp.zeros((batch, CONV_W - 1, 2 * HB_W), F32), jnp.zeros((batch, N_HEADS, HEAD_DIM, HEAD_DIM), F32),
                jnp.zeros((batch, N_HEADS, HEAD_DIM), F32), jnp.zeros((batch, N_HEADS), F32),
                jnp.zeros((batch, N_HEADS, HEAD_DIM, HEAD_DIM), F32))

    def carried_states(l):
        return (state_mlstm_conv[l], state_mlstm_c[l], state_mlstm_n[l], state_mlstm_m[l], state_hgrn_s[l])

    y_p, p_out = _group(x_prompt.reshape(batch * seq, d_model), depth, layer_w, zero_states, None,
                        batch, seq, alpha, _consts(_hgrn_rows(seq)[1]))
    cache = (cache_diff_k, cache_diff_v)
    y_s, s_out = _group(x_sample.reshape(dec_batch * dec_seq, d_model), depth, layer_w, carried_states, cache,
                        dec_batch, dec_seq, alpha, _consts(_hgrn_rows(dec_seq)[1]))
    return (y_p.reshape(batch, seq, d_model), y_s.reshape(dec_batch, dec_seq, d_model)) + p_out + s_out
```

```python
import functools
import math

import jax
import jax.numpy as jnp
from jax import lax
from jax.experimental import pallas as pl
from jax.experimental.pallas import tpu as pltpu

F32 = jnp.float32
BF16 = jnp.bfloat16

HEAD_DIM = 64
N_HEADS = 4
CONV_W = 4
CHUNK = 64
HB_W = N_HEADS * HEAD_DIM
ATT_W = N_HEADS * 2 * HEAD_DIM
LN_EPS = 1e-5
RMS_EPS = 1e-6
NEG_BIG = -1e30
SCAN_ROWS = 128
HGRN_ROWS = 256
SCAN_STREAMS = 1
SUB = 16
LOG2E = 1.4426950408889634
VT_ROWS = 2 * HEAD_DIM + 16
Q_STRIP = 256
ATT_TILE = 2048
V7X_VMEM_LIMIT = 56 * 1024 * 1024

A_QK0 = 0
A_V0 = A_QK0 + 2 * HB_W
A_I0 = A_V0 + HB_W
A_F0 = A_I0 + N_HEADS
A_O0 = A_F0 + N_HEADS
B_Q0 = A_O0 + HB_W
B_K0 = B_Q0 + ATT_W
B_V0 = B_K0 + ATT_W
C_Q0 = B_V0 + ATT_W
N_IN = C_Q0 + 4 * HB_W

_SEGS = {}
_off = 0
for _name, _w in (("aqk", 2 * HB_W), ("av", HB_W), ("ao", HB_W), ("gi", HB_W), ("gf", HB_W),
                  ("bq", ATT_W), ("bk", ATT_W), ("bv", ATT_W), ("c", 4 * HB_W)):
    _SEGS[_name] = (_off, _w)
    _off += _w
N_CAT = _off


def _tile(n, pref, mult=8):
    t = min(pref, n)
    while t > mult and (n % t or t % mult):
        t -= mult
    assert n % t == 0 and t % mult == 0, (n, pref, mult)
    return t


def _cparams(sem):
    return pltpu.CompilerParams(dimension_semantics=sem, vmem_limit_bytes=V7X_VMEM_LIMIT)


def _const_spec(shape):
    nd = len(shape)
    return pl.BlockSpec(shape, lambda *_: (0,) * nd)


def _dot(a, b):
    return jnp.dot(a, b, preferred_element_type=F32)


def _dot_nt(a, b):
    return lax.dot_general(a, b, (((1,), (1,)), ((), ())), preferred_element_type=F32)


def _split3(x):
    a = x.astype(BF16)
    r = x - a.astype(F32)
    b = r.astype(BF16)
    c = (r - b.astype(F32)).astype(BF16)
    return a, b, c


def _dot_exact_left(mat01, x):
    a, b, c = _split3(x)
    return _dot(mat01, a) + _dot(mat01, b) + _dot(mat01, c)


def _dot_exact_right(x, mat01):
    a, b, c = _split3(x)
    return _dot(a, mat01) + _dot(b, mat01) + _dot(c, mat01)


def _layer_norm(y, g, b):
    mu = jnp.mean(y, axis=-1, keepdims=True)
    d = y - mu
    var = jnp.mean(d * d, axis=-1, keepdims=True)
    return d * lax.rsqrt(var + LN_EPS) * g + b


def _round_robin(phased):
    last = [None] * len(phased)
    live = list(range(len(phased)))
    while live:
        for g in list(live):
            try:
                out = next(phased[g])
            except StopIteration:
                live.remove(g)
            else:
                if out is not None:
                    last[g] = out
    return last


def _pad_rows(x, rows):
    if x.shape[0] == rows:
        return x
    return jnp.concatenate([x, jnp.zeros((rows - x.shape[0],) + x.shape[1:], x.dtype)], axis=0)


def _swiglu_ln(x, wi_ref, wo_ref, g_ref, b_ref, acc_ref, alpha, d_ff, tf):
    xb = x.astype(BF16)
    for c in range(d_ff // tf):
        gate = _dot(xb, wi_ref[:, c * tf:(c + 1) * tf])
        up = _dot(xb, wi_ref[:, d_ff + c * tf:d_ff + (c + 1) * tf])
        h = (gate * jax.nn.sigmoid(gate) * up).astype(BF16)
        part = _dot(h, wo_ref[c * tf:(c + 1) * tf, :])
        if c == 0:
            acc_ref[...] = part
        else:
            acc_ref[...] += part
    y = alpha * x + 0.5 * acc_ref[...]
    return _layer_norm(y, g_ref[...], b_ref[...])


def _ffn_ln_body(x_ref, wi_ref, wo_ref, g_ref, b_ref, o_ref, acc_ref, *, alpha, d_ff, tf):
    o_ref[...] = _swiglu_ln(x_ref[...], wi_ref, wo_ref, g_ref, b_ref, acc_ref, alpha, d_ff, tf)


def _ffn_ln(x, wi, wo, g, b, alpha):
    n, d = x.shape
    d_ff = wo.shape[0]
    tm = _tile(n, 512)
    tf = _tile(d_ff, 256, 128)
    return pl.pallas_call(
        functools.partial(_ffn_ln_body, alpha=alpha, d_ff=d_ff, tf=tf),
        grid=(n // tm,),
        in_specs=[pl.BlockSpec((tm, d), lambda i: (i, 0)),
                  _const_spec(wi.shape), _const_spec(wo.shape),
                  _const_spec((1, d)), _const_spec((1, d))],
        out_specs=pl.BlockSpec((tm, d), lambda i: (i, 0)),
        out_shape=jax.ShapeDtypeStruct((n, d), F32),
        scratch_shapes=[pltpu.VMEM((tm, d), F32)],
        compiler_params=_cparams(("parallel",)),
        name="ffn_ln",
    )(x, wi, wo, g.reshape(1, d), b.reshape(1, d))


_PROJ_OUTS = ("aqk", "av", "ao", "gi", "gf", "bq", "bk", "bv", "c")


def _in_proj_body(*refs, emit_vt, n_alias):
    x_ref, w_ref, bias_ref = refs[:3]
    out_refs = refs[3 + n_alias:]
    xb = x_ref[...].astype(BF16)
    tm = xb.shape[0]
    outs = dict(zip(_PROJ_OUTS, out_refs))
    dv = 2 * HEAD_DIM
    ones_rows = jnp.where(lax.broadcasted_iota(jnp.int32, (VT_ROWS - dv, Q_STRIP), 0) == 0, 1.0, 0.0).astype(BF16)
    for name in _PROJ_OUTS:
        c0, w = _SEGS[name]
        for s in range(0, w, 2 * HB_W):
            e = min(s + 2 * HB_W, w)
            val = _dot(xb, w_ref[:, c0 + s:c0 + e]) + bias_ref[:, c0 + s:c0 + e]
            if name in ("bk", "bv"):
                for h in range(N_HEADS):
                    outs[name][pl.ds(h, tm, stride=N_HEADS), :] = val[:, h * 2 * HEAD_DIM:(h + 1) * 2 * HEAD_DIM]
            else:
                outs[name][:, s:e] = val
            if name == "bk":
                out_refs[len(_PROJ_OUTS)][:, s:e] = val.astype(BF16)
            if name == "bv" and emit_vt:
                vt_ref = out_refs[len(_PROJ_OUTS) + 1]
                for h in range(N_HEADS):
                    vt = val[:, h * dv:(h + 1) * dv].T.astype(BF16)
                    for j in range(tm // Q_STRIP):
                        vt_ref[h, j, 0:dv, :] = vt[:, j * Q_STRIP:(j + 1) * Q_STRIP]
                        vt_ref[h, j, dv:VT_ROWS, :] = ones_rows


def _in_proj(x, w_cat, b_cat, emit_vt, layer, depth, kv_bufs):
    n, d = x.shape
    tm = _tile(n, 512, Q_STRIP if emit_vt else 8)
    out_shape, out_specs = [], []
    for k in _PROJ_OUTS:
        if k in ("bk", "bv"):
            out_shape.append(jax.ShapeDtypeStruct((depth, n * N_HEADS, 2 * HEAD_DIM), F32))
            out_specs.append(pl.BlockSpec((None, tm * N_HEADS, 2 * HEAD_DIM), lambda i: (layer, i, 0)))
        else:
            out_shape.append(jax.ShapeDtypeStruct((n, _SEGS[k][1]), F32))
            out_specs.append(pl.BlockSpec((tm, _SEGS[k][1]), lambda i: (i, 0)))
    out_shape.append(jax.ShapeDtypeStruct((n, ATT_W), BF16))
    out_specs.append(pl.BlockSpec((tm, ATT_W), lambda i: (i, 0)))
    if emit_vt:
        out_shape.append(jax.ShapeDtypeStruct((N_HEADS, n // Q_STRIP, VT_ROWS, Q_STRIP), BF16))
        out_specs.append(pl.BlockSpec((N_HEADS, tm // Q_STRIP, VT_ROWS, Q_STRIP), lambda i: (0, i, 0, 0)))
    in_specs = [pl.BlockSpec((tm, d), lambda i: (i, 0)), _const_spec(w_cat.shape), _const_spec(b_cat.shape)]
    args = [x, w_cat, b_cat]
    aliases = {}
    for buf, name in zip(kv_bufs, ("bk", "bv")):
        aliases[len(args)] = _PROJ_OUTS.index(name)
        in_specs.append(pl.BlockSpec(memory_space=pl.ANY))
        args.append(buf)
    return pl.pallas_call(
        functools.partial(_in_proj_body, emit_vt=emit_vt, n_alias=len(aliases)),
        grid=(n // tm,),
        in_specs=in_specs,
        out_specs=out_specs,
        out_shape=out_shape,
        input_output_aliases=aliases,
        compiler_params=_cparams(("parallel",)),
        name="in_proj",
    )(*args)


def _out_proj_ffn_body(ya_ref, yb_ref, yc_ref, x_ref, w_ref, g1_ref, b1_ref, wi_ref, wo_ref, g2_ref, b2_ref,
                       o_ref, acc_ref, *, alpha, d_ff, tf):
    y = _dot(ya_ref[...].astype(BF16), w_ref[0:HB_W, :])
    y += _dot(yb_ref[...].astype(BF16), w_ref[HB_W:HB_W + ATT_W, :])
    y += _dot(yc_ref[...].astype(BF16), w_ref[HB_W + ATT_W:, :])
    x = _layer_norm(alpha * x_ref[...] + y, g1_ref[...], b1_ref[...])
    o_ref[...] = _swiglu_ln(x, wi_ref, wo_ref, g2_ref, b2_ref, acc_ref, alpha, d_ff, tf)


def _out_proj_ffn(ya, yb, yc, x, w, g1, b1, wi, wo, g2, b2, alpha):
    n, d = x.shape
    d_ff = wo.shape[0]
    tm = _tile(n, 512)
    tf = _tile(d_ff, 256, 128)
    row = lambda width: pl.BlockSpec((tm, width), lambda i: (i, 0))
    vec = _const_spec((1, d))
    return pl.pallas_call(
        functools.partial(_out_proj_ffn_body, alpha=alpha, d_ff=d_ff, tf=tf),
        grid=(n // tm,),
        in_specs=[row(HB_W), row(ATT_W), row(HB_W), row(d), _const_spec(w.shape), vec, vec,
                  _const_spec(wi.shape), _const_spec(wo.shape), vec, vec],
        out_specs=row(d),
        out_shape=jax.ShapeDtypeStruct((n, d), F32),
        scratch_shapes=[pltpu.VMEM((tm, d), F32)],
        compiler_params=_cparams(("parallel",)),
        name="out_proj_ffn",
    )(ya, yb, yc, x, w, g1.reshape(1, d), b1.reshape(1, d), wi, wo, g2.reshape(1, d), b2.reshape(1, d))


def _mlstm_body(qk_ref, v_ref, o_ref, gi_ref, gf_ref, cw_ref, cb_ref, conv0_ref, c0_ref, n0_ref, m0_ref,
                tri_ref, eblk_ref,
                ya_ref, c1_ref, n1_ref, m1_ref, convn_ref,
                up_ref, c_ref, n_ref, m_ref, *, rows, streams):
    step = pl.program_id(1)
    low = lax.broadcasted_iota(jnp.int32, (1, 2 * HEAD_DIM), 1) < HEAD_DIM
    pw = 2 * HEAD_DIM

    @pl.when(step == 0)
    def _():
        for g in range(streams):
            up_ref[g, 0:8, :] = conv0_ref[g]
            c_ref[g] = jnp.zeros(c_ref.shape[1:], F32)
            for i in range(N_HEADS // 2):
                pair = c0_ref[g, i]
                c_ref[g, i * pw:i * pw + HEAD_DIM, i * pw:(i + 1) * pw] = jnp.where(low, pair, 0.0)
                c_ref[g, i * pw + HEAD_DIM:(i + 1) * pw, i * pw:(i + 1) * pw] = jnp.where(low, 0.0, pair)
        n_ref[...] = n0_ref[...]
        m_ref[...] = m0_ref[...]

    new_states = _round_robin([_mlstm_stream(
        qk_ref.at[g], v_ref.at[g], o_ref.at[g], gi_ref.at[g], gf_ref.at[g], cw_ref, cb_ref, tri_ref, eblk_ref,
        ya_ref.at[g], convn_ref.at[g], up_ref.at[g], c_ref.at[g], n_ref.at[g], m_ref.at[g], rows=rows)
        for g in range(streams)])

    @pl.when(step == pl.num_programs(1) - 1)
    def _():
        for g, (c_new, n_new, m_new) in enumerate(new_states):
            for i in range(N_HEADS // 2):
                c1_ref[g, i] = jnp.where(low, c_new[i * pw:i * pw + HEAD_DIM, i * pw:(i + 1) * pw],
                                         c_new[i * pw + HEAD_DIM:(i + 1) * pw, i * pw:(i + 1) * pw])
            n1_ref[g] = n_new
            m1_ref[g] = m_new


def _mlstm_stream(qk_ref, v_ref, o_ref, gi_ref, gf_ref, cw_ref, cb_ref, tri_ref, eblk_ref,
                  ya_ref, convn_ref, up_ref, c_ref, n_ref, m_ref, *, rows):
    L = SCAN_ROWS

    gi = _pad_rows(gi_ref[...], L)
    flog = jax.nn.log_sigmoid(_pad_rows(gf_ref[...], L))
    if rows < L:
        live = lax.broadcasted_iota(jnp.int32, (L, 1), 0) < rows
        gi = jnp.where(live, gi, NEG_BIG)
        flog = jnp.where(live, flog, 0.0)
    b = _dot_exact_left(tri_ref[...], flog)
    b_t = b.T
    gi_t = gi.T
    yield

    u = _pad_rows(qk_ref[...], L)
    up_ref[8:8 + L, :] = u
    y = cb_ref[...]
    for j in range(CONV_W):
        y = y + cw_ref[j:j + 1, :] * up_ref[5 + j:5 + j + L, :]
    convn_ref[...] = up_ref[rows:rows + 8, :]
    up_ref[0:8, :] = up_ref[L:L + 8, :]
    qk = y * jax.nn.sigmoid(y)
    q = qk[:, :HB_W] * (HEAD_DIM ** -0.5)
    k = qk[:, HB_W:]
    v = _pad_rows(v_ref[...], L)
    kb = k.astype(BF16)
    vb = v.astype(BF16)
    qb = q.astype(BF16)

    lane_head = lax.broadcasted_iota(jnp.int32, (1, HB_W), 1) // HEAD_DIM
    qk_scores = [_dot_nt(jnp.where(lane_head == h, q, 0.0).astype(BF16), kb) for h in range(N_HEADS)]
    c_prev = c_ref[...]
    n_prev = n_ref[...]
    eblk = eblk_ref[...]
    num_state = _dot(qb, c_prev.astype(BF16))
    den_state = _dot((q * n_prev).astype(BF16), eblk)
    yield

    m_prev = m_ref[...]
    g = b + m_prev

    t_idx = lax.broadcasted_iota(jnp.int32, (L, L), 0)
    s_idx = lax.broadcasted_iota(jnp.int32, (L, L), 1)
    causal = s_idx <= t_idx

    mrow_hb = jnp.zeros((L, HB_W), F32)
    wg_hb = jnp.zeros((L, HB_W), F32)
    den_hb = jnp.zeros((L, HB_W), F32)
    num_hb = jnp.zeros((L, HB_W), F32)
    for h in range(N_HEADS):
        sel = lane_head == h
        c0 = h * HEAD_DIM
        dmat = b[:, c0:c0 + 1] - b_t[c0:c0 + 1, :] + gi_t[c0:c0 + 1, :]
        dmat = jnp.where(causal, dmat, NEG_BIG)
        gcol = g[:, c0:c0 + 1]
        mrow = jnp.maximum(gcol, jnp.max(dmat, axis=1, keepdims=True))
        wd = jnp.exp(dmat - mrow)
        wg = jnp.exp(gcol - mrow)
        qkw = qk_scores[h] * wd
        den = jnp.sum(qkw, axis=1, keepdims=True)
        num = _dot(qkw.astype(BF16), vb)
        mrow_hb = jnp.where(sel, mrow, mrow_hb)
        wg_hb = jnp.where(sel, wg, wg_hb)
        den_hb = jnp.where(sel, den, den_hb)
        num_hb = jnp.where(sel, num, num_hb)

    num_hb = wg_hb * num_state + num_hb
    den_hb = wg_hb * den_state + den_hb
    hout = num_hb / jnp.maximum(jnp.abs(den_hb), jnp.exp(-mrow_hb))
    ya = jax.nn.sigmoid(_pad_rows(o_ref[...], L)) * hout
    ya_ref[...] = ya[:rows]
    yield

    m_new = mrow_hb[L - 1:L, :]
    b_last = b[L - 1:L, :]
    decay = jnp.exp(b_last + m_prev - m_new)
    ws = jnp.exp(b_last - b + gi - m_new)
    kw = k * ws
    upd = _dot(kw.T.astype(BF16), vb) * eblk.astype(F32)
    c_new = decay * c_prev + upd
    n_new = decay * n_prev + jnp.sum(kw, axis=0, keepdims=True)
    c_ref[...] = c_new
    n_ref[...] = n_new
    m_ref[...] = m_new
    yield c_new, n_new, m_new


def _mlstm(proj, conv_w, conv_b, conv0, c0_pk, n0, m0_hb, tri, eblk, batch, t_len):
    rows = min(SCAN_ROWS, t_len)
    steps = t_len // rows
    g = SCAN_STREAMS
    tok = lambda width: pl.BlockSpec((g, rows, width), lambda bi, ci: (bi, ci, 0))
    per_b = lambda shape: pl.BlockSpec((g,) + shape, lambda bi, ci: (bi,) + (0,) * len(shape))
    seq = lambda a: a.reshape(batch, t_len, a.shape[-1])
    outs = pl.pallas_call(
        functools.partial(_mlstm_body, rows=rows, streams=g),
        grid=(batch // g, steps),
        in_specs=[tok(2 * HB_W), tok(HB_W), tok(HB_W), tok(HB_W), tok(HB_W),
                  _const_spec(conv_w.shape), _const_spec(conv_b.shape),
                  per_b((8, 2 * HB_W)), per_b((N_HEADS // 2, HEAD_DIM, 2 * HEAD_DIM)), per_b((1, HB_W)), per_b((1, HB_W)),
                  _const_spec(tri.shape), _const_spec(eblk.shape)],
        out_specs=[tok(HB_W), per_b((N_HEADS // 2, HEAD_DIM, 2 * HEAD_DIM)), per_b((1, HB_W)), per_b((1, HB_W)),
                   per_b((8, 2 * HB_W))],
        out_shape=[jax.ShapeDtypeStruct((batch, t_len, HB_W), F32),
                   jax.ShapeDtypeStruct((batch, N_HEADS // 2, HEAD_DIM, 2 * HEAD_DIM), F32),
                   jax.ShapeDtypeStruct((batch, 1, HB_W), F32),
                   jax.ShapeDtypeStruct((batch, 1, HB_W), F32),
                   jax.ShapeDtypeStruct((batch, 8, 2 * HB_W), F32)],
        scratch_shapes=[pltpu.VMEM((g, SCAN_ROWS + 8, 2 * HB_W), F32),
                        pltpu.VMEM((g, HB_W, HB_W), F32),
                        pltpu.VMEM((g, 1, HB_W), F32),
                        pltpu.VMEM((g, 1, HB_W), F32)],
        compiler_params=_cparams(("parallel", "arbitrary")),
        name="mlstm",
    )(seq(proj["aqk"]), seq(proj["av"]), seq(proj["ao"]), seq(proj["gi"]), seq(proj["gf"]), conv_w, conv_b,
      conv0, c0_pk, n0, m0_hb, tri, eblk)
    return (outs[0].reshape(batch * t_len, HB_W),) + tuple(outs[1:])


def _hgrn_body(c_ref, lb_ref, hng_ref, s0_ref, tri16_ref, eblk_ref, esel_ref, erep_ref, yc_ref, s1_ref, s_ref,
               *, rows, length):
    step = pl.program_id(1)
    L = length

    @pl.when(step == 0)
    def _():
        s_ref[...] = s0_ref[...]

    cin = _pad_rows(c_ref[...], L)
    q = cin[:, 0:HB_W]
    f_pre = cin[:, HB_W:2 * HB_W]
    v = cin[:, 2 * HB_W:3 * HB_W]
    gate = cin[:, 3 * HB_W:]
    lb = lb_ref[...]
    forget = lb + (1.0 - lb) * jax.nn.sigmoid(f_pre)
    logf = jnp.log(forget)
    key = 1.0 - forget
    if rows < L:
        live = lax.broadcasted_iota(jnp.int32, (L, 1), 0) < rows
        logf = jnp.where(live, logf, 0.0)
        key = jnp.where(live, key, 0.0)
    bl = _dot_exact_left(tri16_ref[...], logf) * LOG2E
    eblk = eblk_ref[...]
    groups = L // SUB
    t_loc = lax.broadcasted_iota(jnp.int32, (L, 1), 0) % SUB
    vb = v.astype(BF16)

    def from_source(x, s):
        picked = x.reshape(groups, SUB, HB_W)[:, s:s + 1, :]
        return jnp.broadcast_to(picked, (groups, SUB, HB_W)).reshape(L, HB_W)

    n_sub = max(rows // SUB, 1)
    qd = (q * jnp.exp2(bl)).astype(BF16)
    bl_t = bl.T
    low = lax.broadcasted_iota(jnp.int32, (1, 2 * HEAD_DIM), 1) < HEAD_DIM
    zero_b = jnp.zeros((HEAD_DIM, 2 * HEAD_DIM), BF16)

    def contribution(j):
        r0 = j * SUB
        b_last = bl[r0 + SUB - 1:r0 + SUB, :]
        kw_j = (key[r0:r0 + SUB] * jnp.exp2(b_last - bl[r0:r0 + SUB])).astype(BF16)
        upd = lax.dot_general(kw_j, vb[r0:r0 + SUB], (((0,), (0,)), ((), ())),
                              preferred_element_type=F32)
        dec = jnp.exp2(bl_t[:, r0 + SUB - 1:r0 + SUB])
        pair = []
        for i in range(2):
            rk = 2 * i * HEAD_DIM
            u = jnp.where(low, upd[rk:rk + HEAD_DIM, rk:rk + 2 * HEAD_DIM],
                          upd[rk + HEAD_DIM:rk + 2 * HEAD_DIM, rk:rk + 2 * HEAD_DIM])
            d = jnp.where(low, dec[rk:rk + HEAD_DIM], dec[rk + HEAD_DIM:rk + 2 * HEAD_DIM])
            pair.append((d, u))
        return pair

    pk = [s_ref[0], s_ref[1]]
    o_parts = []

    def advance(j):
        r0 = j * SUB
        pb = [x.astype(BF16) for x in pk]
        s_bd = jnp.concatenate([
            jnp.concatenate([jnp.where(low, pb[0], 0.0).astype(BF16), zero_b], axis=1),
            jnp.concatenate([jnp.where(low, 0.0, pb[0]).astype(BF16), zero_b], axis=1),
            jnp.concatenate([zero_b, jnp.where(low, pb[1], 0.0).astype(BF16)], axis=1),
            jnp.concatenate([zero_b, jnp.where(low, 0.0, pb[1]).astype(BF16)], axis=1)], axis=0)
        o_parts.append(_dot(qd[r0:r0 + SUB], s_bd))
        pk[:] = [d * p + u for p, (d, u) in zip(pk, contrib[j])]

    contrib = []
    att = jnp.zeros((L, 2 * HEAD_DIM), F32)
    for s in range(SUB):
        w = jnp.exp2(bl - from_source(bl, s))
        p = jnp.where(t_loc >= s, q * from_source(key, s) * w, 0.0)
        att = att + _dot(p.astype(BF16), esel_ref[s])
        while len(contrib) < ((s + 1) * n_sub) // SUB:
            contrib.append(contribution(len(contrib)))
    for j in range(n_sub):
        advance(j)
    s_ref[0] = pk[0]
    s_ref[1] = pk[1]
    attb = att.astype(BF16)
    same_sub = (lax.broadcasted_iota(jnp.int32, (L, L), 0) // SUB
                == lax.broadcasted_iota(jnp.int32, (L, L), 1) // SUB)
    lane_head = lax.broadcasted_iota(jnp.int32, (1, HB_W), 1) // HEAD_DIM
    o = jnp.zeros((L, HB_W), F32)
    for h in range(N_HEADS):
        full = jnp.where(same_sub, _dot(attb, erep_ref[h]), 0.0)
        o = jnp.where(lane_head == h, _dot(full.astype(BF16), vb), o)

    o_inter = jnp.concatenate(o_parts, axis=0)
    if rows < L:
        o_inter = _pad_rows(o_inter, L)
    o = o + o_inter

    ms = _dot_exact_right(o * o, eblk) * (1.0 / HEAD_DIM)
    yc = o * lax.rsqrt(ms + RMS_EPS) * hng_ref[...] * (gate * jax.nn.sigmoid(gate))
    yc_ref[...] = yc[:rows]

    @pl.when(step == pl.num_programs(1) - 1)
    def _():
        s1_ref[0] = pk[0]
        s1_ref[1] = pk[1]


def _hgrn_rows(t_len):
    rows = _tile(t_len, HGRN_ROWS, SUB)
    return rows, max(rows, SCAN_ROWS)


def _hgrn(c_all, lb, hn_g, s0_pk, tri16, eblk, esel, erep, batch, t_len):
    rows, length = _hgrn_rows(t_len)
    steps = t_len // rows
    n = batch * t_len
    pk_spec = pl.BlockSpec((None, 2, HEAD_DIM, 2 * HEAD_DIM), lambda bi, ci: (bi, 0, 0, 0))
    return pl.pallas_call(
        functools.partial(_hgrn_body, rows=rows, length=length),
        grid=(batch, steps),
        in_specs=[pl.BlockSpec((rows, 4 * HB_W), lambda bi, ci: (bi * steps + ci, 0)),
                  _const_spec((1, HB_W)), _const_spec((1, HB_W)), pk_spec,
                  _const_spec(tri16.shape), _const_spec(eblk.shape), _const_spec(esel.shape),
                  _const_spec(erep.shape)],
        out_specs=[pl.BlockSpec((rows, HB_W), lambda bi, ci: (bi * steps + ci, 0)), pk_spec],
        out_shape=[jax.ShapeDtypeStruct((n, HB_W), F32),
                   jax.ShapeDtypeStruct((batch, 2, HEAD_DIM, 2 * HEAD_DIM), F32)],
        scratch_shapes=[pltpu.VMEM((2, HEAD_DIM, 2 * HEAD_DIM), F32)],
        compiler_params=_cparams(("parallel", "arbitrary")),
        name="hgrn",
    )(c_all, lb, hn_g, s0_pk, tri16, eblk, esel, erep)


def _attn_prompt_body(qi_tab, ki_tab, lam_ref, q_ref, k_ref, vt_ref, g_ref, o_ref, qx_ref, m_ref, acc_ref,
                      s_ref, bmax_ref, *, tq, out_scale):
    pair = pl.program_id(2)
    qi = qi_tab[pair]
    ki = ki_tab[pair]
    ns = tq // Q_STRIP
    dv = 2 * HEAD_DIM

    @pl.when(ki == 0)
    def _():
        qt = (q_ref[...] * (HEAD_DIM ** -0.5 * LOG2E)).T
        first = lax.broadcasted_iota(jnp.int32, (dv, 1), 0) < HEAD_DIM
        qa = jnp.where(first, qt, 0.0).astype(BF16)
        qb = jnp.where(first, 0.0, qt).astype(BF16)
        for st in range(ns):
            qx_ref[0, st] = qa[:, st * Q_STRIP:(st + 1) * Q_STRIP]
            qx_ref[1, st] = qb[:, st * Q_STRIP:(st + 1) * Q_STRIP]
        m_ref[...] = jnp.full(m_ref.shape, NEG_BIG, F32)
        acc_ref[...] = jnp.zeros(acc_ref.shape, F32)

    def scores(kb, st, diag):
        k = k_ref[kb * Q_STRIP:(kb + 1) * Q_STRIP, :]
        for mp in range(2):
            s = _dot(k, qx_ref[mp, st])
            if diag and st == kb:
                k_chunk = lax.broadcasted_iota(jnp.int32, (Q_STRIP, 1), 0) // CHUNK
                q_chunk = lax.broadcasted_iota(jnp.int32, (1, Q_STRIP), 1) // CHUNK
                s = jnp.where(k_chunk <= q_chunk, s, NEG_BIG)
            s_ref[kb % 2, mp, st] = s
            bmax_ref[kb % 2, mp, st] = jnp.max(s, axis=0, keepdims=True)

    def accumulate(kb, st):
        vt = vt_ref[kb]
        for mp in range(2):
            m_old = m_ref[mp, st]
            m_new = jnp.maximum(m_old, bmax_ref[kb % 2, mp, st])
            p = jnp.exp2(s_ref[kb % 2, mp, st] - m_new).astype(BF16)
            corr = jnp.exp2(m_old - m_new)
            acc_ref[mp, st] = corr * acc_ref[mp, st] + _dot(vt, p)
            m_ref[mp, st] = m_new

    def key_tile(diag):
        first_strip = (lambda kb: kb) if diag else (lambda kb: 0)
        for st in range(ns):
            scores(0, st, diag)
        for kb in range(ns):
            for st in range(first_strip(kb), ns):
                accumulate(kb, st)
                if kb + 1 < ns and st >= first_strip(kb + 1):
                    scores(kb + 1, st, diag)

    @pl.when(ki < qi)
    def _():
        key_tile(False)

    @pl.when(ki == qi)
    def _():
        key_tile(True)
        lam = lam_ref[0, 0]
        for st in range(ns):
            a1 = acc_ref[0, st]
            a2 = acc_ref[1, st]
            o = a1[:dv] / a1[dv:dv + 1] - lam * (a2[:dv] / a2[dv:dv + 1])
            ms = jnp.mean(o * o, axis=0, keepdims=True)
            o = o * lax.rsqrt(ms + RMS_EPS) * g_ref[...] * out_scale
            o_ref[st * Q_STRIP:(st + 1) * Q_STRIP, :] = o.T


def _attn_prompt(lam, bq, bkb, vt, dn_g, batch, t_len, out_scale):
    tq = _tile(t_len, ATT_TILE, Q_STRIP)
    nq = t_len // tq
    ns = tq // Q_STRIP
    n = batch * t_len
    w = 2 * HEAD_DIM
    pairs = [(qi, ki) for qi in range(nq) for ki in range(qi + 1)]
    qi_tab = jnp.asarray([p[0] for p in pairs], jnp.int32)
    ki_tab = jnp.asarray([p[1] for p in pairs], jnp.int32)
    qmap = lambda b, h, p, qt, kt: (b * nq + qt[p], h)
    kmap = lambda b, h, p, qt, kt: (b * nq + kt[p], h)
    vmap_ = lambda b, h, p, qt, kt: (h, b * nq + kt[p], 0, 0)
    grid_spec = pltpu.PrefetchScalarGridSpec(
        num_scalar_prefetch=2,
        grid=(batch, N_HEADS, len(pairs)),
        in_specs=[pl.BlockSpec(memory_space=pltpu.SMEM),
                  pl.BlockSpec((tq, w), qmap), pl.BlockSpec((tq, w), kmap),
                  pl.BlockSpec((None, ns, VT_ROWS, Q_STRIP), vmap_),
                  pl.BlockSpec((w, 1), lambda b, h, p, qt, kt: (0, 0))],
        out_specs=pl.BlockSpec((tq, w), qmap),
        scratch_shapes=[pltpu.VMEM((2, ns, w, Q_STRIP), BF16),
                        pltpu.VMEM((2, ns, 1, Q_STRIP), F32),
                        pltpu.VMEM((2, ns, VT_ROWS, Q_STRIP), F32),
                        pltpu.VMEM((2, 2, ns, Q_STRIP, Q_STRIP), F32),
                        pltpu.VMEM((2, 2, ns, 1, Q_STRIP), F32)])
    return pl.pallas_call(
        functools.partial(_attn_prompt_body, tq=tq, out_scale=out_scale),
        grid_spec=grid_spec,
        out_shape=jax.ShapeDtypeStruct((n, ATT_W), F32),
        compiler_params=_cparams(("parallel", "parallel", "arbitrary")),
        name="attn_prompt",
    )(qi_tab, ki_tab, lam, bq, bkb, vt, dn_g.reshape(w, 1))


def _attn_sample_body(lam_ref, q_ref, kn_ref, vn_ref, kp_ref, vp_ref, g_ref, o_ref, *, t_new, past, out_scale):
    w = 2 * HEAD_DIM
    first = lax.broadcasted_iota(jnp.int32, (1, w), 1) < HEAD_DIM
    for h in range(N_HEADS):
        cols = slice(h * w, (h + 1) * w)
        q = q_ref[:, cols] * (HEAD_DIM ** -0.5)
        q2 = jnp.concatenate([jnp.where(first, q, 0.0), jnp.where(first, 0.0, q)], axis=0).astype(BF16)
        kp = kp_ref[pl.ds(h, past, stride=N_HEADS), :].astype(BF16)
        vp = vp_ref[pl.ds(h, past, stride=N_HEADS), :].astype(BF16)
        s_p = _dot_nt(q2, kp)
        kn = kn_ref[pl.ds(h, t_new, stride=N_HEADS), :].astype(BF16)
        vn = vn_ref[pl.ds(h, t_new, stride=N_HEADS), :].astype(BF16)
        s_n = _dot_nt(q2, kn)
        m = jnp.maximum(jnp.max(s_p, axis=1, keepdims=True), jnp.max(s_n, axis=1, keepdims=True))
        p_p = jnp.exp(s_p - m)
        p_n = jnp.exp(s_n - m)
        l = jnp.sum(p_p, axis=1, keepdims=True) + jnp.sum(p_n, axis=1, keepdims=True)
        acc = _dot(p_p.astype(BF16), vp) + _dot(p_n.astype(BF16), vn)
        o2 = acc / l
        o = o2[:t_new] - lam_ref[0, 0] * o2[t_new:]
        ms = jnp.mean(o * o, axis=1, keepdims=True)
        o_ref[:, cols] = o * lax.rsqrt(ms + RMS_EPS) * g_ref[...] * out_scale


def _attn_sample(lam, bq, bk, bv, cache_k, cache_v, layer, dn_g, batch, t_new, out_scale):
    w = 2 * HEAD_DIM
    depth, _, past = cache_k.shape[:3]
    rows = past * N_HEADS
    new = pl.BlockSpec((t_new, ATT_W), lambda b: (b, 0))
    new_kv = pl.BlockSpec((None, t_new * N_HEADS, w), lambda b: (layer, b, 0))
    old = pl.BlockSpec((None, None, rows, w), lambda b: (layer, b, 0, 0))
    return pl.pallas_call(
        functools.partial(_attn_sample_body, t_new=t_new, past=past, out_scale=out_scale),
        grid=(batch,),
        in_specs=[pl.BlockSpec(memory_space=pltpu.SMEM), new, new_kv, new_kv, old, old, _const_spec((1, w))],
        out_specs=new,
        out_shape=jax.ShapeDtypeStruct((batch * t_new, ATT_W), F32),
        compiler_params=_cparams(("parallel",)),
        name="attn_sample",
    )(lam, bq, bk, bv, cache_k.reshape(depth, batch, rows, w), cache_v.reshape(depth, batch, rows, w),
      dn_g.reshape(1, w))


def _to_head_pairs(s):
    b = s.shape[0]
    r = s.reshape(b, N_HEADS // 2, 2, HEAD_DIM, HEAD_DIM)
    return jnp.swapaxes(r, 2, 3).reshape(b, N_HEADS // 2, HEAD_DIM, 2 * HEAD_DIM)


def _from_head_pairs(p):
    b = p.shape[0]
    r = p.reshape(b, N_HEADS // 2, HEAD_DIM, 2, HEAD_DIM)
    return jnp.swapaxes(r, 2, 3).reshape(b, N_HEADS, HEAD_DIM, HEAD_DIM)


def _rearrange_proj(a):
    gates = lambda c0: jnp.repeat(a[..., c0:c0 + N_HEADS], HEAD_DIM, axis=-1)
    out = jnp.concatenate([a[..., A_QK0:A_I0], a[..., A_O0:B_Q0], gates(A_I0), gates(A_F0), a[..., B_Q0:N_IN]],
                          axis=-1)
    assert out.shape[-1] == N_CAT
    return out


def _consts(hgrn_len):
    r = jnp.arange(SCAN_ROWS)
    tri = (r[None, :] <= r[:, None]).astype(BF16)
    r = jnp.arange(hgrn_len)
    tri16 = ((r[None, :] <= r[:, None]) & (r[None, :] // SUB == r[:, None] // SUB)).astype(BF16)
    hh = jnp.arange(HB_W) // HEAD_DIM
    eblk = (hh[None, :] == hh[:, None]).astype(BF16)
    lane = jnp.arange(2 * HEAD_DIM)
    src = jnp.arange(SUB)
    esel = ((hh[None, :, None] * SUB + src[:, None, None]) == lane[None, None, :]).astype(BF16)
    erep = ((lane[None, :, None] // SUB == jnp.arange(N_HEADS)[:, None, None])
            & (lane[None, :, None] % SUB == r[None, None, :] % SUB)).astype(BF16)
    return tri, tri16, eblk, esel, erep


def _group(x, depth, layer_w, states, cache, batch, t_len, alpha, consts):
    tri, tri16, eblk, esel, erep = consts
    prompt = cache is None
    outs = []
    kv_bufs = tuple(jnp.zeros((depth, batch * t_len * N_HEADS, 2 * HEAD_DIM), F32) for _ in range(2))
    for l in range(depth):
        w = layer_w[l]
        conv0, c0, n0, m0, s0 = states(l)
        x = _ffn_ln(x, w["f_in0"], w["f_out0"], w["ln_g"][0], w["ln_b"][0], alpha)
        res = _in_proj(x, w["w_cat"], w["b_cat"], prompt, l, depth, kv_bufs)
        proj = dict(zip(_PROJ_OUTS, res[:len(_PROJ_OUTS)]))
        bkb = res[len(_PROJ_OUTS)]
        kv_bufs = (proj["bk"], proj["bv"])

        conv0_p = jnp.pad(conv0, ((0, 0), (8 - (CONV_W - 1), 0), (0, 0)))
        ya, c1, n1, m1, convn = _mlstm(proj, w["conv_w"], w["conv_b"], conv0_p, _to_head_pairs(c0),
                                       n0.reshape(batch, 1, HB_W),
                                       jnp.repeat(m0, HEAD_DIM, axis=-1).reshape(batch, 1, HB_W),
                                       tri, eblk, batch, t_len)
        lam_init = 0.8 - 0.6 * math.exp(-0.3 * l)
        if prompt:
            yb = _attn_prompt(w["lam"], proj["bq"], bkb, res[len(_PROJ_OUTS) + 1], w["dn_g"], batch, t_len,
                              1.0 - lam_init)
        else:
            yb = _attn_sample(w["lam"], proj["bq"], proj["bk"], proj["bv"], cache[0], cache[1], l, w["dn_g"],
                              batch, t_len, 1.0 - lam_init)
        yc, s1 = _hgrn(proj["c"], w["lb"], w["hn_g"], _to_head_pairs(s0), tri16, eblk, esel, erep, batch, t_len)
        x = _out_proj_ffn(ya, yb, yc, x, w["w_out"], w["ln_g"][1], w["ln_b"][1],
                          w["f_in1"], w["f_out1"], w["ln_g"][2], w["ln_b"][2], alpha)
        outs.append((_from_head_pairs(c1),
                     n1.reshape(batch, N_HEADS, HEAD_DIM),
                     m1.reshape(batch, N_HEADS, HEAD_DIM)[:, :, 0],
                     convn[:, 8 - (CONV_W - 1):, :],
                     _from_head_pairs(s1)))
    kv = tuple(buf.reshape(depth, batch, t_len, N_HEADS, 2 * HEAD_DIM) for buf in kv_bufs)
    return x, kv + tuple(jnp.stack(a) for a in zip(*outs))


def kernel(x_prompt, x_sample, cache_diff_k, cache_diff_v, state_mlstm_c, state_mlstm_n, state_mlstm_m,
           state_mlstm_conv, state_hgrn_s, w_in, b_in, w_out, mlstm_conv_w, mlstm_conv_b, diff_lambda,
           diff_norm_g, hgrn_lb_logits, hgrn_norm_g, ffn_w_in, ffn_w_out, ln_g, ln_b):
    depth = w_in.shape[0]
    alpha = (2.0 * depth) ** 0.25
    batch, seq, d_model = x_prompt.shape
    dec_batch, dec_seq, _ = x_sample.shape
    past = cache_diff_k.shape[2]

    p_lb = jax.nn.softmax(hgrn_lb_logits.astype(F32), axis=0)
    lower_bounds = jnp.cumsum(p_lb, axis=0) - p_lb[0]
    w_cat = _rearrange_proj(w_in.astype(BF16))
    b_cat = _rearrange_proj(b_in).reshape(depth, 1, N_CAT)
    lp = diff_lambda.astype(F32)
    lam_init = jnp.asarray([0.8 - 0.6 * math.exp(-0.3 * l) for l in range(depth)], F32)
    lam = (jnp.exp(jnp.sum(lp[:, 0] * lp[:, 1], axis=-1)) - jnp.exp(jnp.sum(lp[:, 2] * lp[:, 3], axis=-1))
           + lam_init)
    f_in = ffn_w_in.astype(BF16)
    f_out = ffn_w_out.astype(BF16)
    w_out_b = w_out.astype(BF16)
    layer_w = [dict(w_cat=w_cat[l], b_cat=b_cat[l],
                    conv_w=mlstm_conv_w[l], conv_b=mlstm_conv_b[l].reshape(1, 2 * HB_W),
                    lam=lam[l].reshape(1, 1), dn_g=diff_norm_g[l],
                    lb=lower_bounds[l].reshape(1, HB_W),
                    hn_g=jnp.tile(hgrn_norm_g[l], N_HEADS).reshape(1, HB_W),
                    w_out=w_out_b[l], f_in0=f_in[l, 0], f_out0=f_out[l, 0], f_in1=f_in[l, 1], f_out1=f_out[l, 1],
                    ln_g=ln_g[l], ln_b=ln_b[l]) for l in range(depth)]
    def zero_states(_):
        return (jnp.zeros((batch, CONV_W - 1, 2 * HB_W), F32), jnp.zeros((batch, N_HEADS, HEAD_DIM, HEAD_DIM), F32),
                jnp.zeros((batch, N_HEADS, HEAD_DIM), F32), jnp.zeros((batch, N_HEADS), F32),
                jnp.zeros((batch, N_HEADS, HEAD_DIM, HEAD_DIM), F32))

    def carried_states(l):
        return (state_mlstm_conv[l], state_mlstm_c[l], state_mlstm_n[l], state_mlstm_m[l], state_hgrn_s[l])

    y_p, p_out = _group(x_prompt.reshape(batch * seq, d_model), depth, layer_w, zero_states, None,
                        batch, seq, alpha, _consts(_hgrn_rows(seq)[1]))
    cache = (cache_diff_k, cache_diff_v)
    y_s, s_out = _group(x_sample.reshape(dec_batch * dec_seq, d_model), depth, layer_w, carried_states, cache,
                        dec_batch, dec_seq, alpha, _consts(_hgrn_rows(dec_seq)[1]))
    return (y_p.reshape(batch, seq, d_model), y_s.reshape(dec_batch, dec_seq, d_model)) + p_out + s_out
```

```python
import functools
import math

import jax
import jax.numpy as jnp
from jax import lax
from jax.experimental import pallas as pl
from jax.experimental.pallas import tpu as pltpu

F32 = jnp.float32
BF16 = jnp.bfloat16

HEAD_DIM = 64
N_HEADS = 4
CONV_W = 4
CONV_PAD = 8
CHUNK = 64
HB_W = N_HEADS * HEAD_DIM
ATT_W = N_HEADS * 2 * HEAD_DIM
LN_EPS = 1e-5
RMS_EPS = 1e-6
NEG_BIG = -1e30
SCAN_ROWS = 128
MLSTM_ROWS = 128
HGRN_ROWS = 256
SCAN_STREAMS = 1
SUB = 16
LOG2E = 1.4426950408889634
VT_ROWS = 2 * HEAD_DIM + 16
Q_STRIP = 256
ATT_TILE = 2048
FFN_CHUNK = 256
FFN_TILE = 1024
V7X_VMEM_LIMIT = 56 * 1024 * 1024

A_QK0 = 0
A_V0 = A_QK0 + 2 * HB_W
A_I0 = A_V0 + HB_W
A_F0 = A_I0 + N_HEADS
A_O0 = A_F0 + N_HEADS
B_Q0 = A_O0 + HB_W
B_K0 = B_Q0 + ATT_W
B_V0 = B_K0 + ATT_W
C_Q0 = B_V0 + ATT_W
N_IN = C_Q0 + 4 * HB_W

_SEGS = {}
_off = 0
for _name, _w in (("aqk", 2 * HB_W), ("av", HB_W), ("ao", HB_W), ("gi", HB_W), ("gf", HB_W),
                  ("bq", ATT_W), ("bk", ATT_W), ("bv", ATT_W), ("c", 4 * HB_W)):
    _SEGS[_name] = (_off, _w)
    _off += _w
N_CAT = _off


def _tile(n, pref, mult=8):
    t = min(pref, n)
    while t > mult and (n % t or t % mult):
        t -= mult
    assert n % t == 0 and t % mult == 0, (n, pref, mult)
    return t


def _cparams(sem):
    return pltpu.CompilerParams(dimension_semantics=sem, vmem_limit_bytes=V7X_VMEM_LIMIT)


def _const_spec(shape, single=False):
    nd = len(shape)
    if single:
        return pl.BlockSpec(shape, lambda *_: (0,) * nd, pipeline_mode=pl.Buffered(1))
    return pl.BlockSpec(shape, lambda *_: (0,) * nd)


def _dot(a, b):
    return jnp.dot(a, b, preferred_element_type=F32)


def _dot_nt(a, b):
    return lax.dot_general(a, b, (((1,), (1,)), ((), ())), preferred_element_type=F32)


def _split3(x):
    a = x.astype(BF16)
    r = x - a.astype(F32)
    b = r.astype(BF16)
    c = (r - b.astype(F32)).astype(BF16)
    return a, b, c


def _dot_exact_left(mat01, x):
    a, b, c = _split3(x)
    return _dot(mat01, a) + _dot(mat01, b) + _dot(mat01, c)


def _dot_exact_right(x, mat01):
    a, b, c = _split3(x)
    return _dot(a, mat01) + _dot(b, mat01) + _dot(c, mat01)


def _layer_norm(y, g, b):
    mu = jnp.mean(y, axis=-1, keepdims=True)
    d = y - mu
    var = jnp.mean(d * d, axis=-1, keepdims=True)
    return d * lax.rsqrt(var + LN_EPS) * g + b


def _round_robin(phased):
    last = [None] * len(phased)
    live = list(range(len(phased)))
    while live:
        for g in list(live):
            try:
                out = next(phased[g])
            except StopIteration:
                live.remove(g)
            else:
                if out is not None:
                    last[g] = out
    return last


def _pad_rows(x, rows):
    if x.shape[0] == rows:
        return x
    return jnp.concatenate([x, jnp.zeros((rows - x.shape[0],) + x.shape[1:], x.dtype)], axis=0)


def _swiglu_ln(x, wi_ref, wo_ref, g_ref, b_ref, o_ref, acc_ref, alpha, d_ff, tf):
    xb = x.astype(BF16)
    for c in range(d_ff // tf):
        gate = _dot(xb, wi_ref[:, c * tf:(c + 1) * tf])
        up = _dot(xb, wi_ref[:, d_ff + c * tf:d_ff + (c + 1) * tf])
        h = (gate * jax.nn.sigmoid(gate) * up).astype(BF16)
        part = _dot(h, wo_ref[c * tf:(c + 1) * tf, :])
        if c == 0:
            acc_ref[...] = part
        else:
            acc_ref[...] += part
    y = alpha * x + 0.5 * acc_ref[...]
    o_ref[...] = _layer_norm(y, g_ref[...], b_ref[...])


def _ffn_ln_body(x_ref, wi_ref, wo_ref, g_ref, b_ref, o_ref, acc_ref, *, alpha, d_ff, tf):
    _swiglu_ln(x_ref[...], wi_ref, wo_ref, g_ref, b_ref, o_ref, acc_ref, alpha, d_ff, tf)


def _ffn_ln(x, wi, wo, g, b, alpha):
    n, d = x.shape
    d_ff = wo.shape[0]
    tm = _tile(n, FFN_TILE)
    tf = _tile(d_ff, FFN_CHUNK, 128)
    return pl.pallas_call(
        functools.partial(_ffn_ln_body, alpha=alpha, d_ff=d_ff, tf=tf),
        grid=(n // tm,),
        in_specs=[pl.BlockSpec((tm, d), lambda i: (i, 0)),
                  _const_spec(wi.shape, single=True), _const_spec(wo.shape, single=True),
                  _const_spec((1, d)), _const_spec((1, d))],
        out_specs=pl.BlockSpec((tm, d), lambda i: (i, 0)),
        out_shape=jax.ShapeDtypeStruct((n, d), F32),
        scratch_shapes=[pltpu.VMEM((tm, d), F32)],
        compiler_params=_cparams(("parallel",)),
        name="ffn_ln",
    )(x, wi, wo, g.reshape(1, d), b.reshape(1, d))


_PROJ_OUTS = ("aqk", "av", "ao", "gi", "gf", "bq", "bk", "bv", "c")


def _in_proj_body(*refs, emit_vt, n_alias):
    x_ref, w_ref, bias_ref = refs[:3]
    out_refs = refs[3 + n_alias:]
    xb = x_ref[...].astype(BF16)
    tm = xb.shape[0]
    outs = dict(zip(_PROJ_OUTS, out_refs))
    dv = 2 * HEAD_DIM
    ones_rows = jnp.where(lax.broadcasted_iota(jnp.int32, (VT_ROWS - dv, Q_STRIP), 0) == 0, 1.0, 0.0).astype(BF16)
    for name in _PROJ_OUTS:
        c0, w = _SEGS[name]
        for s in range(0, w, 2 * HB_W):
            e = min(s + 2 * HB_W, w)
            val = _dot(xb, w_ref[:, c0 + s:c0 + e]) + bias_ref[:, c0 + s:c0 + e]
            if name in ("bk", "bv"):
                for h in range(N_HEADS):
                    outs[name][pl.ds(h, tm, stride=N_HEADS), :] = val[:, h * 2 * HEAD_DIM:(h + 1) * 2 * HEAD_DIM]
            else:
                outs[name][:, s:e] = val
            if name == "bk":
                out_refs[len(_PROJ_OUTS)][:, s:e] = val.astype(BF16)
            if name == "bv" and emit_vt:
                vt_ref = out_refs[len(_PROJ_OUTS) + 1]
                for h in range(N_HEADS):
                    vt = val[:, h * dv:(h + 1) * dv].T.astype(BF16)
                    for j in range(tm // Q_STRIP):
                        vt_ref[h, j, 0:dv, :] = vt[:, j * Q_STRIP:(j + 1) * Q_STRIP]
                        vt_ref[h, j, dv:VT_ROWS, :] = ones_rows


def _in_proj(x, w_cat, b_cat, emit_vt, layer, depth, kv_bufs):
    n, d = x.shape
    tm = _tile(n, 512, Q_STRIP if emit_vt else 8)
    out_shape, out_specs = [], []
    for k in _PROJ_OUTS:
        if k in ("bk", "bv"):
            out_shape.append(jax.ShapeDtypeStruct((depth, n * N_HEADS, 2 * HEAD_DIM), F32))
            out_specs.append(pl.BlockSpec((None, tm * N_HEADS, 2 * HEAD_DIM), lambda i: (layer, i, 0)))
        else:
            out_shape.append(jax.ShapeDtypeStruct((n, _SEGS[k][1]), F32))
            out_specs.append(pl.BlockSpec((tm, _SEGS[k][1]), lambda i: (i, 0)))
    out_shape.append(jax.ShapeDtypeStruct((n, ATT_W), BF16))
    out_specs.append(pl.BlockSpec((tm, ATT_W), lambda i: (i, 0)))
    if emit_vt:
        out_shape.append(jax.ShapeDtypeStruct((N_HEADS, n // Q_STRIP, VT_ROWS, Q_STRIP), BF16))
        out_specs.append(pl.BlockSpec((N_HEADS, tm // Q_STRIP, VT_ROWS, Q_STRIP), lambda i: (0, i, 0, 0)))
    in_specs = [pl.BlockSpec((tm, d), lambda i: (i, 0)), _const_spec(w_cat.shape), _const_spec(b_cat.shape)]
    args = [x, w_cat, b_cat]
    aliases = {}
    for buf, name in zip(kv_bufs, ("bk", "bv")):
        aliases[len(args)] = _PROJ_OUTS.index(name)
        in_specs.append(pl.BlockSpec(memory_space=pl.ANY))
        args.append(buf)
    return pl.pallas_call(
        functools.partial(_in_proj_body, emit_vt=emit_vt, n_alias=len(aliases)),
        grid=(n // tm,),
        in_specs=in_specs,
        out_specs=out_specs,
        out_shape=out_shape,
        input_output_aliases=aliases,
        compiler_params=_cparams(("parallel",)),
        name="in_proj",
    )(*args)


def _out_proj_ffn_body(ya_ref, yb_ref, yc_ref, x_ref, w_ref, g1_ref, b1_ref, wi_ref, wo_ref, g2_ref, b2_ref,
                       o_ref, acc_ref, *, alpha, d_ff, tf):
    y = _dot(ya_ref[...].astype(BF16), w_ref[0:HB_W, :])
    y += _dot(yb_ref[...].astype(BF16), w_ref[HB_W:HB_W + ATT_W, :])
    y += _dot(yc_ref[...].astype(BF16), w_ref[HB_W + ATT_W:, :])
    x = _layer_norm(alpha * x_ref[...] + y, g1_ref[...], b1_ref[...])
    _swiglu_ln(x, wi_ref, wo_ref, g2_ref, b2_ref, o_ref, acc_ref, alpha, d_ff, tf)


def _out_proj_ffn(ya, yb, yc, x, w, g1, b1, wi, wo, g2, b2, alpha):
    n, d = x.shape
    d_ff = wo.shape[0]
    tm = _tile(n, FFN_TILE)
    tf = _tile(d_ff, FFN_CHUNK, 128)
    row = lambda width: pl.BlockSpec((tm, width), lambda i: (i, 0))
    vec = _const_spec((1, d))
    return pl.pallas_call(
        functools.partial(_out_proj_ffn_body, alpha=alpha, d_ff=d_ff, tf=tf),
        grid=(n // tm,),
        in_specs=[row(HB_W), row(ATT_W), row(HB_W), row(d), _const_spec(w.shape, single=True), vec, vec,
                  _const_spec(wi.shape, single=True), _const_spec(wo.shape, single=True), vec, vec],
        out_specs=row(d),
        out_shape=jax.ShapeDtypeStruct((n, d), F32),
        scratch_shapes=[pltpu.VMEM((tm, d), F32)],
        compiler_params=_cparams(("parallel",)),
        name="out_proj_ffn",
    )(ya, yb, yc, x, w, g1.reshape(1, d), b1.reshape(1, d), wi, wo, g2.reshape(1, d), b2.reshape(1, d))


def _mlstm_body(qk_ref, v_ref, o_ref, gi_ref, gf_ref, cw_ref, cb_ref, conv0_ref, c0_ref, n0_ref, m0_ref,
                tri_ref, eblk_ref,
                ya_ref, c1_ref, n1_ref, m1_ref, convn_ref,
                up_ref, c_ref, n_ref, m_ref, *, rows, streams):
    step = pl.program_id(1)
    low = lax.broadcasted_iota(jnp.int32, (1, 2 * HEAD_DIM), 1) < HEAD_DIM
    pw = 2 * HEAD_DIM

    @pl.when(step == 0)
    def _():
        for g in range(streams):
            up_ref[g, 0:CONV_PAD, :] = conv0_ref[g]
            c_ref[g] = jnp.zeros(c_ref.shape[1:], F32)
            for i in range(N_HEADS // 2):
                pair = c0_ref[g, i]
                c_ref[g, i * pw:i * pw + HEAD_DIM, i * pw:(i + 1) * pw] = jnp.where(low, pair, 0.0)
                c_ref[g, i * pw + HEAD_DIM:(i + 1) * pw, i * pw:(i + 1) * pw] = jnp.where(low, 0.0, pair)
        n_ref[...] = n0_ref[...]
        m_ref[...] = m0_ref[...]

    new_states = _round_robin([_mlstm_stream(
        qk_ref.at[g], v_ref.at[g], o_ref.at[g], gi_ref.at[g], gf_ref.at[g], cw_ref, cb_ref, tri_ref, eblk_ref,
        ya_ref.at[g], convn_ref.at[g], up_ref.at[g], c_ref.at[g], n_ref.at[g], m_ref.at[g], rows=rows)
        for g in range(streams)])

    @pl.when(step == pl.num_programs(1) - 1)
    def _():
        for g, (c_new, n_new, m_new) in enumerate(new_states):
            for i in range(N_HEADS // 2):
                c1_ref[g, i] = jnp.where(low, c_new[i * pw:i * pw + HEAD_DIM, i * pw:(i + 1) * pw],
                                         c_new[i * pw + HEAD_DIM:(i + 1) * pw, i * pw:(i + 1) * pw])
            n1_ref[g] = n_new
            m1_ref[g] = m_new


def _mlstm_stream(qk_ref, v_ref, o_ref, gi_ref, gf_ref, cw_ref, cb_ref, tri_ref, eblk_ref,
                  ya_ref, convn_ref, up_ref, c_ref, n_ref, m_ref, *, rows):
    L = tri_ref.shape[0]

    gi = _pad_rows(gi_ref[...], L)
    flog = jax.nn.log_sigmoid(_pad_rows(gf_ref[...], L))
    if rows < L:
        live = lax.broadcasted_iota(jnp.int32, (L, 1), 0) < rows
        gi = jnp.where(live, gi, NEG_BIG)
        flog = jnp.where(live, flog, 0.0)
    b = _dot_exact_left(tri_ref[...], flog)
    b_t = b.T
    gi_t = gi.T
    yield

    u = _pad_rows(qk_ref[...], L)
    up_ref[CONV_PAD:CONV_PAD + L, :] = u
    y = cb_ref[...]
    first = CONV_PAD - (CONV_W - 1)
    for j in range(CONV_W):
        y = y + cw_ref[j:j + 1, :] * up_ref[first + j:first + j + L, :]
    convn_ref[...] = up_ref[rows:rows + CONV_PAD, :]
    up_ref[0:CONV_PAD, :] = up_ref[L:L + CONV_PAD, :]
    qk = y * jax.nn.sigmoid(y)
    q = qk[:, :HB_W] * (HEAD_DIM ** -0.5)
    k = qk[:, HB_W:]
    v = _pad_rows(v_ref[...], L)
    kb = k.astype(BF16)
    vb = v.astype(BF16)
    qb = q.astype(BF16)

    lane_head = lax.broadcasted_iota(jnp.int32, (1, HB_W), 1) // HEAD_DIM
    qk_scores = [_dot_nt(jnp.where(lane_head == h, q, 0.0).astype(BF16), kb) for h in range(N_HEADS)]
    c_prev = c_ref[...]
    n_prev = n_ref[...]
    eblk = eblk_ref[...]
    num_state = _dot(qb, c_prev.astype(BF16))
    den_state = _dot((q * n_prev).astype(BF16), eblk)
    yield

    m_prev = m_ref[...]
    g = b + m_prev

    t_idx = lax.broadcasted_iota(jnp.int32, (L, L), 0)
    s_idx = lax.broadcasted_iota(jnp.int32, (L, L), 1)
    causal = s_idx <= t_idx

    mrow_hb = jnp.zeros((L, HB_W), F32)
    wg_hb = jnp.zeros((L, HB_W), F32)
    den_hb = jnp.zeros((L, HB_W), F32)
    num_hb = jnp.zeros((L, HB_W), F32)
    for h in range(N_HEADS):
        sel = lane_head == h
        c0 = h * HEAD_DIM
        dmat = b[:, c0:c0 + 1] - b_t[c0:c0 + 1, :] + gi_t[c0:c0 + 1, :]
        dmat = jnp.where(causal, dmat, NEG_BIG)
        gcol = g[:, c0:c0 + 1]
        mrow = jnp.maximum(gcol, jnp.max(dmat, axis=1, keepdims=True))
        wd = jnp.exp(dmat - mrow)
        wg = jnp.exp(gcol - mrow)
        qkw = qk_scores[h] * wd
        den = jnp.sum(qkw, axis=1, keepdims=True)
        num = _dot(qkw.astype(BF16), vb)
        mrow_hb = jnp.where(sel, mrow, mrow_hb)
        wg_hb = jnp.where(sel, wg, wg_hb)
        den_hb = jnp.where(sel, den, den_hb)
        num_hb = jnp.where(sel, num, num_hb)

    num_hb = wg_hb * num_state + num_hb
    den_hb = wg_hb * den_state + den_hb
    hout = num_hb / jnp.maximum(jnp.abs(den_hb), jnp.exp(-mrow_hb))
    ya = jax.nn.sigmoid(_pad_rows(o_ref[...], L)) * hout
    ya_ref[...] = ya[:rows]
    yield

    m_new = mrow_hb[L - 1:L, :]
    b_last = b[L - 1:L, :]
    decay = jnp.exp(b_last + m_prev - m_new)
    ws = jnp.exp(b_last - b + gi - m_new)
    kw = k * ws
    upd = _dot(kw.T.astype(BF16), vb) * eblk.astype(F32)
    c_new = decay * c_prev + upd
    n_new = decay * n_prev + jnp.sum(kw, axis=0, keepdims=True)
    c_ref[...] = c_new
    n_ref[...] = n_new
    m_ref[...] = m_new
    yield c_new, n_new, m_new


def _mlstm_rows(t_len):
    rows = _tile(t_len, MLSTM_ROWS, 8)
    return rows, max(rows, SCAN_ROWS)


def _mlstm(proj, conv_w, conv_b, conv0, c0_pk, n0, m0_hb, tri, eblk, batch, t_len):
    rows, length = _mlstm_rows(t_len)
    assert tri.shape == (length, length)
    steps = t_len // rows
    g = SCAN_STREAMS
    tok = lambda width: pl.BlockSpec((g, rows, width), lambda bi, ci: (bi, ci, 0))
    per_b = lambda shape: pl.BlockSpec((g,) + shape, lambda bi, ci: (bi,) + (0,) * len(shape))
    seq = lambda a: a.reshape(batch, t_len, a.shape[-1])
    outs = pl.pallas_call(
        functools.partial(_mlstm_body, rows=rows, streams=g),
        grid=(batch // g, steps),
        in_specs=[tok(2 * HB_W), tok(HB_W), tok(HB_W), tok(HB_W), tok(HB_W),
                  _const_spec(conv_w.shape), _const_spec(conv_b.shape),
                  per_b((CONV_PAD, 2 * HB_W)), per_b((N_HEADS // 2, HEAD_DIM, 2 * HEAD_DIM)), per_b((1, HB_W)),
                  per_b((1, HB_W)), _const_spec(tri.shape), _const_spec(eblk.shape)],
        out_specs=[tok(HB_W), per_b((N_HEADS // 2, HEAD_DIM, 2 * HEAD_DIM)), per_b((1, HB_W)), per_b((1, HB_W)),
                   per_b((CONV_PAD, 2 * HB_W))],
        out_shape=[jax.ShapeDtypeStruct((batch, t_len, HB_W), F32),
                   jax.ShapeDtypeStruct((batch, N_HEADS // 2, HEAD_DIM, 2 * HEAD_DIM), F32),
                   jax.ShapeDtypeStruct((batch, 1, HB_W), F32),
                   jax.ShapeDtypeStruct((batch, 1, HB_W), F32),
                   jax.ShapeDtypeStruct((batch, CONV_PAD, 2 * HB_W), F32)],
        scratch_shapes=[pltpu.VMEM((g, length + CONV_PAD, 2 * HB_W), F32),
                        pltpu.VMEM((g, HB_W, HB_W), F32),
                        pltpu.VMEM((g, 1, HB_W), F32),
                        pltpu.VMEM((g, 1, HB_W), F32)],
        compiler_params=_cparams(("parallel", "arbitrary")),
        name="mlstm",
    )(seq(proj["aqk"]), seq(proj["av"]), seq(proj["ao"]), seq(proj["gi"]), seq(proj["gf"]), conv_w, conv_b,
      conv0, c0_pk, n0, m0_hb, tri, eblk)
    return (outs[0].reshape(batch * t_len, HB_W),) + tuple(outs[1:])


def _hgrn_body(c_ref, lb_ref, hng_ref, s0_ref, tri16_ref, eblk_ref, esel_ref, erep_ref, yc_ref, s1_ref, s_ref,
               *, rows, length):
    step = pl.program_id(1)
    L = length

    @pl.when(step == 0)
    def _():
        s_ref[...] = s0_ref[...]

    cin = _pad_rows(c_ref[...], L)
    q = cin[:, 0:HB_W]
    f_pre = cin[:, HB_W:2 * HB_W]
    v = cin[:, 2 * HB_W:3 * HB_W]
    gate = cin[:, 3 * HB_W:]
    lb = lb_ref[...]
    forget = lb + (1.0 - lb) * jax.nn.sigmoid(f_pre)
    logf = jnp.log(forget)
    key = 1.0 - forget
    if rows < L:
        live = lax.broadcasted_iota(jnp.int32, (L, 1), 0) < rows
        logf = jnp.where(live, logf, 0.0)
        key = jnp.where(live, key, 0.0)
    bl = _dot_exact_left(tri16_ref[...], logf) * LOG2E
    eblk = eblk_ref[...]
    groups = L // SUB
    t_loc = lax.broadcasted_iota(jnp.int32, (L, 1), 0) % SUB
    vb = v.astype(BF16)

    def from_source(x, s):
        picked = x.reshape(groups, SUB, HB_W)[:, s:s + 1, :]
        return jnp.broadcast_to(picked, (groups, SUB, HB_W)).reshape(L, HB_W)

    n_sub = max(rows // SUB, 1)
    qd = (q * jnp.exp2(bl)).astype(BF16)
    bl_t = bl.T
    low = lax.broadcasted_iota(jnp.int32, (1, 2 * HEAD_DIM), 1) < HEAD_DIM
    zero_b = jnp.zeros((HEAD_DIM, 2 * HEAD_DIM), BF16)

    def contribution(j):
        r0 = j * SUB
        b_last = bl[r0 + SUB - 1:r0 + SUB, :]
        kw_j = (key[r0:r0 + SUB] * jnp.exp2(b_last - bl[r0:r0 + SUB])).astype(BF16)
        upd = lax.dot_general(kw_j, vb[r0:r0 + SUB], (((0,), (0,)), ((), ())),
                              preferred_element_type=F32)
        dec = jnp.exp2(bl_t[:, r0 + SUB - 1:r0 + SUB])
        pair = []
        for i in range(2):
            rk = 2 * i * HEAD_DIM
            u = jnp.where(low, upd[rk:rk + HEAD_DIM, rk:rk + 2 * HEAD_DIM],
                          upd[rk + HEAD_DIM:rk + 2 * HEAD_DIM, rk:rk + 2 * HEAD_DIM])
            d = jnp.where(low, dec[rk:rk + HEAD_DIM], dec[rk + HEAD_DIM:rk + 2 * HEAD_DIM])
            pair.append((d, u))
        return pair

    pk = [s_ref[0], s_ref[1]]
    o_parts = []

    def advance(j):
        r0 = j * SUB
        pb = [x.astype(BF16) for x in pk]
        s_bd = jnp.concatenate([
            jnp.concatenate([jnp.where(low, pb[0], 0.0).astype(BF16), zero_b], axis=1),
            jnp.concatenate([jnp.where(low, 0.0, pb[0]).astype(BF16), zero_b], axis=1),
            jnp.concatenate([zero_b, jnp.where(low, pb[1], 0.0).astype(BF16)], axis=1),
            jnp.concatenate([zero_b, jnp.where(low, 0.0, pb[1]).astype(BF16)], axis=1)], axis=0)
        o_parts.append(_dot(qd[r0:r0 + SUB], s_bd))
        pk[:] = [d * p + u for p, (d, u) in zip(pk, contrib[j])]

    contrib = []
    att = jnp.zeros((L, 2 * HEAD_DIM), F32)
    for s in range(SUB):
        w = jnp.exp2(bl - from_source(bl, s))
        p = jnp.where(t_loc >= s, q * from_source(key, s) * w, 0.0)
        att = att + _dot(p.astype(BF16), esel_ref[s])
        while len(contrib) < ((s + 1) * n_sub) // SUB:
            contrib.append(contribution(len(contrib)))
    for j in range(n_sub):
        advance(j)
    s_ref[0] = pk[0]
    s_ref[1] = pk[1]
    attb = att.astype(BF16)
    same_sub = (lax.broadcasted_iota(jnp.int32, (L, L), 0) // SUB
                == lax.broadcasted_iota(jnp.int32, (L, L), 1) // SUB)
    lane_head = lax.broadcasted_iota(jnp.int32, (1, HB_W), 1) // HEAD_DIM
    o = jnp.zeros((L, HB_W), F32)
    for h in range(N_HEADS):
        full = jnp.where(same_sub, _dot(attb, erep_ref[h]), 0.0)
        o = jnp.where(lane_head == h, _dot(full.astype(BF16), vb), o)

    o_inter = jnp.concatenate(o_parts, axis=0)
    if rows < L:
        o_inter = _pad_rows(o_inter, L)
    o = o + o_inter

    ms = _dot_exact_right(o * o, eblk) * (1.0 / HEAD_DIM)
    yc = o * lax.rsqrt(ms + RMS_EPS) * hng_ref[...] * (gate * jax.nn.sigmoid(gate))
    yc_ref[...] = yc[:rows]

    @pl.when(step == pl.num_programs(1) - 1)
    def _():
        s1_ref[0] = pk[0]
        s1_ref[1] = pk[1]


def _hgrn_rows(t_len):
    rows = _tile(t_len, HGRN_ROWS, SUB)
    return rows, max(rows, SCAN_ROWS)


def _hgrn(c_all, lb, hn_g, s0_pk, tri16, eblk, esel, erep, batch, t_len):
    rows, length = _hgrn_rows(t_len)
    steps = t_len // rows
    n = batch * t_len
    pk_spec = pl.BlockSpec((None, 2, HEAD_DIM, 2 * HEAD_DIM), lambda bi, ci: (bi, 0, 0, 0))
    return pl.pallas_call(
        functools.partial(_hgrn_body, rows=rows, length=length),
        grid=(batch, steps),
        in_specs=[pl.BlockSpec((rows, 4 * HB_W), lambda bi, ci: (bi * steps + ci, 0)),
                  _const_spec((1, HB_W)), _const_spec((1, HB_W)), pk_spec,
                  _const_spec(tri16.shape), _const_spec(eblk.shape), _const_spec(esel.shape),
                  _const_spec(erep.shape)],
        out_specs=[pl.BlockSpec((rows, HB_W), lambda bi, ci: (bi * steps + ci, 0)), pk_spec],
        out_shape=[jax.ShapeDtypeStruct((n, HB_W), F32),
                   jax.ShapeDtypeStruct((batch, 2, HEAD_DIM, 2 * HEAD_DIM), F32)],
        scratch_shapes=[pltpu.VMEM((2, HEAD_DIM, 2 * HEAD_DIM), F32)],
        compiler_params=_cparams(("parallel", "arbitrary")),
        name="hgrn",
    )(c_all, lb, hn_g, s0_pk, tri16, eblk, esel, erep)


def _attn_prompt_body(qi_tab, ki_tab, lam_ref, q_ref, k_ref, vt_ref, g_ref, o_ref, qx_ref, m_ref, acc_ref,
                      s_ref, bmax_ref, *, tq, out_scale):
    pair = pl.program_id(2)
    qi = qi_tab[pair]
    ki = ki_tab[pair]
    ns = tq // Q_STRIP
    dv = 2 * HEAD_DIM

    @pl.when(ki == 0)
    def _():
        qt = (q_ref[...] * (HEAD_DIM ** -0.5 * LOG2E)).T
        first = lax.broadcasted_iota(jnp.int32, (dv, 1), 0) < HEAD_DIM
        qa = jnp.where(first, qt, 0.0).astype(BF16)
        qb = jnp.where(first, 0.0, qt).astype(BF16)
        for st in range(ns):
            qx_ref[0, st] = qa[:, st * Q_STRIP:(st + 1) * Q_STRIP]
            qx_ref[1, st] = qb[:, st * Q_STRIP:(st + 1) * Q_STRIP]
        m_ref[...] = jnp.full(m_ref.shape, NEG_BIG, F32)
        acc_ref[...] = jnp.zeros(acc_ref.shape, F32)

    def scores(kb, st, diag):
        k = k_ref[kb * Q_STRIP:(kb + 1) * Q_STRIP, :]
        for mp in range(2):
            s = _dot(k, qx_ref[mp, st])
            if diag and st == kb:
                k_chunk = lax.broadcasted_iota(jnp.int32, (Q_STRIP, 1), 0) // CHUNK
                q_chunk = lax.broadcasted_iota(jnp.int32, (1, Q_STRIP), 1) // CHUNK
                s = jnp.where(k_chunk <= q_chunk, s, NEG_BIG)
            s_ref[kb % 2, mp, st] = s
            bmax_ref[kb % 2, mp, st] = jnp.max(s, axis=0, keepdims=True)

    def accumulate(kb, st):
        vt = vt_ref[kb]
        for mp in range(2):
            m_old = m_ref[mp, st]
            m_new = jnp.maximum(m_old, bmax_ref[kb % 2, mp, st])
            p = jnp.exp2(s_ref[kb % 2, mp, st] - m_new).astype(BF16)
            corr = jnp.exp2(m_old - m_new)
            acc_ref[mp, st] = corr * acc_ref[mp, st] + _dot(vt, p)
            m_ref[mp, st] = m_new

    def key_tile(diag):
        first_strip = (lambda kb: kb) if diag else (lambda kb: 0)
        for st in range(ns):
            scores(0, st, diag)
        for kb in range(ns):
            for st in range(first_strip(kb), ns):
                accumulate(kb, st)
                if kb + 1 < ns and st >= first_strip(kb + 1):
                    scores(kb + 1, st, diag)

    @pl.when(ki < qi)
    def _():
        key_tile(False)

    @pl.when(ki == qi)
    def _():
        key_tile(True)
        lam = lam_ref[0, 0]
        for st in range(ns):
            a1 = acc_ref[0, st]
            a2 = acc_ref[1, st]
            o = a1[:dv] / a1[dv:dv + 1] - lam * (a2[:dv] / a2[dv:dv + 1])
            ms = jnp.mean(o * o, axis=0, keepdims=True)
            o = o * lax.rsqrt(ms + RMS_EPS) * g_ref[...] * out_scale
            o_ref[st * Q_STRIP:(st + 1) * Q_STRIP, :] = o.T


def _attn_prompt(lam, bq, bkb, vt, dn_g, batch, t_len, out_scale):
    tq = _tile(t_len, ATT_TILE, Q_STRIP)
    nq = t_len // tq
    ns = tq // Q_STRIP
    n = batch * t_len
    w = 2 * HEAD_DIM
    pairs = [(qi, ki) for qi in range(nq) for ki in range(qi + 1)]
    qi_tab = jnp.asarray([p[0] for p in pairs], jnp.int32)
    ki_tab = jnp.asarray([p[1] for p in pairs], jnp.int32)
    qmap = lambda b, h, p, qt, kt: (b * nq + qt[p], h)
    kmap = lambda b, h, p, qt, kt: (b * nq + kt[p], h)
    vmap_ = lambda b, h, p, qt, kt: (h, b * nq + kt[p], 0, 0)
    grid_spec = pltpu.PrefetchScalarGridSpec(
        num_scalar_prefetch=2,
        grid=(batch, N_HEADS, len(pairs)),
        in_specs=[pl.BlockSpec(memory_space=pltpu.SMEM),
                  pl.BlockSpec((tq, w), qmap), pl.BlockSpec((tq, w), kmap),
                  pl.BlockSpec((None, ns, VT_ROWS, Q_STRIP), vmap_),
                  pl.BlockSpec((w, 1), lambda b, h, p, qt, kt: (0, 0))],
        out_specs=pl.BlockSpec((tq, w), qmap),
        scratch_shapes=[pltpu.VMEM((2, ns, w, Q_STRIP), BF16),
                        pltpu.VMEM((2, ns, 1, Q_STRIP), F32),
                        pltpu.VMEM((2, ns, VT_ROWS, Q_STRIP), F32),
                        pltpu.VMEM((2, 2, ns, Q_STRIP, Q_STRIP), F32),
                        pltpu.VMEM((2, 2, ns, 1, Q_STRIP), F32)])
    return pl.pallas_call(
        functools.partial(_attn_prompt_body, tq=tq, out_scale=out_scale),
        grid_spec=grid_spec,
        out_shape=jax.ShapeDtypeStruct((n, ATT_W), F32),
        compiler_params=_cparams(("parallel", "parallel", "arbitrary")),
        name="attn_prompt",
    )(qi_tab, ki_tab, lam, bq, bkb, vt, dn_g.reshape(w, 1))


def _attn_sample_body(lam_ref, q_ref, kn_ref, vn_ref, kp_ref, vp_ref, g_ref, o_ref, *, t_new, past, out_scale):
    w = 2 * HEAD_DIM
    first = lax.broadcasted_iota(jnp.int32, (1, w), 1) < HEAD_DIM
    for h in range(N_HEADS):
        cols = slice(h * w, (h + 1) * w)
        q = q_ref[:, cols] * (HEAD_DIM ** -0.5)
        q2 = jnp.concatenate([jnp.where(first, q, 0.0), jnp.where(first, 0.0, q)], axis=0).astype(BF16)
        kp = kp_ref[pl.ds(h, past, stride=N_HEADS), :].astype(BF16)
        vp = vp_ref[pl.ds(h, past, stride=N_HEADS), :].astype(BF16)
        s_p = _dot_nt(q2, kp)
        kn = kn_ref[pl.ds(h, t_new, stride=N_HEADS), :].astype(BF16)
        vn = vn_ref[pl.ds(h, t_new, stride=N_HEADS), :].astype(BF16)
        s_n = _dot_nt(q2, kn)
        m = jnp.maximum(jnp.max(s_p, axis=1, keepdims=True), jnp.max(s_n, axis=1, keepdims=True))
        p_p = jnp.exp(s_p - m)
        p_n = jnp.exp(s_n - m)
        l = jnp.sum(p_p, axis=1, keepdims=True) + jnp.sum(p_n, axis=1, keepdims=True)
        acc = _dot(p_p.astype(BF16), vp) + _dot(p_n.astype(BF16), vn)
        o2 = acc / l
        o = o2[:t_new] - lam_ref[0, 0] * o2[t_new:]
        ms = jnp.mean(o * o, axis=1, keepdims=True)
        o_ref[:, cols] = o * lax.rsqrt(ms + RMS_EPS) * g_ref[...] * out_scale


def _attn_sample(lam, bq, bk, bv, cache_k, cache_v, layer, dn_g, batch, t_new, out_scale):
    w = 2 * HEAD_DIM
    depth, _, past = cache_k.shape[:3]
    rows = past * N_HEADS
    new = pl.BlockSpec((t_new, ATT_W), lambda b: (b, 0))
    new_kv = pl.BlockSpec((None, t_new * N_HEADS, w), lambda b: (layer, b, 0))
    old = pl.BlockSpec((None, None, rows, w), lambda b: (layer, b, 0, 0))
    return pl.pallas_call(
        functools.partial(_attn_sample_body, t_new=t_new, past=past, out_scale=out_scale),
        grid=(batch,),
        in_specs=[pl.BlockSpec(memory_space=pltpu.SMEM), new, new_kv, new_kv, old, old, _const_spec((1, w))],
        out_specs=new,
        out_shape=jax.ShapeDtypeStruct((batch * t_new, ATT_W), F32),
        compiler_params=_cparams(("parallel",)),
        name="attn_sample",
    )(lam, bq, bk, bv, cache_k.reshape(depth, batch, rows, w), cache_v.reshape(depth, batch, rows, w),
      dn_g.reshape(1, w))


def _to_head_pairs(s):
    b = s.shape[0]
    r = s.reshape(b, N_HEADS // 2, 2, HEAD_DIM, HEAD_DIM)
    return jnp.swapaxes(r, 2, 3).reshape(b, N_HEADS // 2, HEAD_DIM, 2 * HEAD_DIM)


def _from_head_pairs(p):
    b = p.shape[0]
    r = p.reshape(b, N_HEADS // 2, HEAD_DIM, 2, HEAD_DIM)
    return jnp.swapaxes(r, 2, 3).reshape(b, N_HEADS, HEAD_DIM, HEAD_DIM)


def _rearrange_proj(a):
    gates = lambda c0: jnp.repeat(a[..., c0:c0 + N_HEADS], HEAD_DIM, axis=-1)
    out = jnp.concatenate([a[..., A_QK0:A_I0], a[..., A_O0:B_Q0], gates(A_I0), gates(A_F0), a[..., B_Q0:N_IN]],
                          axis=-1)
    assert out.shape[-1] == N_CAT
    return out


def _consts(hgrn_len, mlstm_len):
    r = jnp.arange(mlstm_len)
    tri = (r[None, :] <= r[:, None]).astype(BF16)
    r = jnp.arange(hgrn_len)
    tri16 = ((r[None, :] <= r[:, None]) & (r[None, :] // SUB == r[:, None] // SUB)).astype(BF16)
    hh = jnp.arange(HB_W) // HEAD_DIM
    eblk = (hh[None, :] == hh[:, None]).astype(BF16)
    lane = jnp.arange(2 * HEAD_DIM)
    src = jnp.arange(SUB)
    esel = ((hh[None, :, None] * SUB + src[:, None, None]) == lane[None, None, :]).astype(BF16)
    erep = ((lane[None, :, None] // SUB == jnp.arange(N_HEADS)[:, None, None])
            & (lane[None, :, None] % SUB == r[None, None, :] % SUB)).astype(BF16)
    return tri, tri16, eblk, esel, erep


def _group(x, depth, layer_w, states, cache, batch, t_len, alpha, consts):
    tri, tri16, eblk, esel, erep = consts
    prompt = cache is None
    outs = []
    kv_bufs = tuple(jnp.zeros((depth, batch * t_len * N_HEADS, 2 * HEAD_DIM), F32) for _ in range(2))
    for l in range(depth):
        w = layer_w[l]
        conv0, c0, n0, m0, s0 = states(l)
        x = _ffn_ln(x, w["f_in0"], w["f_out0"], w["ln_g"][0], w["ln_b"][0], alpha)
        res = _in_proj(x, w["w_cat"], w["b_cat"], prompt, l, depth, kv_bufs)
        proj = dict(zip(_PROJ_OUTS, res[:len(_PROJ_OUTS)]))
        bkb = res[len(_PROJ_OUTS)]
        kv_bufs = (proj["bk"], proj["bv"])

        conv0_p = jnp.pad(conv0, ((0, 0), (CONV_PAD - (CONV_W - 1), 0), (0, 0)))
        ya, c1, n1, m1, convn = _mlstm(proj, w["conv_w"], w["conv_b"], conv0_p, _to_head_pairs(c0),
                                       n0.reshape(batch, 1, HB_W),
                                       jnp.repeat(m0, HEAD_DIM, axis=-1).reshape(batch, 1, HB_W),
                                       tri, eblk, batch, t_len)
        lam_init = 0.8 - 0.6 * math.exp(-0.3 * l)
        if prompt:
            yb = _attn_prompt(w["lam"], proj["bq"], bkb, res[len(_PROJ_OUTS) + 1], w["dn_g"], batch, t_len,
                              1.0 - lam_init)
        else:
            yb = _attn_sample(w["lam"], proj["bq"], proj["bk"], proj["bv"], cache[0], cache[1], l, w["dn_g"],
                              batch, t_len, 1.0 - lam_init)
        yc, s1 = _hgrn(proj["c"], w["lb"], w["hn_g"], _to_head_pairs(s0), tri16, eblk, esel, erep, batch, t_len)
        x = _out_proj_ffn(ya, yb, yc, x, w["w_out"], w["ln_g"][1], w["ln_b"][1],
                          w["f_in1"], w["f_out1"], w["ln_g"][2], w["ln_b"][2], alpha)
        outs.append((_from_head_pairs(c1),
                     n1.reshape(batch, N_HEADS, HEAD_DIM),
                     m1.reshape(batch, N_HEADS, HEAD_DIM)[:, :, 0],
                     convn[:, CONV_PAD - (CONV_W - 1):, :],
                     _from_head_pairs(s1)))
    kv = tuple(buf.reshape(depth, batch, t_len, N_HEADS, 2 * HEAD_DIM) for buf in kv_bufs)
    return x, kv + tuple(jnp.stack(a) for a in zip(*outs))


def kernel(x_prompt, x_sample, cache_diff_k, cache_diff_v, state_mlstm_c, state_mlstm_n, state_mlstm_m,
           state_mlstm_conv, state_hgrn_s, w_in, b_in, w_out, mlstm_conv_w, mlstm_conv_b, diff_lambda,
           diff_norm_g, hgrn_lb_logits, hgrn_norm_g, ffn_w_in, ffn_w_out, ln_g, ln_b):
    depth = w_in.shape[0]
    alpha = (2.0 * depth) ** 0.25
    batch, seq, d_model = x_prompt.shape
    dec_batch, dec_seq, _ = x_sample.shape
    past = cache_diff_k.shape[2]

    p_lb = jax.nn.softmax(hgrn_lb_logits.astype(F32), axis=0)
    lower_bounds = jnp.cumsum(p_lb, axis=0) - p_lb[0]
    w_cat = _rearrange_proj(w_in.astype(BF16))
    b_cat = _rearrange_proj(b_in).reshape(depth, 1, N_CAT)
    lp = diff_lambda.astype(F32)
    lam_init = jnp.asarray([0.8 - 0.6 * math.exp(-0.3 * l) for l in range(depth)], F32)
    lam = (jnp.exp(jnp.sum(lp[:, 0] * lp[:, 1], axis=-1)) - jnp.exp(jnp.sum(lp[:, 2] * lp[:, 3], axis=-1))
           + lam_init)
    f_in = ffn_w_in.astype(BF16)
    f_out = ffn_w_out.astype(BF16)
    w_out_b = w_out.astype(BF16)
    layer_w = [dict(w_cat=w_cat[l], b_cat=b_cat[l],
                    conv_w=mlstm_conv_w[l], conv_b=mlstm_conv_b[l].reshape(1, 2 * HB_W),
                    lam=lam[l].reshape(1, 1), dn_g=diff_norm_g[l],
                    lb=lower_bounds[l].reshape(1, HB_W),
                    hn_g=jnp.tile(hgrn_norm_g[l], N_HEADS).reshape(1, HB_W),
                    w_out=w_out_b[l], f_in0=f_in[l, 0], f_out0=f_out[l, 0], f_in1=f_in[l, 1], f_out1=f_out[l, 1],
                    ln_g=ln_g[l], ln_b=ln_b[l]) for l in range(depth)]
    def zero_states(_):
        return (jnp.zeros((batch, CONV_W - 1, 2 * HB_W), F32), jnp.zeros((batch, N_HEADS, HEAD_DIM, HEAD_DIM), F32),
                jnp.zeros((batch, N_HEADS, HEAD_DIM), F32), jnp.zeros((batch, N_HEADS), F32),
                jnp.zeros((batch, N_HEADS, HEAD_DIM, HEAD_DIM), F32))

    def carried_states(l):
        return (state_mlstm_conv[l], state_mlstm_c[l], state_mlstm_n[l], state_mlstm_m[l], state_hgrn_s[l])

    y_p, p_out = _group(x_prompt.reshape(batch * seq, d_model), depth, layer_w, zero_states, None,
                        batch, seq, alpha, _consts(_hgrn_rows(seq)[1], _mlstm_rows(seq)[1]))
    cache = (cache_diff_k, cache_diff_v)
    y_s, s_out = _group(x_sample.reshape(dec_batch * dec_seq, d_model), depth, layer_w, carried_states, cache,
                        dec_batch, dec_seq, alpha, _consts(_hgrn_rows(dec_seq)[1], _mlstm_rows(dec_seq)[1]))
    return (y_p.reshape(batch, seq, d_model), y_s.reshape(dec_batch, dec_seq, d_model)) + p_out + s_out
```

```python
import functools
import math

import jax
import jax.numpy as jnp
from jax import lax
from jax.experimental import pallas as pl
from jax.experimental.pallas import tpu as pltpu

F32 = jnp.float32
BF16 = jnp.bfloat16

HEAD_DIM = 64
N_HEADS = 4
CONV_W = 4
CONV_PAD = 8
CHUNK = 64
HB_W = N_HEADS * HEAD_DIM
ATT_W = N_HEADS * 2 * HEAD_DIM
LN_EPS = 1e-5
RMS_EPS = 1e-6
NEG_BIG = -1e30
SCAN_ROWS = 128
MLSTM_ROWS = 128
HGRN_ROWS = 256
SCAN_STREAMS = 1
SUB = 16
LOG2E = 1.4426950408889634
VT_ROWS = 2 * HEAD_DIM + 16
Q_STRIP = 256
ATT_TILE = 2048
FFN_CHUNK = 256
FFN_TILE = 1024
V7X_VMEM_LIMIT = 56 * 1024 * 1024

A_QK0 = 0
A_V0 = A_QK0 + 2 * HB_W
A_I0 = A_V0 + HB_W
A_F0 = A_I0 + N_HEADS
A_O0 = A_F0 + N_HEADS
B_Q0 = A_O0 + HB_W
B_K0 = B_Q0 + ATT_W
B_V0 = B_K0 + ATT_W
C_Q0 = B_V0 + ATT_W
N_IN = C_Q0 + 4 * HB_W

_SEGS = {}
_off = 0
for _name, _w in (("aqk", 2 * HB_W), ("av", HB_W), ("ao", HB_W), ("gi", HB_W), ("gf", HB_W),
                  ("bq", ATT_W), ("bk", ATT_W), ("bv", ATT_W), ("c", 4 * HB_W)):
    _SEGS[_name] = (_off, _w)
    _off += _w
N_CAT = _off


def _tile(n, pref, mult=8):
    t = min(pref, n)
    while t > mult and (n % t or t % mult):
        t -= mult
    assert n % t == 0 and t % mult == 0, (n, pref, mult)
    return t


def _cparams(sem):
    return pltpu.CompilerParams(dimension_semantics=sem, vmem_limit_bytes=V7X_VMEM_LIMIT)


def _const_spec(shape, single=False):
    nd = len(shape)
    if single:
        return pl.BlockSpec(shape, lambda *_: (0,) * nd, pipeline_mode=pl.Buffered(1))
    return pl.BlockSpec(shape, lambda *_: (0,) * nd)


def _dot(a, b):
    return jnp.dot(a, b, preferred_element_type=F32)


def _dot_nt(a, b):
    return lax.dot_general(a, b, (((1,), (1,)), ((), ())), preferred_element_type=F32)


def _split3(x):
    a = x.astype(BF16)
    r = x - a.astype(F32)
    b = r.astype(BF16)
    c = (r - b.astype(F32)).astype(BF16)
    return a, b, c


def _dot_exact_left(mat01, x):
    a, b, c = _split3(x)
    return _dot(mat01, a) + _dot(mat01, b) + _dot(mat01, c)


def _dot_exact_right(x, mat01):
    a, b, c = _split3(x)
    return _dot(a, mat01) + _dot(b, mat01) + _dot(c, mat01)


def _layer_norm(y, g, b):
    mu = jnp.mean(y, axis=-1, keepdims=True)
    d = y - mu
    var = jnp.mean(d * d, axis=-1, keepdims=True)
    return d * lax.rsqrt(var + LN_EPS) * g + b


def _round_robin(phased):
    last = [None] * len(phased)
    live = list(range(len(phased)))
    while live:
        for g in list(live):
            try:
                out = next(phased[g])
            except StopIteration:
                live.remove(g)
            else:
                if out is not None:
                    last[g] = out
    return last


def _pad_rows(x, rows):
    if x.shape[0] == rows:
        return x
    return jnp.concatenate([x, jnp.zeros((rows - x.shape[0],) + x.shape[1:], x.dtype)], axis=0)


def _swiglu_ln(x, wi_ref, wo_ref, g_ref, b_ref, o_ref, acc_ref, alpha, d_ff, tf):
    xb = x.astype(BF16)
    for c in range(d_ff // tf):
        gate = _dot(xb, wi_ref[:, c * tf:(c + 1) * tf])
        up = _dot(xb, wi_ref[:, d_ff + c * tf:d_ff + (c + 1) * tf])
        h = (gate * jax.nn.sigmoid(gate) * up).astype(BF16)
        part = _dot(h, wo_ref[c * tf:(c + 1) * tf, :])
        if c == 0:
            acc_ref[...] = part
        else:
            acc_ref[...] += part
    y = alpha * x + 0.5 * acc_ref[...]
    o_ref[...] = _layer_norm(y, g_ref[...], b_ref[...])


def _ffn_ln_body(x_ref, wi_ref, wo_ref, g_ref, b_ref, o_ref, acc_ref, *, alpha, d_ff, tf):
    _swiglu_ln(x_ref[...], wi_ref, wo_ref, g_ref, b_ref, o_ref, acc_ref, alpha, d_ff, tf)


def _ffn_ln(x, wi, wo, g, b, alpha):
    n, d = x.shape
    d_ff = wo.shape[0]
    tm = _tile(n, FFN_TILE)
    tf = _tile(d_ff, FFN_CHUNK, 128)
    return pl.pallas_call(
        functools.partial(_ffn_ln_body, alpha=alpha, d_ff=d_ff, tf=tf),
        grid=(n // tm,),
        in_specs=[pl.BlockSpec((tm, d), lambda i: (i, 0)),
                  _const_spec(wi.shape, single=True), _const_spec(wo.shape, single=True),
                  _const_spec((1, d)), _const_spec((1, d))],
        out_specs=pl.BlockSpec((tm, d), lambda i: (i, 0)),
        out_shape=jax.ShapeDtypeStruct((n, d), F32),
        scratch_shapes=[pltpu.VMEM((tm, d), F32)],
        compiler_params=_cparams(("parallel",)),
        name="ffn_ln",
    )(x, wi, wo, g.reshape(1, d), b.reshape(1, d))


_PROJ_OUTS = ("aqk", "av", "ao", "gi", "gf", "bq", "bk", "bv", "c")


def _in_proj_body(*refs, emit_vt, n_alias):
    x_ref, w_ref, bias_ref = refs[:3]
    out_refs = refs[3 + n_alias:]
    xb = x_ref[...].astype(BF16)
    tm = xb.shape[0]
    outs = dict(zip(_PROJ_OUTS, out_refs))
    dv = 2 * HEAD_DIM
    ones_rows = jnp.where(lax.broadcasted_iota(jnp.int32, (VT_ROWS - dv, Q_STRIP), 0) == 0, 1.0, 0.0).astype(BF16)
    for name in _PROJ_OUTS:
        c0, w = _SEGS[name]
        for s in range(0, w, 2 * HB_W):
            e = min(s + 2 * HB_W, w)
            val = _dot(xb, w_ref[:, c0 + s:c0 + e]) + bias_ref[:, c0 + s:c0 + e]
            if name in ("bk", "bv"):
                for h in range(N_HEADS):
                    outs[name][pl.ds(h, tm, stride=N_HEADS), :] = val[:, h * 2 * HEAD_DIM:(h + 1) * 2 * HEAD_DIM]
            else:
                outs[name][:, s:e] = val
            if name == "bk":
                out_refs[len(_PROJ_OUTS)][:, s:e] = val.astype(BF16)
            if name == "bv" and emit_vt:
                vt_ref = out_refs[len(_PROJ_OUTS) + 1]
                for h in range(N_HEADS):
                    vt = val[:, h * dv:(h + 1) * dv].T.astype(BF16)
                    for j in range(tm // Q_STRIP):
                        vt_ref[h, j, 0:dv, :] = vt[:, j * Q_STRIP:(j + 1) * Q_STRIP]
                        vt_ref[h, j, dv:VT_ROWS, :] = ones_rows


def _in_proj(x, w_cat, b_cat, emit_vt, layer, depth, kv_bufs):
    n, d = x.shape
    tm = _tile(n, 512, Q_STRIP if emit_vt else 8)
    out_shape, out_specs = [], []
    for k in _PROJ_OUTS:
        if k in ("bk", "bv"):
            out_shape.append(jax.ShapeDtypeStruct((depth, n * N_HEADS, 2 * HEAD_DIM), F32))
            out_specs.append(pl.BlockSpec((None, tm * N_HEADS, 2 * HEAD_DIM), lambda i: (layer, i, 0)))
        else:
            out_shape.append(jax.ShapeDtypeStruct((n, _SEGS[k][1]), F32))
            out_specs.append(pl.BlockSpec((tm, _SEGS[k][1]), lambda i: (i, 0)))
    out_shape.append(jax.ShapeDtypeStruct((n, ATT_W), BF16))
    out_specs.append(pl.BlockSpec((tm, ATT_W), lambda i: (i, 0)))
    if emit_vt:
        out_shape.append(jax.ShapeDtypeStruct((N_HEADS, n // Q_STRIP, VT_ROWS, Q_STRIP), BF16))
        out_specs.append(pl.BlockSpec((N_HEADS, tm // Q_STRIP, VT_ROWS, Q_STRIP), lambda i: (0, i, 0, 0)))
    in_specs = [pl.BlockSpec((tm, d), lambda i: (i, 0)), _const_spec(w_cat.shape), _const_spec(b_cat.shape)]
    args = [x, w_cat, b_cat]
    aliases = {}
    for buf, name in zip(kv_bufs, ("bk", "bv")):
        aliases[len(args)] = _PROJ_OUTS.index(name)
        in_specs.append(pl.BlockSpec(memory_space=pl.ANY))
        args.append(buf)
    return pl.pallas_call(
        functools.partial(_in_proj_body, emit_vt=emit_vt, n_alias=len(aliases)),
        grid=(n // tm,),
        in_specs=in_specs,
        out_specs=out_specs,
        out_shape=out_shape,
        input_output_aliases=aliases,
        compiler_params=_cparams(("parallel",)),
        name="in_proj",
    )(*args)


def _out_proj_ffn_body(ya_ref, yb_ref, yc_ref, x_ref, w_ref, g1_ref, b1_ref, wi_ref, wo_ref, g2_ref, b2_ref,
                       o_ref, acc_ref, *, alpha, d_ff, tf):
    y = _dot(ya_ref[...].astype(BF16), w_ref[0:HB_W, :])
    y += _dot(yb_ref[...].astype(BF16), w_ref[HB_W:HB_W + ATT_W, :])
    y += _dot(yc_ref[...].astype(BF16), w_ref[HB_W + ATT_W:, :])
    x = _layer_norm(alpha * x_ref[...] + y, g1_ref[...], b1_ref[...])
    _swiglu_ln(x, wi_ref, wo_ref, g2_ref, b2_ref, o_ref, acc_ref, alpha, d_ff, tf)


def _out_proj_ffn(ya, yb, yc, x, w, g1, b1, wi, wo, g2, b2, alpha):
    n, d = x.shape
    d_ff = wo.shape[0]
    tm = _tile(n, FFN_TILE)
    tf = _tile(d_ff, FFN_CHUNK, 128)
    row = lambda width: pl.BlockSpec((tm, width), lambda i: (i, 0))
    vec = _const_spec((1, d))
    return pl.pallas_call(
        functools.partial(_out_proj_ffn_body, alpha=alpha, d_ff=d_ff, tf=tf),
        grid=(n // tm,),
        in_specs=[row(HB_W), row(ATT_W), row(HB_W), row(d), _const_spec(w.shape, single=True), vec, vec,
                  _const_spec(wi.shape, single=True), _const_spec(wo.shape, single=True), vec, vec],
        out_specs=row(d),
        out_shape=jax.ShapeDtypeStruct((n, d), F32),
        scratch_shapes=[pltpu.VMEM((tm, d), F32)],
        compiler_params=_cparams(("parallel",)),
        name="out_proj_ffn",
    )(ya, yb, yc, x, w, g1.reshape(1, d), b1.reshape(1, d), wi, wo, g2.reshape(1, d), b2.reshape(1, d))


def _mlstm_body(qk_ref, v_ref, o_ref, gi_ref, gf_ref, cw_ref, cb_ref, conv0_ref, c0_ref, n0_ref, m0_ref,
                tri_ref, eblk_ref,
                ya_ref, c1_ref, n1_ref, m1_ref, convn_ref,
                up_ref, c_ref, n_ref, m_ref, *, rows, streams):
    step = pl.program_id(1)
    low = lax.broadcasted_iota(jnp.int32, (1, 2 * HEAD_DIM), 1) < HEAD_DIM
    pw = 2 * HEAD_DIM

    @pl.when(step == 0)
    def _():
        for g in range(streams):
            up_ref[g, 0:CONV_PAD, :] = conv0_ref[g]
            c_ref[g] = jnp.zeros(c_ref.shape[1:], F32)
            for i in range(N_HEADS // 2):
                pair = c0_ref[g, i]
                c_ref[g, i * pw:i * pw + HEAD_DIM, i * pw:(i + 1) * pw] = jnp.where(low, pair, 0.0)
                c_ref[g, i * pw + HEAD_DIM:(i + 1) * pw, i * pw:(i + 1) * pw] = jnp.where(low, 0.0, pair)
        n_ref[...] = n0_ref[...]
        m_ref[...] = m0_ref[...]

    new_states = _round_robin([_mlstm_stream(
        qk_ref.at[g], v_ref.at[g], o_ref.at[g], gi_ref.at[g], gf_ref.at[g], cw_ref, cb_ref, tri_ref, eblk_ref,
        ya_ref.at[g], convn_ref.at[g], up_ref.at[g], c_ref.at[g], n_ref.at[g], m_ref.at[g], rows=rows)
        for g in range(streams)])

    @pl.when(step == pl.num_programs(1) - 1)
    def _():
        for g, (c_new, n_new, m_new) in enumerate(new_states):
            for i in range(N_HEADS // 2):
                c1_ref[g, i] = jnp.where(low, c_new[i * pw:i * pw + HEAD_DIM, i * pw:(i + 1) * pw],
                                         c_new[i * pw + HEAD_DIM:(i + 1) * pw, i * pw:(i + 1) * pw])
            n1_ref[g] = n_new
            m1_ref[g] = m_new


def _mlstm_stream(qk_ref, v_ref, o_ref, gi_ref, gf_ref, cw_ref, cb_ref, tri_ref, eblk_ref,
                  ya_ref, convn_ref, up_ref, c_ref, n_ref, m_ref, *, rows):
    L = tri_ref.shape[0]

    gi = _pad_rows(gi_ref[...], L)
    flog = jax.nn.log_sigmoid(_pad_rows(gf_ref[...], L))
    if rows < L:
        live = lax.broadcasted_iota(jnp.int32, (L, 1), 0) < rows
        gi = jnp.where(live, gi, NEG_BIG)
        flog = jnp.where(live, flog, 0.0)
    b = _dot_exact_left(tri_ref[...], flog)
    b_t = b.T
    gi_t = gi.T
    yield

    u = _pad_rows(qk_ref[...], L)
    up_ref[CONV_PAD:CONV_PAD + L, :] = u
    y = cb_ref[...]
    first = CONV_PAD - (CONV_W - 1)
    for j in range(CONV_W):
        y = y + cw_ref[j:j + 1, :] * up_ref[first + j:first + j + L, :]
    convn_ref[...] = up_ref[rows:rows + CONV_PAD, :]
    up_ref[0:CONV_PAD, :] = up_ref[L:L + CONV_PAD, :]
    qk = y * jax.nn.sigmoid(y)
    q = qk[:, :HB_W] * (HEAD_DIM ** -0.5)
    k = qk[:, HB_W:]
    v = _pad_rows(v_ref[...], L)
    kb = k.astype(BF16)
    vb = v.astype(BF16)
    qb = q.astype(BF16)

    lane_head = lax.broadcasted_iota(jnp.int32, (1, HB_W), 1) // HEAD_DIM
    qk_scores = [_dot_nt(jnp.where(lane_head == h, q, 0.0).astype(BF16), kb) for h in range(N_HEADS)]
    c_prev = c_ref[...]
    n_prev = n_ref[...]
    eblk = eblk_ref[...]
    num_state = _dot(qb, c_prev.astype(BF16))
    den_state = _dot((q * n_prev).astype(BF16), eblk)
    yield

    m_prev = m_ref[...]
    g = b + m_prev

    t_idx = lax.broadcasted_iota(jnp.int32, (L, L), 0)
    s_idx = lax.broadcasted_iota(jnp.int32, (L, L), 1)
    causal = s_idx <= t_idx

    mrow_hb = jnp.zeros((L, HB_W), F32)
    wg_hb = jnp.zeros((L, HB_W), F32)
    den_hb = jnp.zeros((L, HB_W), F32)
    num_hb = jnp.zeros((L, HB_W), F32)
    for h in range(N_HEADS):
        sel = lane_head == h
        c0 = h * HEAD_DIM
        dmat = b[:, c0:c0 + 1] - b_t[c0:c0 + 1, :] + gi_t[c0:c0 + 1, :]
        dmat = jnp.where(causal, dmat, NEG_BIG)
        gcol = g[:, c0:c0 + 1]
        mrow = jnp.maximum(gcol, jnp.max(dmat, axis=1, keepdims=True))
        wd = jnp.exp(dmat - mrow)
        wg = jnp.exp(gcol - mrow)
        qkw = qk_scores[h] * wd
        den = jnp.sum(qkw, axis=1, keepdims=True)
        num = _dot(qkw.astype(BF16), vb)
        mrow_hb = jnp.where(sel, mrow, mrow_hb)
        wg_hb = jnp.where(sel, wg, wg_hb)
        den_hb = jnp.where(sel, den, den_hb)
        num_hb = jnp.where(sel, num, num_hb)

    num_hb = wg_hb * num_state + num_hb
    den_hb = wg_hb * den_state + den_hb
    hout = num_hb / jnp.maximum(jnp.abs(den_hb), jnp.exp(-mrow_hb))
    ya = jax.nn.sigmoid(_pad_rows(o_ref[...], L)) * hout
    ya_ref[...] = ya[:rows]
    yield

    m_new = mrow_hb[L - 1:L, :]
    b_last = b[L - 1:L, :]
    decay = jnp.exp(b_last + m_prev - m_new)
    ws = jnp.exp(b_last - b + gi - m_new)
    kw = k * ws
    upd = _dot(kw.T.astype(BF16), vb) * eblk.astype(F32)
    c_new = decay * c_prev + upd
    n_new = decay * n_prev + jnp.sum(kw, axis=0, keepdims=True)
    c_ref[...] = c_new
    n_ref[...] = n_new
    m_ref[...] = m_new
    yield c_new, n_new, m_new


def _mlstm_rows(t_len):
    rows = _tile(t_len, MLSTM_ROWS, 8)
    return rows, max(rows, SCAN_ROWS)


def _mlstm(proj, conv_w, conv_b, conv0, c0_pk, n0, m0_hb, tri, eblk, batch, t_len):
    rows, length = _mlstm_rows(t_len)
    assert tri.shape == (length, length)
    steps = t_len // rows
    g = SCAN_STREAMS
    tok = lambda width: pl.BlockSpec((g, rows, width), lambda bi, ci: (bi, ci, 0))
    per_b = lambda shape: pl.BlockSpec((g,) + shape, lambda bi, ci: (bi,) + (0,) * len(shape))
    seq = lambda a: a.reshape(batch, t_len, a.shape[-1])
    outs = pl.pallas_call(
        functools.partial(_mlstm_body, rows=rows, streams=g),
        grid=(batch // g, steps),
        in_specs=[tok(2 * HB_W), tok(HB_W), tok(HB_W), tok(HB_W), tok(HB_W),
                  _const_spec(conv_w.shape), _const_spec(conv_b.shape),
                  per_b((CONV_PAD, 2 * HB_W)), per_b((N_HEADS // 2, HEAD_DIM, 2 * HEAD_DIM)), per_b((1, HB_W)),
                  per_b((1, HB_W)), _const_spec(tri.shape), _const_spec(eblk.shape)],
        out_specs=[tok(HB_W), per_b((N_HEADS // 2, HEAD_DIM, 2 * HEAD_DIM)), per_b((1, HB_W)), per_b((1, HB_W)),
                   per_b((CONV_PAD, 2 * HB_W))],
        out_shape=[jax.ShapeDtypeStruct((batch, t_len, HB_W), F32),
                   jax.ShapeDtypeStruct((batch, N_HEADS // 2, HEAD_DIM, 2 * HEAD_DIM), F32),
                   jax.ShapeDtypeStruct((batch, 1, HB_W), F32),
                   jax.ShapeDtypeStruct((batch, 1, HB_W), F32),
                   jax.ShapeDtypeStruct((batch, CONV_PAD, 2 * HB_W), F32)],
        scratch_shapes=[pltpu.VMEM((g, length + CONV_PAD, 2 * HB_W), F32),
                        pltpu.VMEM((g, HB_W, HB_W), F32),
                        pltpu.VMEM((g, 1, HB_W), F32),
                        pltpu.VMEM((g, 1, HB_W), F32)],
        compiler_params=_cparams(("parallel", "arbitrary")),
        name="mlstm",
    )(seq(proj["aqk"]), seq(proj["av"]), seq(proj["ao"]), seq(proj["gi"]), seq(proj["gf"]), conv_w, conv_b,
      conv0, c0_pk, n0, m0_hb, tri, eblk)
    return (outs[0].reshape(batch * t_len, HB_W),) + tuple(outs[1:])


def _hgrn_body(c_ref, lb_ref, hng_ref, s0_ref, tri16_ref, eblk_ref, esel_ref, erep_ref, yc_ref, s1_ref, s_ref,
               *, rows, length, seqs):
    step = pl.program_id(1)
    L = length

    @pl.when(step == 0)
    def _():
        s_ref[...] = s0_ref[...]

    cin = _pad_rows(c_ref[...], L)
    q = cin[:, 0:HB_W]
    f_pre = cin[:, HB_W:2 * HB_W]
    v = cin[:, 2 * HB_W:3 * HB_W]
    gate = cin[:, 3 * HB_W:]
    lb = lb_ref[...]
    forget = lb + (1.0 - lb) * jax.nn.sigmoid(f_pre)
    logf = jnp.log(forget)
    key = 1.0 - forget
    if rows < L:
        live = lax.broadcasted_iota(jnp.int32, (L, 1), 0) < rows
        logf = jnp.where(live, logf, 0.0)
        key = jnp.where(live, key, 0.0)
    bl = _dot_exact_left(tri16_ref[...], logf) * LOG2E
    eblk = eblk_ref[...]
    groups = L // SUB
    t_loc = lax.broadcasted_iota(jnp.int32, (L, 1), 0) % SUB
    vb = v.astype(BF16)

    def from_source(x, s):
        picked = x.reshape(groups, SUB, HB_W)[:, s:s + 1, :]
        return jnp.broadcast_to(picked, (groups, SUB, HB_W)).reshape(L, HB_W)

    n_sub = max(rows // SUB, 1)
    qd = (q * jnp.exp2(bl)).astype(BF16)
    bl_t = bl.T
    low = lax.broadcasted_iota(jnp.int32, (1, 2 * HEAD_DIM), 1) < HEAD_DIM
    zero_b = jnp.zeros((HEAD_DIM, 2 * HEAD_DIM), BF16)

    def contribution(j):
        r0 = j * SUB
        b_last = bl[r0 + SUB - 1:r0 + SUB, :]
        kw_j = (key[r0:r0 + SUB] * jnp.exp2(b_last - bl[r0:r0 + SUB])).astype(BF16)
        upd = lax.dot_general(kw_j, vb[r0:r0 + SUB], (((0,), (0,)), ((), ())),
                              preferred_element_type=F32)
        dec = jnp.exp2(bl_t[:, r0 + SUB - 1:r0 + SUB])
        pair = []
        for i in range(2):
            rk = 2 * i * HEAD_DIM
            u = jnp.where(low, upd[rk:rk + HEAD_DIM, rk:rk + 2 * HEAD_DIM],
                          upd[rk + HEAD_DIM:rk + 2 * HEAD_DIM, rk:rk + 2 * HEAD_DIM])
            d = jnp.where(low, dec[rk:rk + HEAD_DIM], dec[rk + HEAD_DIM:rk + 2 * HEAD_DIM])
            pair.append((d, u))
        return pair

    per_seq = n_sub // seqs
    states = [[s_ref[q, 0], s_ref[q, 1]] for q in range(seqs)]
    o_parts = []

    def advance(j):
        r0 = j * SUB
        pk = states[j // per_seq]
        pb = [x.astype(BF16) for x in pk]
        s_bd = jnp.concatenate([
            jnp.concatenate([jnp.where(low, pb[0], 0.0).astype(BF16), zero_b], axis=1),
            jnp.concatenate([jnp.where(low, 0.0, pb[0]).astype(BF16), zero_b], axis=1),
            jnp.concatenate([zero_b, jnp.where(low, pb[1], 0.0).astype(BF16)], axis=1),
            jnp.concatenate([zero_b, jnp.where(low, 0.0, pb[1]).astype(BF16)], axis=1)], axis=0)
        o_parts.append(_dot(qd[r0:r0 + SUB], s_bd))
        pk[:] = [d * p + u for p, (d, u) in zip(pk, contrib[j])]

    contrib = []
    att = jnp.zeros((L, 2 * HEAD_DIM), F32)
    for s in range(SUB):
        w = jnp.exp2(bl - from_source(bl, s))
        p = jnp.where(t_loc >= s, q * from_source(key, s) * w, 0.0)
        att = att + _dot(p.astype(BF16), esel_ref[s])
        while len(contrib) < ((s + 1) * n_sub) // SUB:
            contrib.append(contribution(len(contrib)))
    for j in range(n_sub):
        advance(j)
    for q_i in range(seqs):
        s_ref[q_i, 0] = states[q_i][0]
        s_ref[q_i, 1] = states[q_i][1]
    attb = att.astype(BF16)
    same_sub = (lax.broadcasted_iota(jnp.int32, (L, L), 0) // SUB
                == lax.broadcasted_iota(jnp.int32, (L, L), 1) // SUB)
    lane_head = lax.broadcasted_iota(jnp.int32, (1, HB_W), 1) // HEAD_DIM
    o = jnp.zeros((L, HB_W), F32)
    for h in range(N_HEADS):
        full = jnp.where(same_sub, _dot(attb, erep_ref[h]), 0.0)
        o = jnp.where(lane_head == h, _dot(full.astype(BF16), vb), o)

    o_inter = jnp.concatenate(o_parts, axis=0)
    if rows < L:
        o_inter = _pad_rows(o_inter, L)
    o = o + o_inter

    ms = _dot_exact_right(o * o, eblk) * (1.0 / HEAD_DIM)
    yc = o * lax.rsqrt(ms + RMS_EPS) * hng_ref[...] * (gate * jax.nn.sigmoid(gate))
    yc_ref[...] = yc[:rows]

    @pl.when(step == pl.num_programs(1) - 1)
    def _():
        for q_i in range(seqs):
            s1_ref[q_i, 0] = states[q_i][0]
            s1_ref[q_i, 1] = states[q_i][1]


def _hgrn_rows(t_len, batch=1):
    if t_len < SCAN_ROWS and t_len % SUB == 0 and SCAN_ROWS % t_len == 0 and batch % (SCAN_ROWS // t_len) == 0:
        return SCAN_ROWS, SCAN_ROWS, SCAN_ROWS // t_len
    rows = _tile(t_len, HGRN_ROWS, SUB)
    return rows, max(rows, SCAN_ROWS), 1


def _hgrn(c_all, lb, hn_g, s0_pk, tri16, eblk, esel, erep, batch, t_len):
    rows, length, seqs = _hgrn_rows(t_len, batch)
    n = batch * t_len
    steps = n // (batch // seqs) // rows
    pk_spec = pl.BlockSpec((seqs, 2, HEAD_DIM, 2 * HEAD_DIM), lambda bi, ci: (bi, 0, 0, 0))
    return pl.pallas_call(
        functools.partial(_hgrn_body, rows=rows, length=length, seqs=seqs),
        grid=(batch // seqs, steps),
        in_specs=[pl.BlockSpec((rows, 4 * HB_W), lambda bi, ci: (bi * steps + ci, 0)),
                  _const_spec((1, HB_W)), _const_spec((1, HB_W)), pk_spec,
                  _const_spec(tri16.shape), _const_spec(eblk.shape), _const_spec(esel.shape),
                  _const_spec(erep.shape)],
        out_specs=[pl.BlockSpec((rows, HB_W), lambda bi, ci: (bi * steps + ci, 0)), pk_spec],
        out_shape=[jax.ShapeDtypeStruct((n, HB_W), F32),
                   jax.ShapeDtypeStruct((batch, 2, HEAD_DIM, 2 * HEAD_DIM), F32)],
        scratch_shapes=[pltpu.VMEM((seqs, 2, HEAD_DIM, 2 * HEAD_DIM), F32)],
        compiler_params=_cparams(("parallel", "arbitrary")),
        name="hgrn",
    )(c_all, lb, hn_g, s0_pk, tri16, eblk, esel, erep)


def _attn_prompt_body(qi_tab, ki_tab, lam_ref, q_ref, k_ref, vt_ref, g_ref, o_ref, qx_ref, m_ref, acc_ref,
                      s_ref, bmax_ref, *, tq, out_scale):
    pair = pl.program_id(2)
    qi = qi_tab[pair]
    ki = ki_tab[pair]
    ns = tq // Q_STRIP
    dv = 2 * HEAD_DIM

    @pl.when(ki == 0)
    def _():
        qt = (q_ref[...] * (HEAD_DIM ** -0.5 * LOG2E)).T
        first = lax.broadcasted_iota(jnp.int32, (dv, 1), 0) < HEAD_DIM
        qa = jnp.where(first, qt, 0.0).astype(BF16)
        qb = jnp.where(first, 0.0, qt).astype(BF16)
        for st in range(ns):
            qx_ref[0, st] = qa[:, st * Q_STRIP:(st + 1) * Q_STRIP]
            qx_ref[1, st] = qb[:, st * Q_STRIP:(st + 1) * Q_STRIP]
        m_ref[...] = jnp.full(m_ref.shape, NEG_BIG, F32)
        acc_ref[...] = jnp.zeros(acc_ref.shape, F32)

    def scores(kb, st, diag):
        k = k_ref[kb * Q_STRIP:(kb + 1) * Q_STRIP, :]
        for mp in range(2):
            s = _dot(k, qx_ref[mp, st])
            if diag and st == kb:
                k_chunk = lax.broadcasted_iota(jnp.int32, (Q_STRIP, 1), 0) // CHUNK
                q_chunk = lax.broadcasted_iota(jnp.int32, (1, Q_STRIP), 1) // CHUNK
                s = jnp.where(k_chunk <= q_chunk, s, NEG_BIG)
            s_ref[kb % 2, mp, st] = s
            bmax_ref[kb % 2, mp, st] = jnp.max(s, axis=0, keepdims=True)

    def accumulate(kb, st):
        vt = vt_ref[kb]
        for mp in range(2):
            m_old = m_ref[mp, st]
            m_new = jnp.maximum(m_old, bmax_ref[kb % 2, mp, st])
            p = jnp.exp2(s_ref[kb % 2, mp, st] - m_new).astype(BF16)
            corr = jnp.exp2(m_old - m_new)
            acc_ref[mp, st] = corr * acc_ref[mp, st] + _dot(vt, p)
            m_ref[mp, st] = m_new

    def key_tile(diag):
        first_strip = (lambda kb: kb) if diag else (lambda kb: 0)
        for st in range(ns):
            scores(0, st, diag)
        for kb in range(ns):
            for st in range(first_strip(kb), ns):
                accumulate(kb, st)
                if kb + 1 < ns and st >= first_strip(kb + 1):
                    scores(kb + 1, st, diag)

    @pl.when(ki < qi)
    def _():
        key_tile(False)

    @pl.when(ki == qi)
    def _():
        key_tile(True)
        lam = lam_ref[0, 0]
        for st in range(ns):
            a1 = acc_ref[0, st]
            a2 = acc_ref[1, st]
            o = a1[:dv] / a1[dv:dv + 1] - lam * (a2[:dv] / a2[dv:dv + 1])
            ms = jnp.mean(o * o, axis=0, keepdims=True)
            o = o * lax.rsqrt(ms + RMS_EPS) * g_ref[...] * out_scale
            o_ref[st * Q_STRIP:(st + 1) * Q_STRIP, :] = o.T


def _attn_prompt(lam, bq, bkb, vt, dn_g, batch, t_len, out_scale):
    tq = _tile(t_len, ATT_TILE, Q_STRIP)
    nq = t_len // tq
    ns = tq // Q_STRIP
    n = batch * t_len
    w = 2 * HEAD_DIM
    pairs = [(qi, ki) for qi in range(nq) for ki in range(qi + 1)]
    qi_tab = jnp.asarray([p[0] for p in pairs], jnp.int32)
    ki_tab = jnp.asarray([p[1] for p in pairs], jnp.int32)
    qmap = lambda b, h, p, qt, kt: (b * nq + qt[p], h)
    kmap = lambda b, h, p, qt, kt: (b * nq + kt[p], h)
    vmap_ = lambda b, h, p, qt, kt: (h, b * nq + kt[p], 0, 0)
    grid_spec = pltpu.PrefetchScalarGridSpec(
        num_scalar_prefetch=2,
        grid=(batch, N_HEADS, len(pairs)),
        in_specs=[pl.BlockSpec(memory_space=pltpu.SMEM),
                  pl.BlockSpec((tq, w), qmap), pl.BlockSpec((tq, w), kmap),
                  pl.BlockSpec((None, ns, VT_ROWS, Q_STRIP), vmap_),
                  pl.BlockSpec((w, 1), lambda b, h, p, qt, kt: (0, 0))],
        out_specs=pl.BlockSpec((tq, w), qmap),
        scratch_shapes=[pltpu.VMEM((2, ns, w, Q_STRIP), BF16),
                        pltpu.VMEM((2, ns, 1, Q_STRIP), F32),
                        pltpu.VMEM((2, ns, VT_ROWS, Q_STRIP), F32),
                        pltpu.VMEM((2, 2, ns, Q_STRIP, Q_STRIP), F32),
                        pltpu.VMEM((2, 2, ns, 1, Q_STRIP), F32)])
    return pl.pallas_call(
        functools.partial(_attn_prompt_body, tq=tq, out_scale=out_scale),
        grid_spec=grid_spec,
        out_shape=jax.ShapeDtypeStruct((n, ATT_W), F32),
        compiler_params=_cparams(("parallel", "parallel", "arbitrary")),
        name="attn_prompt",
    )(qi_tab, ki_tab, lam, bq, bkb, vt, dn_g.reshape(w, 1))


def _attn_sample_body(lam_ref, q_ref, kn_ref, vn_ref, kp_ref, vp_ref, g_ref, o_ref, *, t_new, past, out_scale):
    w = 2 * HEAD_DIM
    first = lax.broadcasted_iota(jnp.int32, (1, w), 1) < HEAD_DIM
    for h in range(N_HEADS):
        cols = slice(h * w, (h + 1) * w)
        q = q_ref[:, cols] * (HEAD_DIM ** -0.5)
        q2 = jnp.concatenate([jnp.where(first, q, 0.0), jnp.where(first, 0.0, q)], axis=0).astype(BF16)
        kp = kp_ref[pl.ds(h, past, stride=N_HEADS), :].astype(BF16)
        vp = vp_ref[pl.ds(h, past, stride=N_HEADS), :].astype(BF16)
        s_p = _dot_nt(q2, kp)
        kn = kn_ref[pl.ds(h, t_new, stride=N_HEADS), :].astype(BF16)
        vn = vn_ref[pl.ds(h, t_new, stride=N_HEADS), :].astype(BF16)
        s_n = _dot_nt(q2, kn)
        m = jnp.maximum(jnp.max(s_p, axis=1, keepdims=True), jnp.max(s_n, axis=1, keepdims=True))
        p_p = jnp.exp(s_p - m)
        p_n = jnp.exp(s_n - m)
        l = jnp.sum(p_p, axis=1, keepdims=True) + jnp.sum(p_n, axis=1, keepdims=True)
        acc = _dot(p_p.astype(BF16), vp) + _dot(p_n.astype(BF16), vn)
        o2 = acc / l
        o = o2[:t_new] - lam_ref[0, 0] * o2[t_new:]
        ms = jnp.mean(o * o, axis=1, keepdims=True)
        o_ref[:, cols] = o * lax.rsqrt(ms + RMS_EPS) * g_ref[...] * out_scale


def _attn_sample(lam, bq, bk, bv, cache_k, cache_v, layer, dn_g, batch, t_new, out_scale):
    w = 2 * HEAD_DIM
    depth, _, past = cache_k.shape[:3]
    rows = past * N_HEADS
    new = pl.BlockSpec((t_new, ATT_W), lambda b: (b, 0))
    new_kv = pl.BlockSpec((None, t_new * N_HEADS, w), lambda b: (layer, b, 0))
    old = pl.BlockSpec((None, None, rows, w), lambda b: (layer, b, 0, 0))
    return pl.pallas_call(
        functools.partial(_attn_sample_body, t_new=t_new, past=past, out_scale=out_scale),
        grid=(batch,),
        in_specs=[pl.BlockSpec(memory_space=pltpu.SMEM), new, new_kv, new_kv, old, old, _const_spec((1, w))],
        out_specs=new,
        out_shape=jax.ShapeDtypeStruct((batch * t_new, ATT_W), F32),
        compiler_params=_cparams(("parallel",)),
        name="attn_sample",
    )(lam, bq, bk, bv, cache_k.reshape(depth, batch, rows, w), cache_v.reshape(depth, batch, rows, w),
      dn_g.reshape(1, w))


def _to_head_pairs(s):
    b = s.shape[0]
    r = s.reshape(b, N_HEADS // 2, 2, HEAD_DIM, HEAD_DIM)
    return jnp.swapaxes(r, 2, 3).reshape(b, N_HEADS // 2, HEAD_DIM, 2 * HEAD_DIM)


def _from_head_pairs(p):
    b = p.shape[0]
    r = p.reshape(b, N_HEADS // 2, HEAD_DIM, 2, HEAD_DIM)
    return jnp.swapaxes(r, 2, 3).reshape(b, N_HEADS, HEAD_DIM, HEAD_DIM)


def _rearrange_proj(a):
    gates = lambda c0: jnp.repeat(a[..., c0:c0 + N_HEADS], HEAD_DIM, axis=-1)
    out = jnp.concatenate([a[..., A_QK0:A_I0], a[..., A_O0:B_Q0], gates(A_I0), gates(A_F0), a[..., B_Q0:N_IN]],
                          axis=-1)
    assert out.shape[-1] == N_CAT
    return out


def _consts(hgrn_len, mlstm_len):
    r = jnp.arange(mlstm_len)
    tri = (r[None, :] <= r[:, None]).astype(BF16)
    r = jnp.arange(hgrn_len)
    tri16 = ((r[None, :] <= r[:, None]) & (r[None, :] // SUB == r[:, None] // SUB)).astype(BF16)
    hh = jnp.arange(HB_W) // HEAD_DIM
    eblk = (hh[None, :] == hh[:, None]).astype(BF16)
    lane = jnp.arange(2 * HEAD_DIM)
    src = jnp.arange(SUB)
    esel = ((hh[None, :, None] * SUB + src[:, None, None]) == lane[None, None, :]).astype(BF16)
    erep = ((lane[None, :, None] // SUB == jnp.arange(N_HEADS)[:, None, None])
            & (lane[None, :, None] % SUB == r[None, None, :] % SUB)).astype(BF16)
    return tri, tri16, eblk, esel, erep


def _group(x, depth, layer_w, states, cache, batch, t_len, alpha, consts):
    tri, tri16, eblk, esel, erep = consts
    prompt = cache is None
    outs = []
    kv_bufs = tuple(jnp.zeros((depth, batch * t_len * N_HEADS, 2 * HEAD_DIM), F32) for _ in range(2))
    for l in range(depth):
        w = layer_w[l]
        conv0, c0, n0, m0, s0 = states(l)
        x = _ffn_ln(x, w["f_in0"], w["f_out0"], w["ln_g"][0], w["ln_b"][0], alpha)
        res = _in_proj(x, w["w_cat"], w["b_cat"], prompt, l, depth, kv_bufs)
        proj = dict(zip(_PROJ_OUTS, res[:len(_PROJ_OUTS)]))
        bkb = res[len(_PROJ_OUTS)]
        kv_bufs = (proj["bk"], proj["bv"])

        conv0_p = jnp.pad(conv0, ((0, 0), (CONV_PAD - (CONV_W - 1), 0), (0, 0)))
        ya, c1, n1, m1, convn = _mlstm(proj, w["conv_w"], w["conv_b"], conv0_p, _to_head_pairs(c0),
                                       n0.reshape(batch, 1, HB_W),
                                       jnp.repeat(m0, HEAD_DIM, axis=-1).reshape(batch, 1, HB_W),
                                       tri, eblk, batch, t_len)
        lam_init = 0.8 - 0.6 * math.exp(-0.3 * l)
        if prompt:
            yb = _attn_prompt(w["lam"], proj["bq"], bkb, res[len(_PROJ_OUTS) + 1], w["dn_g"], batch, t_len,
                              1.0 - lam_init)
        else:
            yb = _attn_sample(w["lam"], proj["bq"], proj["bk"], proj["bv"], cache[0], cache[1], l, w["dn_g"],
                              batch, t_len, 1.0 - lam_init)
        yc, s1 = _hgrn(proj["c"], w["lb"], w["hn_g"], _to_head_pairs(s0), tri16, eblk, esel, erep, batch, t_len)
        x = _out_proj_ffn(ya, yb, yc, x, w["w_out"], w["ln_g"][1], w["ln_b"][1],
                          w["f_in1"], w["f_out1"], w["ln_g"][2], w["ln_b"][2], alpha)
        outs.append((_from_head_pairs(c1),
                     n1.reshape(batch, N_HEADS, HEAD_DIM),
                     m1.reshape(batch, N_HEADS, HEAD_DIM)[:, :, 0],
                     convn[:, CONV_PAD - (CONV_W - 1):, :],
                     _from_head_pairs(s1)))
    kv = tuple(buf.reshape(depth, batch, t_len, N_HEADS, 2 * HEAD_DIM) for buf in kv_bufs)
    return x, kv + tuple(jnp.stack(a) for a in zip(*outs))


def kernel(x_prompt, x_sample, cache_diff_k, cache_diff_v, state_mlstm_c, state_mlstm_n, state_mlstm_m,
           state_mlstm_conv, state_hgrn_s, w_in, b_in, w_out, mlstm_conv_w, mlstm_conv_b, diff_lambda,
           diff_norm_g, hgrn_lb_logits, hgrn_norm_g, ffn_w_in, ffn_w_out, ln_g, ln_b):
    depth = w_in.shape[0]
    alpha = (2.0 * depth) ** 0.25
    batch, seq, d_model = x_prompt.shape
    dec_batch, dec_seq, _ = x_sample.shape
    past = cache_diff_k.shape[2]

    p_lb = jax.nn.softmax(hgrn_lb_logits.astype(F32), axis=0)
    lower_bounds = jnp.cumsum(p_lb, axis=0) - p_lb[0]
    w_cat = _rearrange_proj(w_in.astype(BF16))
    b_cat = _rearrange_proj(b_in).reshape(depth, 1, N_CAT)
    lp = diff_lambda.astype(F32)
    lam_init = jnp.asarray([0.8 - 0.6 * math.exp(-0.3 * l) for l in range(depth)], F32)
    lam = (jnp.exp(jnp.sum(lp[:, 0] * lp[:, 1], axis=-1)) - jnp.exp(jnp.sum(lp[:, 2] * lp[:, 3], axis=-1))
           + lam_init)
    f_in = ffn_w_in.astype(BF16)
    f_out = ffn_w_out.astype(BF16)
    w_out_b = w_out.astype(BF16)
    layer_w = [dict(w_cat=w_cat[l], b_cat=b_cat[l],
                    conv_w=mlstm_conv_w[l], conv_b=mlstm_conv_b[l].reshape(1, 2 * HB_W),
                    lam=lam[l].reshape(1, 1), dn_g=diff_norm_g[l],
                    lb=lower_bounds[l].reshape(1, HB_W),
                    hn_g=jnp.tile(hgrn_norm_g[l], N_HEADS).reshape(1, HB_W),
                    w_out=w_out_b[l], f_in0=f_in[l, 0], f_out0=f_out[l, 0], f_in1=f_in[l, 1], f_out1=f_out[l, 1],
                    ln_g=ln_g[l], ln_b=ln_b[l]) for l in range(depth)]
    def zero_states(_):
        return (jnp.zeros((batch, CONV_W - 1, 2 * HB_W), F32), jnp.zeros((batch, N_HEADS, HEAD_DIM, HEAD_DIM), F32),
                jnp.zeros((batch, N_HEADS, HEAD_DIM), F32), jnp.zeros((batch, N_HEADS), F32),
                jnp.zeros((batch, N_HEADS, HEAD_DIM, HEAD_DIM), F32))

    def carried_states(l):
        return (state_mlstm_conv[l], state_mlstm_c[l], state_mlstm_n[l], state_mlstm_m[l], state_hgrn_s[l])

    y_p, p_out = _group(x_prompt.reshape(batch * seq, d_model), depth, layer_w, zero_states, None,
                        batch, seq, alpha, _consts(_hgrn_rows(seq)[1], _mlstm_rows(seq)[1]))
    cache = (cache_diff_k, cache_diff_v)
    y_s, s_out = _group(x_sample.reshape(dec_batch * dec_seq, d_model), depth, layer_w, carried_states, cache,
                        dec_batch, dec_seq, alpha, _consts(_hgrn_rows(dec_seq)[1], _mlstm_rows(dec_seq)[1]))
    return (y_p.reshape(batch, seq, d_model), y_s.reshape(dec_batch, dec_seq, d_model)) + p_out + s_out
```

```python
import functools
import math

import jax
import jax.numpy as jnp
from jax import lax
from jax.experimental import pallas as pl
from jax.experimental.pallas import tpu as pltpu

F32 = jnp.float32
BF16 = jnp.bfloat16

HEAD_DIM = 64
N_HEADS = 4
CONV_W = 4
CONV_PAD = 8
CHUNK = 64
HB_W = N_HEADS * HEAD_DIM
ATT_W = N_HEADS * 2 * HEAD_DIM
LN_EPS = 1e-5
RMS_EPS = 1e-6
NEG_BIG = -1e30
SCAN_ROWS = 128
MLSTM_ROWS = 128
HGRN_ROWS = 256
SUB = 16
LOG2E = 1.4426950408889634
VT_ROWS = 2 * HEAD_DIM + 16
Q_STRIP = 256
ATT_TILE = 2048
FFN_CHUNK = 256
FFN_TILE = 1024
V7X_VMEM_LIMIT = 56 * 1024 * 1024

A_QK0 = 0
A_V0 = A_QK0 + 2 * HB_W
A_I0 = A_V0 + HB_W
A_F0 = A_I0 + N_HEADS
A_O0 = A_F0 + N_HEADS
B_Q0 = A_O0 + HB_W
B_K0 = B_Q0 + ATT_W
B_V0 = B_K0 + ATT_W
C_Q0 = B_V0 + ATT_W
N_IN = C_Q0 + 4 * HB_W

_SEGS = {}
_off = 0
for _name, _w in (("aqk", 2 * HB_W), ("av", HB_W), ("ao", HB_W), ("gi", HB_W), ("gf", HB_W),
                  ("bq", ATT_W), ("bk", ATT_W), ("bv", ATT_W), ("c", 4 * HB_W)):
    _SEGS[_name] = (_off, _w)
    _off += _w
N_CAT = _off


def _tile(n, pref, mult=8):
    t = min(pref, n)
    while t > mult and (n % t or t % mult):
        t -= mult
    assert n % t == 0 and t % mult == 0, (n, pref, mult)
    return t


def _cparams(sem):
    return pltpu.CompilerParams(dimension_semantics=sem, vmem_limit_bytes=V7X_VMEM_LIMIT)


def _const_spec(shape, single=False):
    nd = len(shape)
    if single:
        return pl.BlockSpec(shape, lambda *_: (0,) * nd, pipeline_mode=pl.Buffered(1))
    return pl.BlockSpec(shape, lambda *_: (0,) * nd)


def _dot(a, b):
    return jnp.dot(a, b, preferred_element_type=F32)


def _dot_nt(a, b):
    return lax.dot_general(a, b, (((1,), (1,)), ((), ())), preferred_element_type=F32)


def _split3(x):
    a = x.astype(BF16)
    r = x - a.astype(F32)
    b = r.astype(BF16)
    c = (r - b.astype(F32)).astype(BF16)
    return a, b, c


def _dot_exact_left(mat01, x):
    a, b, c = _split3(x)
    return _dot(mat01, a) + _dot(mat01, b) + _dot(mat01, c)


def _dot_exact_right(x, mat01):
    a, b, c = _split3(x)
    return _dot(a, mat01) + _dot(b, mat01) + _dot(c, mat01)


def _layer_norm(y, g, b):
    mu = jnp.mean(y, axis=-1, keepdims=True)
    d = y - mu
    var = jnp.mean(d * d, axis=-1, keepdims=True)
    return d * lax.rsqrt(var + LN_EPS) * g + b


def _pad_rows(x, rows):
    if x.shape[0] == rows:
        return x
    return jnp.concatenate([x, jnp.zeros((rows - x.shape[0],) + x.shape[1:], x.dtype)], axis=0)


def _swiglu_ln(x, wi_ref, wo_ref, g_ref, b_ref, o_ref, acc_ref, alpha, d_ff, tf):
    xb = x.astype(BF16)
    for c in range(d_ff // tf):
        gate = _dot(xb, wi_ref[:, c * tf:(c + 1) * tf])
        up = _dot(xb, wi_ref[:, d_ff + c * tf:d_ff + (c + 1) * tf])
        h = (gate * jax.nn.sigmoid(gate) * up).astype(BF16)
        part = _dot(h, wo_ref[c * tf:(c + 1) * tf, :])
        if c == 0:
            acc_ref[...] = part
        else:
            acc_ref[...] += part
    y = alpha * x + 0.5 * acc_ref[...]
    o_ref[...] = _layer_norm(y, g_ref[...], b_ref[...])


def _ffn_ln_body(x_ref, wi_ref, wo_ref, g_ref, b_ref, o_ref, acc_ref, *, alpha, d_ff, tf):
    _swiglu_ln(x_ref[...], wi_ref, wo_ref, g_ref, b_ref, o_ref, acc_ref, alpha, d_ff, tf)


def _ffn_ln(x, wi, wo, g, b, alpha):
    n, d = x.shape
    d_ff = wo.shape[0]
    tm = _tile(n, FFN_TILE)
    tf = _tile(d_ff, FFN_CHUNK, 128)
    return pl.pallas_call(
        functools.partial(_ffn_ln_body, alpha=alpha, d_ff=d_ff, tf=tf),
        grid=(n // tm,),
        in_specs=[pl.BlockSpec((tm, d), lambda i: (i, 0)),
                  _const_spec(wi.shape, single=True), _const_spec(wo.shape, single=True),
                  _const_spec((1, d)), _const_spec((1, d))],
        out_specs=pl.BlockSpec((tm, d), lambda i: (i, 0)),
        out_shape=jax.ShapeDtypeStruct((n, d), F32),
        scratch_shapes=[pltpu.VMEM((tm, d), F32)],
        compiler_params=_cparams(("parallel",)),
        name="ffn_ln",
    )(x, wi, wo, g.reshape(1, d), b.reshape(1, d))


_PROJ_OUTS = ("aqk", "av", "ao", "gi", "gf", "bq", "bk", "bv", "c")


def _in_proj_body(*refs, emit_vt, n_alias):
    x_ref, w_ref, bias_ref = refs[:3]
    out_refs = refs[3 + n_alias:]
    xb = x_ref[...].astype(BF16)
    tm = xb.shape[0]
    outs = dict(zip(_PROJ_OUTS, out_refs))
    dv = 2 * HEAD_DIM
    ones_rows = jnp.where(lax.broadcasted_iota(jnp.int32, (VT_ROWS - dv, Q_STRIP), 0) == 0, 1.0, 0.0).astype(BF16)
    for name in _PROJ_OUTS:
        c0, w = _SEGS[name]
        for s in range(0, w, 2 * HB_W):
            e = min(s + 2 * HB_W, w)
            val = _dot(xb, w_ref[:, c0 + s:c0 + e]) + bias_ref[:, c0 + s:c0 + e]
            if name in ("bk", "bv"):
                for h in range(N_HEADS):
                    outs[name][pl.ds(h, tm, stride=N_HEADS), :] = val[:, h * 2 * HEAD_DIM:(h + 1) * 2 * HEAD_DIM]
            else:
                outs[name][:, s:e] = val
            if name == "bk":
                out_refs[len(_PROJ_OUTS)][:, s:e] = val.astype(BF16)
            if name == "bv" and emit_vt:
                vt_ref = out_refs[len(_PROJ_OUTS) + 1]
                for h in range(N_HEADS):
                    vt = val[:, h * dv:(h + 1) * dv].T.astype(BF16)
                    for j in range(tm // Q_STRIP):
                        vt_ref[h, j, 0:dv, :] = vt[:, j * Q_STRIP:(j + 1) * Q_STRIP]
                        vt_ref[h, j, dv:VT_ROWS, :] = ones_rows


def _in_proj(x, w_cat, b_cat, emit_vt, layer, depth, kv_bufs):
    n, d = x.shape
    tm = _tile(n, 512, Q_STRIP if emit_vt else 8)
    out_shape, out_specs = [], []
    for k in _PROJ_OUTS:
        if k in ("bk", "bv"):
            out_shape.append(jax.ShapeDtypeStruct((depth, n * N_HEADS, 2 * HEAD_DIM), F32))
            out_specs.append(pl.BlockSpec((None, tm * N_HEADS, 2 * HEAD_DIM), lambda i: (layer, i, 0)))
        else:
            out_shape.append(jax.ShapeDtypeStruct((n, _SEGS[k][1]), F32))
            out_specs.append(pl.BlockSpec((tm, _SEGS[k][1]), lambda i: (i, 0)))
    out_shape.append(jax.ShapeDtypeStruct((n, ATT_W), BF16))
    out_specs.append(pl.BlockSpec((tm, ATT_W), lambda i: (i, 0)))
    if emit_vt:
        out_shape.append(jax.ShapeDtypeStruct((N_HEADS, n // Q_STRIP, VT_ROWS, Q_STRIP), BF16))
        out_specs.append(pl.BlockSpec((N_HEADS, tm // Q_STRIP, VT_ROWS, Q_STRIP), lambda i: (0, i, 0, 0)))
    in_specs = [pl.BlockSpec((tm, d), lambda i: (i, 0)), _const_spec(w_cat.shape), _const_spec(b_cat.shape)]
    args = [x, w_cat, b_cat]
    aliases = {}
    for buf, name in zip(kv_bufs, ("bk", "bv")):
        aliases[len(args)] = _PROJ_OUTS.index(name)
        in_specs.append(pl.BlockSpec(memory_space=pl.ANY))
        args.append(buf)
    return pl.pallas_call(
        functools.partial(_in_proj_body, emit_vt=emit_vt, n_alias=len(aliases)),
        grid=(n // tm,),
        in_specs=in_specs,
        out_specs=out_specs,
        out_shape=out_shape,
        input_output_aliases=aliases,
        compiler_params=_cparams(("parallel",)),
        name="in_proj",
    )(*args)


def _out_proj_ffn_body(ya_ref, yb_ref, yc_ref, x_ref, w_ref, g1_ref, b1_ref, wi_ref, wo_ref, g2_ref, b2_ref,
                       o_ref, acc_ref, *, alpha, d_ff, tf):
    y = _dot(ya_ref[...].astype(BF16), w_ref[0:HB_W, :])
    y += _dot(yb_ref[...].astype(BF16), w_ref[HB_W:HB_W + ATT_W, :])
    y += _dot(yc_ref[...].astype(BF16), w_ref[HB_W + ATT_W:, :])
    x = _layer_norm(alpha * x_ref[...] + y, g1_ref[...], b1_ref[...])
    _swiglu_ln(x, wi_ref, wo_ref, g2_ref, b2_ref, o_ref, acc_ref, alpha, d_ff, tf)


def _out_proj_ffn(ya, yb, yc, x, w, g1, b1, wi, wo, g2, b2, alpha):
    n, d = x.shape
    d_ff = wo.shape[0]
    tm = _tile(n, FFN_TILE)
    tf = _tile(d_ff, FFN_CHUNK, 128)
    row = lambda width: pl.BlockSpec((tm, width), lambda i: (i, 0))
    vec = _const_spec((1, d))
    return pl.pallas_call(
        functools.partial(_out_proj_ffn_body, alpha=alpha, d_ff=d_ff, tf=tf),
        grid=(n // tm,),
        in_specs=[row(HB_W), row(ATT_W), row(HB_W), row(d), _const_spec(w.shape, single=True), vec, vec,
                  _const_spec(wi.shape, single=True), _const_spec(wo.shape, single=True), vec, vec],
        out_specs=row(d),
        out_shape=jax.ShapeDtypeStruct((n, d), F32),
        scratch_shapes=[pltpu.VMEM((tm, d), F32)],
        compiler_params=_cparams(("parallel",)),
        name="out_proj_ffn",
    )(ya, yb, yc, x, w, g1.reshape(1, d), b1.reshape(1, d), wi, wo, g2.reshape(1, d), b2.reshape(1, d))


def _mlstm_body(qk_ref, v_ref, o_ref, gi_ref, gf_ref, cw_ref, cb_ref, conv0_ref, c0_ref, n0_ref, m0_ref,
                tri_ref, eblk_ref,
                ya_ref, c1_ref, n1_ref, m1_ref, convn_ref,
                up_ref, c_ref, n_ref, m_ref, *, rows):
    step = pl.program_id(1)
    low = lax.broadcasted_iota(jnp.int32, (1, 2 * HEAD_DIM), 1) < HEAD_DIM
    pw = 2 * HEAD_DIM
    L = tri_ref.shape[0]

    @pl.when(step == 0)
    def _():
        up_ref[0:CONV_PAD, :] = conv0_ref[...]
        c_ref[...] = jnp.zeros(c_ref.shape, F32)
        for i in range(N_HEADS // 2):
            pair = c0_ref[i]
            c_ref[i * pw:i * pw + HEAD_DIM, i * pw:(i + 1) * pw] = jnp.where(low, pair, 0.0)
            c_ref[i * pw + HEAD_DIM:(i + 1) * pw, i * pw:(i + 1) * pw] = jnp.where(low, 0.0, pair)
        n_ref[...] = n0_ref[...]
        m_ref[...] = m0_ref[...]

    gi = _pad_rows(gi_ref[...], L)
    flog = jax.nn.log_sigmoid(_pad_rows(gf_ref[...], L))
    if rows < L:
        live = lax.broadcasted_iota(jnp.int32, (L, 1), 0) < rows
        gi = jnp.where(live, gi, NEG_BIG)
        flog = jnp.where(live, flog, 0.0)
    b = _dot_exact_left(tri_ref[...], flog)
    b_t = b.T
    gi_t = gi.T

    u = _pad_rows(qk_ref[...], L)
    up_ref[CONV_PAD:CONV_PAD + L, :] = u
    y = cb_ref[...]
    first = CONV_PAD - (CONV_W - 1)
    for j in range(CONV_W):
        y = y + cw_ref[j:j + 1, :] * up_ref[first + j:first + j + L, :]
    convn_ref[...] = up_ref[rows:rows + CONV_PAD, :]
    up_ref[0:CONV_PAD, :] = up_ref[L:L + CONV_PAD, :]
    qk = y * jax.nn.sigmoid(y)
    q = qk[:, :HB_W] * (HEAD_DIM ** -0.5)
    k = qk[:, HB_W:]
    v = _pad_rows(v_ref[...], L)
    kb = k.astype(BF16)
    vb = v.astype(BF16)
    qb = q.astype(BF16)

    lane_head = lax.broadcasted_iota(jnp.int32, (1, HB_W), 1) // HEAD_DIM
    qk_scores = [_dot_nt(jnp.where(lane_head == h, q, 0.0).astype(BF16), kb) for h in range(N_HEADS)]
    c_prev = c_ref[...]
    n_prev = n_ref[...]
    eblk = eblk_ref[...]
    num_state = _dot(qb, c_prev.astype(BF16))
    den_state = _dot((q * n_prev).astype(BF16), eblk)

    m_prev = m_ref[...]
    g = b + m_prev

    t_idx = lax.broadcasted_iota(jnp.int32, (L, L), 0)
    s_idx = lax.broadcasted_iota(jnp.int32, (L, L), 1)
    causal = s_idx <= t_idx

    mrow_hb = jnp.zeros((L, HB_W), F32)
    wg_hb = jnp.zeros((L, HB_W), F32)
    den_hb = jnp.zeros((L, HB_W), F32)
    num_hb = jnp.zeros((L, HB_W), F32)
    for h in range(N_HEADS):
        sel = lane_head == h
        c0 = h * HEAD_DIM
        dmat = b[:, c0:c0 + 1] - b_t[c0:c0 + 1, :] + gi_t[c0:c0 + 1, :]
        dmat = jnp.where(causal, dmat, NEG_BIG)
        gcol = g[:, c0:c0 + 1]
        mrow = jnp.maximum(gcol, jnp.max(dmat, axis=1, keepdims=True))
        wd = jnp.exp(dmat - mrow)
        wg = jnp.exp(gcol - mrow)
        qkw = qk_scores[h] * wd
        den = jnp.sum(qkw, axis=1, keepdims=True)
        num = _dot(qkw.astype(BF16), vb)
        mrow_hb = jnp.where(sel, mrow, mrow_hb)
        wg_hb = jnp.where(sel, wg, wg_hb)
        den_hb = jnp.where(sel, den, den_hb)
        num_hb = jnp.where(sel, num, num_hb)

    num_hb = wg_hb * num_state + num_hb
    den_hb = wg_hb * den_state + den_hb
    hout = num_hb / jnp.maximum(jnp.abs(den_hb), jnp.exp(-mrow_hb))
    ya = jax.nn.sigmoid(_pad_rows(o_ref[...], L)) * hout
    ya_ref[...] = ya[:rows]

    m_new = mrow_hb[L - 1:L, :]
    b_last = b[L - 1:L, :]
    decay = jnp.exp(b_last + m_prev - m_new)
    ws = jnp.exp(b_last - b + gi - m_new)
    kw = k * ws
    upd = _dot(kw.T.astype(BF16), vb) * eblk.astype(F32)
    c_new = decay * c_prev + upd
    n_new = decay * n_prev + jnp.sum(kw, axis=0, keepdims=True)
    c_ref[...] = c_new
    n_ref[...] = n_new
    m_ref[...] = m_new

    @pl.when(step == pl.num_programs(1) - 1)
    def _():
        for i in range(N_HEADS // 2):
            c1_ref[i] = jnp.where(low, c_new[i * pw:i * pw + HEAD_DIM, i * pw:(i + 1) * pw],
                                  c_new[i * pw + HEAD_DIM:(i + 1) * pw, i * pw:(i + 1) * pw])
        n1_ref[...] = n_new
        m1_ref[...] = m_new


def _mlstm_rows(t_len):
    rows = _tile(t_len, MLSTM_ROWS, 8)
    return rows, max(rows, SCAN_ROWS)


def _mlstm(proj, conv_w, conv_b, conv0, c0_pk, n0, m0_hb, tri, eblk, batch, t_len):
    rows, length = _mlstm_rows(t_len)
    assert tri.shape == (length, length)
    steps = t_len // rows
    tok = lambda width: pl.BlockSpec((rows, width), lambda bi, ci: (bi * steps + ci, 0))
    per_b = lambda shape: pl.BlockSpec((None,) + shape, lambda bi, ci: (bi,) + (0,) * len(shape))
    return pl.pallas_call(
        functools.partial(_mlstm_body, rows=rows),
        grid=(batch, steps),
        in_specs=[tok(2 * HB_W), tok(HB_W), tok(HB_W), tok(HB_W), tok(HB_W),
                  _const_spec(conv_w.shape), _const_spec(conv_b.shape),
                  per_b((CONV_PAD, 2 * HB_W)), per_b((N_HEADS // 2, HEAD_DIM, 2 * HEAD_DIM)), per_b((1, HB_W)),
                  per_b((1, HB_W)), _const_spec(tri.shape), _const_spec(eblk.shape)],
        out_specs=[tok(HB_W), per_b((N_HEADS // 2, HEAD_DIM, 2 * HEAD_DIM)), per_b((1, HB_W)), per_b((1, HB_W)),
                   per_b((CONV_PAD, 2 * HB_W))],
        out_shape=[jax.ShapeDtypeStruct((batch * t_len, HB_W), F32),
                   jax.ShapeDtypeStruct((batch, N_HEADS // 2, HEAD_DIM, 2 * HEAD_DIM), F32),
                   jax.ShapeDtypeStruct((batch, 1, HB_W), F32),
                   jax.ShapeDtypeStruct((batch, 1, HB_W), F32),
                   jax.ShapeDtypeStruct((batch, CONV_PAD, 2 * HB_W), F32)],
        scratch_shapes=[pltpu.VMEM((length + CONV_PAD, 2 * HB_W), F32),
                        pltpu.VMEM((HB_W, HB_W), F32),
                        pltpu.VMEM((1, HB_W), F32),
                        pltpu.VMEM((1, HB_W), F32)],
        compiler_params=_cparams(("parallel", "arbitrary")),
        name="mlstm",
    )(proj["aqk"], proj["av"], proj["ao"], proj["gi"], proj["gf"], conv_w, conv_b, conv0, c0_pk, n0, m0_hb,
      tri, eblk)


def _hgrn_body(c_ref, lb_ref, hng_ref, s0_ref, tri16_ref, eblk_ref, esel_ref, erep_ref, yc_ref, s1_ref, s_ref,
               *, rows, length, seqs):
    step = pl.program_id(1)
    L = length

    @pl.when(step == 0)
    def _():
        s_ref[...] = s0_ref[...]

    cin = _pad_rows(c_ref[...], L)
    q = cin[:, 0:HB_W]
    f_pre = cin[:, HB_W:2 * HB_W]
    v = cin[:, 2 * HB_W:3 * HB_W]
    gate = cin[:, 3 * HB_W:]
    lb = lb_ref[...]
    forget = lb + (1.0 - lb) * jax.nn.sigmoid(f_pre)
    logf = jnp.log(forget)
    key = 1.0 - forget
    if rows < L:
        live = lax.broadcasted_iota(jnp.int32, (L, 1), 0) < rows
        logf = jnp.where(live, logf, 0.0)
        key = jnp.where(live, key, 0.0)
    bl = _dot_exact_left(tri16_ref[...], logf) * LOG2E
    eblk = eblk_ref[...]
    groups = L // SUB
    t_loc = lax.broadcasted_iota(jnp.int32, (L, 1), 0) % SUB
    vb = v.astype(BF16)

    def from_source(x, s):
        picked = x.reshape(groups, SUB, HB_W)[:, s:s + 1, :]
        return jnp.broadcast_to(picked, (groups, SUB, HB_W)).reshape(L, HB_W)

    n_sub = max(rows // SUB, 1)
    qd = (q * jnp.exp2(bl)).astype(BF16)
    bl_t = bl.T
    low = lax.broadcasted_iota(jnp.int32, (1, 2 * HEAD_DIM), 1) < HEAD_DIM
    zero_b = jnp.zeros((HEAD_DIM, 2 * HEAD_DIM), BF16)

    def contribution(j):
        r0 = j * SUB
        b_last = bl[r0 + SUB - 1:r0 + SUB, :]
        kw_j = (key[r0:r0 + SUB] * jnp.exp2(b_last - bl[r0:r0 + SUB])).astype(BF16)
        upd = lax.dot_general(kw_j, vb[r0:r0 + SUB], (((0,), (0,)), ((), ())),
                              preferred_element_type=F32)
        dec = jnp.exp2(bl_t[:, r0 + SUB - 1:r0 + SUB])
        pair = []
        for i in range(2):
            rk = 2 * i * HEAD_DIM
            u = jnp.where(low, upd[rk:rk + HEAD_DIM, rk:rk + 2 * HEAD_DIM],
                          upd[rk + HEAD_DIM:rk + 2 * HEAD_DIM, rk:rk + 2 * HEAD_DIM])
            d = jnp.where(low, dec[rk:rk + HEAD_DIM], dec[rk + HEAD_DIM:rk + 2 * HEAD_DIM])
            pair.append((d, u))
        return pair

    per_seq = n_sub // seqs
    states = [[s_ref[q, 0], s_ref[q, 1]] for q in range(seqs)]
    o_parts = []

    def advance(j):
        r0 = j * SUB
        pk = states[j // per_seq]
        pb = [x.astype(BF16) for x in pk]
        s_bd = jnp.concatenate([
            jnp.concatenate([jnp.where(low, pb[0], 0.0).astype(BF16), zero_b], axis=1),
            jnp.concatenate([jnp.where(low, 0.0, pb[0]).astype(BF16), zero_b], axis=1),
            jnp.concatenate([zero_b, jnp.where(low, pb[1], 0.0).astype(BF16)], axis=1),
            jnp.concatenate([zero_b, jnp.where(low, 0.0, pb[1]).astype(BF16)], axis=1)], axis=0)
        o_parts.append(_dot(qd[r0:r0 + SUB], s_bd))
        pk[:] = [d * p + u for p, (d, u) in zip(pk, contrib[j])]

    contrib = []
    att = jnp.zeros((L, 2 * HEAD_DIM), F32)
    for s in range(SUB):
        w = jnp.exp2(bl - from_source(bl, s))
        p = jnp.where(t_loc >= s, q * from_source(key, s) * w, 0.0)
        att = att + _dot(p.astype(BF16), esel_ref[s])
        while len(contrib) < ((s + 1) * n_sub) // SUB:
            contrib.append(contribution(len(contrib)))
    for j in range(n_sub):
        advance(j)
    for q_i in range(seqs):
        s_ref[q_i, 0] = states[q_i][0]
        s_ref[q_i, 1] = states[q_i][1]
    attb = att.astype(BF16)
    same_sub = (lax.broadcasted_iota(jnp.int32, (L, L), 0) // SUB
                == lax.broadcasted_iota(jnp.int32, (L, L), 1) // SUB)
    lane_head = lax.broadcasted_iota(jnp.int32, (1, HB_W), 1) // HEAD_DIM
    o = jnp.zeros((L, HB_W), F32)
    for h in range(N_HEADS):
        full = jnp.where(same_sub, _dot(attb, erep_ref[h]), 0.0)
        o = jnp.where(lane_head == h, _dot(full.astype(BF16), vb), o)

    o_inter = jnp.concatenate(o_parts, axis=0)
    if rows < L:
        o_inter = _pad_rows(o_inter, L)
    o = o + o_inter

    ms = _dot_exact_right(o * o, eblk) * (1.0 / HEAD_DIM)
    yc = o * lax.rsqrt(ms + RMS_EPS) * hng_ref[...] * (gate * jax.nn.sigmoid(gate))
    yc_ref[...] = yc[:rows]

    @pl.when(step == pl.num_programs(1) - 1)
    def _():
        for q_i in range(seqs):
            s1_ref[q_i, 0] = states[q_i][0]
            s1_ref[q_i, 1] = states[q_i][1]


def _hgrn_rows(t_len, batch=1):
    if t_len < SCAN_ROWS and t_len % SUB == 0 and SCAN_ROWS % t_len == 0 and batch % (SCAN_ROWS // t_len) == 0:
        return SCAN_ROWS, SCAN_ROWS, SCAN_ROWS // t_len
    rows = _tile(t_len, HGRN_ROWS, SUB)
    return rows, max(rows, SCAN_ROWS), 1


def _hgrn(c_all, lb, hn_g, s0_pk, tri16, eblk, esel, erep, batch, t_len):
    rows, length, seqs = _hgrn_rows(t_len, batch)
    n = batch * t_len
    steps = n // (batch // seqs) // rows
    pk_spec = pl.BlockSpec((seqs, 2, HEAD_DIM, 2 * HEAD_DIM), lambda bi, ci: (bi, 0, 0, 0))
    return pl.pallas_call(
        functools.partial(_hgrn_body, rows=rows, length=length, seqs=seqs),
        grid=(batch // seqs, steps),
        in_specs=[pl.BlockSpec((rows, 4 * HB_W), lambda bi, ci: (bi * steps + ci, 0)),
                  _const_spec((1, HB_W)), _const_spec((1, HB_W)), pk_spec,
                  _const_spec(tri16.shape), _const_spec(eblk.shape), _const_spec(esel.shape),
                  _const_spec(erep.shape)],
        out_specs=[pl.BlockSpec((rows, HB_W), lambda bi, ci: (bi * steps + ci, 0)), pk_spec],
        out_shape=[jax.ShapeDtypeStruct((n, HB_W), F32),
                   jax.ShapeDtypeStruct((batch, 2, HEAD_DIM, 2 * HEAD_DIM), F32)],
        scratch_shapes=[pltpu.VMEM((seqs, 2, HEAD_DIM, 2 * HEAD_DIM), F32)],
        compiler_params=_cparams(("parallel", "arbitrary")),
        name="hgrn",
    )(c_all, lb, hn_g, s0_pk, tri16, eblk, esel, erep)


def _attn_prompt_body(qi_tab, ki_tab, lam_ref, q_ref, k_ref, vt_ref, g_ref, o_ref, qx_ref, m_ref, acc_ref,
                      s_ref, bmax_ref, *, tq, out_scale):
    pair = pl.program_id(2)
    qi = qi_tab[pair]
    ki = ki_tab[pair]
    ns = tq // Q_STRIP
    dv = 2 * HEAD_DIM

    @pl.when(ki == 0)
    def _():
        qt = (q_ref[...] * (HEAD_DIM ** -0.5 * LOG2E)).T
        first = lax.broadcasted_iota(jnp.int32, (dv, 1), 0) < HEAD_DIM
        qa = jnp.where(first, qt, 0.0).astype(BF16)
        qb = jnp.where(first, 0.0, qt).astype(BF16)
        for st in range(ns):
            qx_ref[0, st] = qa[:, st * Q_STRIP:(st + 1) * Q_STRIP]
            qx_ref[1, st] = qb[:, st * Q_STRIP:(st + 1) * Q_STRIP]
        m_ref[...] = jnp.full(m_ref.shape, NEG_BIG, F32)
        acc_ref[...] = jnp.zeros(acc_ref.shape, F32)

    def scores(kb, st, diag):
        k = k_ref[kb * Q_STRIP:(kb + 1) * Q_STRIP, :]
        for mp in range(2):
            s = _dot(k, qx_ref[mp, st])
            if diag and st == kb:
                k_chunk = lax.broadcasted_iota(jnp.int32, (Q_STRIP, 1), 0) // CHUNK
                q_chunk = lax.broadcasted_iota(jnp.int32, (1, Q_STRIP), 1) // CHUNK
                s = jnp.where(k_chunk <= q_chunk, s, NEG_BIG)
            s_ref[kb % 2, mp, st] = s
            bmax_ref[kb % 2, mp, st] = jnp.max(s, axis=0, keepdims=True)

    def accumulate(kb, st):
        vt = vt_ref[kb]
        for mp in range(2):
            m_old = m_ref[mp, st]
            m_new = jnp.maximum(m_old, bmax_ref[kb % 2, mp, st])
            p = jnp.exp2(s_ref[kb % 2, mp, st] - m_new).astype(BF16)
            corr = jnp.exp2(m_old - m_new)
            acc_ref[mp, st] = corr * acc_ref[mp, st] + _dot(vt, p)
            m_ref[mp, st] = m_new

    def key_tile(diag):
        first_strip = (lambda kb: kb) if diag else (lambda kb: 0)
        for st in range(ns):
            scores(0, st, diag)
        for kb in range(ns):
            for st in range(first_strip(kb), ns):
                accumulate(kb, st)
                if kb + 1 < ns and st >= first_strip(kb + 1):
                    scores(kb + 1, st, diag)

    @pl.when(ki < qi)
    def _():
        key_tile(False)

    @pl.when(ki == qi)
    def _():
        key_tile(True)
        lam = lam_ref[0, 0]
        for st in range(ns):
            a1 = acc_ref[0, st]
            a2 = acc_ref[1, st]
            o = a1[:dv] / a1[dv:dv + 1] - lam * (a2[:dv] / a2[dv:dv + 1])
            ms = jnp.mean(o * o, axis=0, keepdims=True)
            o = o * lax.rsqrt(ms + RMS_EPS) * g_ref[...] * out_scale
            o_ref[st * Q_STRIP:(st + 1) * Q_STRIP, :] = o.T


def _attn_prompt(lam, bq, bkb, vt, dn_g, batch, t_len, out_scale):
    tq = _tile(t_len, ATT_TILE, Q_STRIP)
    nq = t_len // tq
    ns = tq // Q_STRIP
    n = batch * t_len
    w = 2 * HEAD_DIM
    pairs = [(qi, ki) for qi in range(nq) for ki in range(qi + 1)]
    qi_tab = jnp.asarray([p[0] for p in pairs], jnp.int32)
    ki_tab = jnp.asarray([p[1] for p in pairs], jnp.int32)
    qmap = lambda b, h, p, qt, kt: (b * nq + qt[p], h)
    kmap = lambda b, h, p, qt, kt: (b * nq + kt[p], h)
    vmap_ = lambda b, h, p, qt, kt: (h, b * nq + kt[p], 0, 0)
    grid_spec = pltpu.PrefetchScalarGridSpec(
        num_scalar_prefetch=2,
        grid=(batch, N_HEADS, len(pairs)),
        in_specs=[pl.BlockSpec(memory_space=pltpu.SMEM),
                  pl.BlockSpec((tq, w), qmap), pl.BlockSpec((tq, w), kmap),
                  pl.BlockSpec((None, ns, VT_ROWS, Q_STRIP), vmap_),
                  pl.BlockSpec((w, 1), lambda b, h, p, qt, kt: (0, 0))],
        out_specs=pl.BlockSpec((tq, w), qmap),
        scratch_shapes=[pltpu.VMEM((2, ns, w, Q_STRIP), BF16),
                        pltpu.VMEM((2, ns, 1, Q_STRIP), F32),
                        pltpu.VMEM((2, ns, VT_ROWS, Q_STRIP), F32),
                        pltpu.VMEM((2, 2, ns, Q_STRIP, Q_STRIP), F32),
                        pltpu.VMEM((2, 2, ns, 1, Q_STRIP), F32)])
    return pl.pallas_call(
        functools.partial(_attn_prompt_body, tq=tq, out_scale=out_scale),
        grid_spec=grid_spec,
        out_shape=jax.ShapeDtypeStruct((n, ATT_W), F32),
        compiler_params=_cparams(("parallel", "parallel", "arbitrary")),
        name="attn_prompt",
    )(qi_tab, ki_tab, lam, bq, bkb, vt, dn_g.reshape(w, 1))


def _attn_sample_body(lam_ref, q_ref, kn_ref, vn_ref, kp_ref, vp_ref, g_ref, o_ref, *, t_new, past, out_scale):
    w = 2 * HEAD_DIM
    first = lax.broadcasted_iota(jnp.int32, (1, w), 1) < HEAD_DIM
    for h in range(N_HEADS):
        cols = slice(h * w, (h + 1) * w)
        q = q_ref[:, cols] * (HEAD_DIM ** -0.5)
        q2 = jnp.concatenate([jnp.where(first, q, 0.0), jnp.where(first, 0.0, q)], axis=0).astype(BF16)
        kp = kp_ref[pl.ds(h, past, stride=N_HEADS), :].astype(BF16)
        vp = vp_ref[pl.ds(h, past, stride=N_HEADS), :].astype(BF16)
        s_p = _dot_nt(q2, kp)
        kn = kn_ref[pl.ds(h, t_new, stride=N_HEADS), :].astype(BF16)
        vn = vn_ref[pl.ds(h, t_new, stride=N_HEADS), :].astype(BF16)
        s_n = _dot_nt(q2, kn)
        m = jnp.maximum(jnp.max(s_p, axis=1, keepdims=True), jnp.max(s_n, axis=1, keepdims=True))
        p_p = jnp.exp(s_p - m)
        p_n = jnp.exp(s_n - m)
        l = jnp.sum(p_p, axis=1, keepdims=True) + jnp.sum(p_n, axis=1, keepdims=True)
        acc = _dot(p_p.astype(BF16), vp) + _dot(p_n.astype(BF16), vn)
        o2 = acc / l
        o = o2[:t_new] - lam_ref[0, 0] * o2[t_new:]
        ms = jnp.mean(o * o, axis=1, keepdims=True)
        o_ref[:, cols] = o * lax.rsqrt(ms + RMS_EPS) * g_ref[...] * out_scale


def _attn_sample(lam, bq, bk, bv, cache_k, cache_v, layer, dn_g, batch, t_new, out_scale):
    w = 2 * HEAD_DIM
    depth, _, past = cache_k.shape[:3]
    rows = past * N_HEADS
    new = pl.BlockSpec((t_new, ATT_W), lambda b: (b, 0))
    new_kv = pl.BlockSpec((None, t_new * N_HEADS, w), lambda b: (layer, b, 0))
    old = pl.BlockSpec((None, None, rows, w), lambda b: (layer, b, 0, 0))
    return pl.pallas_call(
        functools.partial(_attn_sample_body, t_new=t_new, past=past, out_scale=out_scale),
        grid=(batch,),
        in_specs=[pl.BlockSpec(memory_space=pltpu.SMEM), new, new_kv, new_kv, old, old, _const_spec((1, w))],
        out_specs=new,
        out_shape=jax.ShapeDtypeStruct((batch * t_new, ATT_W), F32),
        compiler_params=_cparams(("parallel",)),
        name="attn_sample",
    )(lam, bq, bk, bv, cache_k.reshape(depth, batch, rows, w), cache_v.reshape(depth, batch, rows, w),
      dn_g.reshape(1, w))


def _to_head_pairs(s):
    b = s.shape[0]
    r = s.reshape(b, N_HEADS // 2, 2, HEAD_DIM, HEAD_DIM)
    return jnp.swapaxes(r, 2, 3).reshape(b, N_HEADS // 2, HEAD_DIM, 2 * HEAD_DIM)


def _from_head_pairs(p):
    b = p.shape[0]
    r = p.reshape(b, N_HEADS // 2, HEAD_DIM, 2, HEAD_DIM)
    return jnp.swapaxes(r, 2, 3).reshape(b, N_HEADS, HEAD_DIM, HEAD_DIM)


def _rearrange_proj(a):
    gates = lambda c0: jnp.repeat(a[..., c0:c0 + N_HEADS], HEAD_DIM, axis=-1)
    out = jnp.concatenate([a[..., A_QK0:A_I0], a[..., A_O0:B_Q0], gates(A_I0), gates(A_F0), a[..., B_Q0:N_IN]],
                          axis=-1)
    assert out.shape[-1] == N_CAT
    return out


def _consts(hgrn_len, mlstm_len):
    r = jnp.arange(mlstm_len)
    tri = (r[None, :] <= r[:, None]).astype(BF16)
    r = jnp.arange(hgrn_len)
    tri16 = ((r[None, :] <= r[:, None]) & (r[None, :] // SUB == r[:, None] // SUB)).astype(BF16)
    hh = jnp.arange(HB_W) // HEAD_DIM
    eblk = (hh[None, :] == hh[:, None]).astype(BF16)
    lane = jnp.arange(2 * HEAD_DIM)
    src = jnp.arange(SUB)
    esel = ((hh[None, :, None] * SUB + src[:, None, None]) == lane[None, None, :]).astype(BF16)
    erep = ((lane[None, :, None] // SUB == jnp.arange(N_HEADS)[:, None, None])
            & (lane[None, :, None] % SUB == r[None, None, :] % SUB)).astype(BF16)
    return tri, tri16, eblk, esel, erep


def _group(x, depth, layer_w, states, cache, batch, t_len, alpha, consts):
    tri, tri16, eblk, esel, erep = consts
    prompt = cache is None
    outs = []
    kv_bufs = tuple(jnp.zeros((depth, batch * t_len * N_HEADS, 2 * HEAD_DIM), F32) for _ in range(2))
    for l in range(depth):
        w = layer_w[l]
        conv0, c0, n0, m0, s0 = states(l)
        x = _ffn_ln(x, w["f_in0"], w["f_out0"], w["ln_g"][0], w["ln_b"][0], alpha)
        res = _in_proj(x, w["w_cat"], w["b_cat"], prompt, l, depth, kv_bufs)
        proj = dict(zip(_PROJ_OUTS, res[:len(_PROJ_OUTS)]))
        bkb = res[len(_PROJ_OUTS)]
        kv_bufs = (proj["bk"], proj["bv"])

        conv0_p = jnp.pad(conv0, ((0, 0), (CONV_PAD - (CONV_W - 1), 0), (0, 0)))
        ya, c1, n1, m1, convn = _mlstm(proj, w["conv_w"], w["conv_b"], conv0_p, _to_head_pairs(c0),
                                       n0.reshape(batch, 1, HB_W),
                                       jnp.repeat(m0, HEAD_DIM, axis=-1).reshape(batch, 1, HB_W),
                                       tri, eblk, batch, t_len)
        lam_init = 0.8 - 0.6 * math.exp(-0.3 * l)
        if prompt:
            yb = _attn_prompt(w["lam"], proj["bq"], bkb, res[len(_PROJ_OUTS) + 1], w["dn_g"], batch, t_len,
                              1.0 - lam_init)
        else:
            yb = _attn_sample(w["lam"], proj["bq"], proj["bk"], proj["bv"], cache[0], cache[1], l, w["dn_g"],
                              batch, t_len, 1.0 - lam_init)
        yc, s1 = _hgrn(proj["c"], w["lb"], w["hn_g"], _to_head_pairs(s0), tri16, eblk, esel, erep, batch, t_len)
        x = _out_proj_ffn(ya, yb, yc, x, w["w_out"], w["ln_g"][1], w["ln_b"][1],
                          w["f_in1"], w["f_out1"], w["ln_g"][2], w["ln_b"][2], alpha)
        outs.append((_from_head_pairs(c1),
                     n1.reshape(batch, N_HEADS, HEAD_DIM),
                     m1.reshape(batch, N_HEADS, HEAD_DIM)[:, :, 0],
                     convn[:, CONV_PAD - (CONV_W - 1):, :],
                     _from_head_pairs(s1)))
    kv = tuple(buf.reshape(depth, batch, t_len, N_HEADS, 2 * HEAD_DIM) for buf in kv_bufs)
    return x, kv + tuple(jnp.stack(a) for a in zip(*outs))


def kernel(x_prompt, x_sample, cache_diff_k, cache_diff_v, state_mlstm_c, state_mlstm_n, state_mlstm_m,
           state_mlstm_conv, state_hgrn_s, w_in, b_in, w_out, mlstm_conv_w, mlstm_conv_b, diff_lambda,
           diff_norm_g, hgrn_lb_logits, hgrn_norm_g, ffn_w_in, ffn_w_out, ln_g, ln_b):
    depth = w_in.shape[0]
    alpha = (2.0 * depth) ** 0.25
    batch, seq, d_model = x_prompt.shape
    dec_batch, dec_seq, _ = x_sample.shape
    past = cache_diff_k.shape[2]

    p_lb = jax.nn.softmax(hgrn_lb_logits.astype(F32), axis=0)
    lower_bounds = jnp.cumsum(p_lb, axis=0) - p_lb[0]
    w_cat = _rearrange_proj(w_in.astype(BF16))
    b_cat = _rearrange_proj(b_in).reshape(depth, 1, N_CAT)
    lp = diff_lambda.astype(F32)
    lam_init = jnp.asarray([0.8 - 0.6 * math.exp(-0.3 * l) for l in range(depth)], F32)
    lam = (jnp.exp(jnp.sum(lp[:, 0] * lp[:, 1], axis=-1)) - jnp.exp(jnp.sum(lp[:, 2] * lp[:, 3], axis=-1))
           + lam_init)
    f_in = ffn_w_in.astype(BF16)
    f_out = ffn_w_out.astype(BF16)
    w_out_b = w_out.astype(BF16)
    layer_w = [dict(w_cat=w_cat[l], b_cat=b_cat[l],
                    conv_w=mlstm_conv_w[l], conv_b=mlstm_conv_b[l].reshape(1, 2 * HB_W),
                    lam=lam[l].reshape(1, 1), dn_g=diff_norm_g[l],
                    lb=lower_bounds[l].reshape(1, HB_W),
                    hn_g=jnp.tile(hgrn_norm_g[l], N_HEADS).reshape(1, HB_W),
                    w_out=w_out_b[l], f_in0=f_in[l, 0], f_out0=f_out[l, 0], f_in1=f_in[l, 1], f_out1=f_out[l, 1],
                    ln_g=ln_g[l], ln_b=ln_b[l]) for l in range(depth)]
    def zero_states(_):
        return (jnp.zeros((batch, CONV_W - 1, 2 * HB_W), F32), jnp.zeros((batch, N_HEADS, HEAD_DIM, HEAD_DIM), F32),
                jnp.zeros((batch, N_HEADS, HEAD_DIM), F32), jnp.zeros((batch, N_HEADS), F32),
                jnp.zeros((batch, N_HEADS, HEAD_DIM, HEAD_DIM), F32))

    def carried_states(l):
        return (state_mlstm_conv[l], state_mlstm_c[l], state_mlstm_n[l], state_mlstm_m[l], state_hgrn_s[l])

    y_p, p_out = _group(x_prompt.reshape(batch * seq, d_model), depth, layer_w, zero_states, None,
                        batch, seq, alpha, _consts(_hgrn_rows(seq)[1], _mlstm_rows(seq)[1]))
    cache = (cache_diff_k, cache_diff_v)
    y_s, s_out = _group(x_sample.reshape(dec_batch * dec_seq, d_model), depth, layer_w, carried_states, cache,
                        dec_batch, dec_seq, alpha, _consts(_hgrn_rows(dec_seq)[1], _mlstm_rows(dec_seq)[1]))
    return (y_p.reshape(batch, seq, d_model), y_s.reshape(dec_batch, dec_seq, d_model)) + p_out + s_out
```

```python
import functools
import math

import jax
import jax.numpy as jnp
from jax import lax
from jax.experimental import pallas as pl
from jax.experimental.pallas import tpu as pltpu

F32 = jnp.float32
BF16 = jnp.bfloat16

HEAD_DIM = 64
N_HEADS = 4
CONV_W = 4
CONV_PAD = 8
CHUNK = 64
HB_W = N_HEADS * HEAD_DIM
ATT_W = N_HEADS * 2 * HEAD_DIM
LN_EPS = 1e-5
RMS_EPS = 1e-6
NEG_BIG = -1e30
SCAN_ROWS = 128
MLSTM_ROWS = 128
HGRN_ROWS = 256
SUB = 16
LOG2E = 1.4426950408889634
VT_ROWS = 2 * HEAD_DIM + 16
Q_STRIP = 256
ATT_TILE = 2048
FFN_CHUNK = 256
FFN_TILE = 1024
FFN_ROWS = 512
V7X_VMEM_LIMIT = 56 * 1024 * 1024

A_QK0 = 0
A_V0 = A_QK0 + 2 * HB_W
A_I0 = A_V0 + HB_W
A_F0 = A_I0 + N_HEADS
A_O0 = A_F0 + N_HEADS
B_Q0 = A_O0 + HB_W
B_K0 = B_Q0 + ATT_W
B_V0 = B_K0 + ATT_W
C_Q0 = B_V0 + ATT_W
N_IN = C_Q0 + 4 * HB_W

_SEGS = {}
_off = 0
for _name, _w in (("aqk", 2 * HB_W), ("av", HB_W), ("ao", HB_W), ("gi", HB_W), ("gf", HB_W),
                  ("bq", ATT_W), ("bk", ATT_W), ("bv", ATT_W), ("c", 4 * HB_W)):
    _SEGS[_name] = (_off, _w)
    _off += _w
N_CAT = _off


def _tile(n, pref, mult=8):
    t = min(pref, n)
    while t > mult and (n % t or t % mult):
        t -= mult
    assert n % t == 0 and t % mult == 0, (n, pref, mult)
    return t


def _cparams(sem):
    return pltpu.CompilerParams(dimension_semantics=sem, vmem_limit_bytes=V7X_VMEM_LIMIT)


def _const_spec(shape, single=False):
    nd = len(shape)
    if single:
        return pl.BlockSpec(shape, lambda *_: (0,) * nd, pipeline_mode=pl.Buffered(1))
    return pl.BlockSpec(shape, lambda *_: (0,) * nd)


def _dot(a, b):
    return jnp.dot(a, b, preferred_element_type=F32)


def _dot_nt(a, b):
    return lax.dot_general(a, b, (((1,), (1,)), ((), ())), preferred_element_type=F32)


def _split3(x):
    a = x.astype(BF16)
    r = x - a.astype(F32)
    b = r.astype(BF16)
    c = (r - b.astype(F32)).astype(BF16)
    return a, b, c


def _dot_exact_left(mat01, x):
    a, b, c = _split3(x)
    return _dot(mat01, a) + _dot(mat01, b) + _dot(mat01, c)


def _dot_exact_right(x, mat01):
    a, b, c = _split3(x)
    return _dot(a, mat01) + _dot(b, mat01) + _dot(c, mat01)


def _layer_norm(y, g, b):
    mu = jnp.mean(y, axis=-1, keepdims=True)
    d = y - mu
    var = jnp.mean(d * d, axis=-1, keepdims=True)
    return d * lax.rsqrt(var + LN_EPS) * g + b


def _pad_rows(x, rows):
    if x.shape[0] == rows:
        return x
    return jnp.concatenate([x, jnp.zeros((rows - x.shape[0],) + x.shape[1:], x.dtype)], axis=0)


def _swiglu_ln(x, wi_ref, wo_ref, g_ref, b_ref, o_ref, acc_ref, alpha, d_ff, tf):
    rb = min(FFN_ROWS, x.shape[0])
    n_blocks = x.shape[0] // rb

    def finish(r):
        rows = slice(r * rb, (r + 1) * rb)
        y = alpha * x[rows] + 0.5 * acc_ref[rows, :]
        o_ref[rows, :] = _layer_norm(y, g_ref[...], b_ref[...])

    for r in range(n_blocks):
        rows = slice(r * rb, (r + 1) * rb)
        xb = x[rows].astype(BF16)
        for c in range(d_ff // tf):
            gate = _dot(xb, wi_ref[:, c * tf:(c + 1) * tf])
            up = _dot(xb, wi_ref[:, d_ff + c * tf:d_ff + (c + 1) * tf])
            h = (gate * jax.nn.sigmoid(gate) * up).astype(BF16)
            part = _dot(h, wo_ref[c * tf:(c + 1) * tf, :])
            if c == 0:
                acc_ref[rows, :] = part
            else:
                acc_ref[rows, :] += part
            if c == 0 and r > 0:
                finish(r - 1)
    finish(n_blocks - 1)


def _ffn_ln_body(x_ref, wi_ref, wo_ref, g_ref, b_ref, o_ref, acc_ref, *, alpha, d_ff, tf):
    _swiglu_ln(x_ref[...], wi_ref, wo_ref, g_ref, b_ref, o_ref, acc_ref, alpha, d_ff, tf)


def _ffn_ln(x, wi, wo, g, b, alpha):
    n, d = x.shape
    d_ff = wo.shape[0]
    tm = _tile(n, FFN_TILE)
    tf = _tile(d_ff, FFN_CHUNK, 128)
    return pl.pallas_call(
        functools.partial(_ffn_ln_body, alpha=alpha, d_ff=d_ff, tf=tf),
        grid=(n // tm,),
        in_specs=[pl.BlockSpec((tm, d), lambda i: (i, 0)),
                  _const_spec(wi.shape, single=True), _const_spec(wo.shape, single=True),
                  _const_spec((1, d)), _const_spec((1, d))],
        out_specs=pl.BlockSpec((tm, d), lambda i: (i, 0)),
        out_shape=jax.ShapeDtypeStruct((n, d), F32),
        scratch_shapes=[pltpu.VMEM((tm, d), F32)],
        compiler_params=_cparams(("parallel",)),
        name="ffn_ln",
    )(x, wi, wo, g.reshape(1, d), b.reshape(1, d))


_PROJ_OUTS = ("aqk", "av", "ao", "gi", "gf", "bq", "bk", "bv", "c")


def _in_proj_body(*refs, emit_vt, n_alias):
    x_ref, w_ref, bias_ref = refs[:3]
    out_refs = refs[3 + n_alias:]
    xb = x_ref[...].astype(BF16)
    tm = xb.shape[0]
    outs = dict(zip(_PROJ_OUTS, out_refs))
    dv = 2 * HEAD_DIM
    ones_rows = jnp.where(lax.broadcasted_iota(jnp.int32, (VT_ROWS - dv, Q_STRIP), 0) == 0, 1.0, 0.0).astype(BF16)
    for name in _PROJ_OUTS:
        c0, w = _SEGS[name]
        for s in range(0, w, 2 * HB_W):
            e = min(s + 2 * HB_W, w)
            val = _dot(xb, w_ref[:, c0 + s:c0 + e]) + bias_ref[:, c0 + s:c0 + e]
            if name in ("bk", "bv"):
                for h in range(N_HEADS):
                    outs[name][pl.ds(h, tm, stride=N_HEADS), :] = val[:, h * 2 * HEAD_DIM:(h + 1) * 2 * HEAD_DIM]
            else:
                outs[name][:, s:e] = val
            if name == "bk":
                out_refs[len(_PROJ_OUTS)][:, s:e] = val.astype(BF16)
            if name == "bv" and emit_vt:
                vt_ref = out_refs[len(_PROJ_OUTS) + 1]
                for h in range(N_HEADS):
                    vt = val[:, h * dv:(h + 1) * dv].T.astype(BF16)
                    for j in range(tm // Q_STRIP):
                        vt_ref[h, j, 0:dv, :] = vt[:, j * Q_STRIP:(j + 1) * Q_STRIP]
                        vt_ref[h, j, dv:VT_ROWS, :] = ones_rows


def _in_proj(x, w_cat, b_cat, emit_vt, layer, depth, kv_bufs):
    n, d = x.shape
    tm = _tile(n, 512, Q_STRIP if emit_vt else 8)
    out_shape, out_specs = [], []
    for k in _PROJ_OUTS:
        if k in ("bk", "bv"):
            out_shape.append(jax.ShapeDtypeStruct((depth, n * N_HEADS, 2 * HEAD_DIM), F32))
            out_specs.append(pl.BlockSpec((None, tm * N_HEADS, 2 * HEAD_DIM), lambda i: (layer, i, 0)))
        else:
            out_shape.append(jax.ShapeDtypeStruct((n, _SEGS[k][1]), F32))
            out_specs.append(pl.BlockSpec((tm, _SEGS[k][1]), lambda i: (i, 0)))
    out_shape.append(jax.ShapeDtypeStruct((n, ATT_W), BF16))
    out_specs.append(pl.BlockSpec((tm, ATT_W), lambda i: (i, 0)))
    if emit_vt:
        out_shape.append(jax.ShapeDtypeStruct((N_HEADS, n // Q_STRIP, VT_ROWS, Q_STRIP), BF16))
        out_specs.append(pl.BlockSpec((N_HEADS, tm // Q_STRIP, VT_ROWS, Q_STRIP), lambda i: (0, i, 0, 0)))
    in_specs = [pl.BlockSpec((tm, d), lambda i: (i, 0)), _const_spec(w_cat.shape), _const_spec(b_cat.shape)]
    args = [x, w_cat, b_cat]
    aliases = {}
    for buf, name in zip(kv_bufs, ("bk", "bv")):
        aliases[len(args)] = _PROJ_OUTS.index(name)
        in_specs.append(pl.BlockSpec(memory_space=pl.ANY))
        args.append(buf)
    return pl.pallas_call(
        functools.partial(_in_proj_body, emit_vt=emit_vt, n_alias=len(aliases)),
        grid=(n // tm,),
        in_specs=in_specs,
        out_specs=out_specs,
        out_shape=out_shape,
        input_output_aliases=aliases,
        compiler_params=_cparams(("parallel",)),
        name="in_proj",
    )(*args)


def _out_proj_ffn_body(ya_ref, yb_ref, yc_ref, x_ref, w_ref, g1_ref, b1_ref, wi_ref, wo_ref, g2_ref, b2_ref,
                       o_ref, acc_ref, *, alpha, d_ff, tf):
    y = _dot(ya_ref[...].astype(BF16), w_ref[0:HB_W, :])
    y += _dot(yb_ref[...].astype(BF16), w_ref[HB_W:HB_W + ATT_W, :])
    y += _dot(yc_ref[...].astype(BF16), w_ref[HB_W + ATT_W:, :])
    x = _layer_norm(alpha * x_ref[...] + y, g1_ref[...], b1_ref[...])
    _swiglu_ln(x, wi_ref, wo_ref, g2_ref, b2_ref, o_ref, acc_ref, alpha, d_ff, tf)


def _out_proj_ffn(ya, yb, yc, x, w, g1, b1, wi, wo, g2, b2, alpha):
    n, d = x.shape
    d_ff = wo.shape[0]
    tm = _tile(n, FFN_TILE)
    tf = _tile(d_ff, FFN_CHUNK, 128)
    row = lambda width: pl.BlockSpec((tm, width), lambda i: (i, 0))
    vec = _const_spec((1, d))
    return pl.pallas_call(
        functools.partial(_out_proj_ffn_body, alpha=alpha, d_ff=d_ff, tf=tf),
        grid=(n // tm,),
        in_specs=[row(HB_W), row(ATT_W), row(HB_W), row(d), _const_spec(w.shape, single=True), vec, vec,
                  _const_spec(wi.shape, single=True), _const_spec(wo.shape, single=True), vec, vec],
        out_specs=row(d),
        out_shape=jax.ShapeDtypeStruct((n, d), F32),
        scratch_shapes=[pltpu.VMEM((tm, d), F32)],
        compiler_params=_cparams(("parallel",)),
        name="out_proj_ffn",
    )(ya, yb, yc, x, w, g1.reshape(1, d), b1.reshape(1, d), wi, wo, g2.reshape(1, d), b2.reshape(1, d))


def _mlstm_body(qk_ref, v_ref, o_ref, gi_ref, gf_ref, cw_ref, cb_ref, conv0_ref, c0_ref, n0_ref, m0_ref,
                tri_ref, eblk_ref,
                ya_ref, c1_ref, n1_ref, m1_ref, convn_ref,
                up_ref, c_ref, n_ref, m_ref, *, rows):
    step = pl.program_id(1)
    low = lax.broadcasted_iota(jnp.int32, (1, 2 * HEAD_DIM), 1) < HEAD_DIM
    pw = 2 * HEAD_DIM
    L = tri_ref.shape[0]

    @pl.when(step == 0)
    def _():
        up_ref[0:CONV_PAD, :] = conv0_ref[...]
        c_ref[...] = jnp.zeros(c_ref.shape, F32)
        for i in range(N_HEADS // 2):
            pair = c0_ref[i]
            c_ref[i * pw:i * pw + HEAD_DIM, i * pw:(i + 1) * pw] = jnp.where(low, pair, 0.0)
            c_ref[i * pw + HEAD_DIM:(i + 1) * pw, i * pw:(i + 1) * pw] = jnp.where(low, 0.0, pair)
        n_ref[...] = n0_ref[...]
        m_ref[...] = m0_ref[...]

    gi = _pad_rows(gi_ref[...], L)
    flog = jax.nn.log_sigmoid(_pad_rows(gf_ref[...], L))
    if rows < L:
        live = lax.broadcasted_iota(jnp.int32, (L, 1), 0) < rows
        gi = jnp.where(live, gi, NEG_BIG)
        flog = jnp.where(live, flog, 0.0)
    b = _dot_exact_left(tri_ref[...], flog)
    b_t = b.T
    gi_t = gi.T

    u = _pad_rows(qk_ref[...], L)
    up_ref[CONV_PAD:CONV_PAD + L, :] = u
    y = cb_ref[...]
    first = CONV_PAD - (CONV_W - 1)
    for j in range(CONV_W):
        y = y + cw_ref[j:j + 1, :] * up_ref[first + j:first + j + L, :]
    convn_ref[...] = up_ref[rows:rows + CONV_PAD, :]
    up_ref[0:CONV_PAD, :] = up_ref[L:L + CONV_PAD, :]
    qk = y * jax.nn.sigmoid(y)
    q = qk[:, :HB_W] * (HEAD_DIM ** -0.5)
    k = qk[:, HB_W:]
    v = _pad_rows(v_ref[...], L)
    kb = k.astype(BF16)
    vb = v.astype(BF16)
    qb = q.astype(BF16)

    lane_head = lax.broadcasted_iota(jnp.int32, (1, HB_W), 1) // HEAD_DIM
    qk_scores = [_dot_nt(jnp.where(lane_head == h, q, 0.0).astype(BF16), kb) for h in range(N_HEADS)]
    c_prev = c_ref[...]
    n_prev = n_ref[...]
    eblk = eblk_ref[...]
    num_state = _dot(qb, c_prev.astype(BF16))
    den_state = _dot((q * n_prev).astype(BF16), eblk)

    m_prev = m_ref[...]
    g = b + m_prev

    t_idx = lax.broadcasted_iota(jnp.int32, (L, L), 0)
    s_idx = lax.broadcasted_iota(jnp.int32, (L, L), 1)
    causal = s_idx <= t_idx

    mrow_hb = jnp.zeros((L, HB_W), F32)
    wg_hb = jnp.zeros((L, HB_W), F32)
    den_hb = jnp.zeros((L, HB_W), F32)
    num_hb = jnp.zeros((L, HB_W), F32)
    for h in range(N_HEADS):
        sel = lane_head == h
        c0 = h * HEAD_DIM
        dmat = b[:, c0:c0 + 1] - b_t[c0:c0 + 1, :] + gi_t[c0:c0 + 1, :]
        dmat = jnp.where(causal, dmat, NEG_BIG)
        gcol = g[:, c0:c0 + 1]
        mrow = jnp.maximum(gcol, jnp.max(dmat, axis=1, keepdims=True))
        wd = jnp.exp(dmat - mrow)
        wg = jnp.exp(gcol - mrow)
        qkw = qk_scores[h] * wd
        den = jnp.sum(qkw, axis=1, keepdims=True)
        num = _dot(qkw.astype(BF16), vb)
        mrow_hb = jnp.where(sel, mrow, mrow_hb)
        wg_hb = jnp.where(sel, wg, wg_hb)
        den_hb = jnp.where(sel, den, den_hb)
        num_hb = jnp.where(sel, num, num_hb)

    num_hb = wg_hb * num_state + num_hb
    den_hb = wg_hb * den_state + den_hb
    hout = num_hb / jnp.maximum(jnp.abs(den_hb), jnp.exp(-mrow_hb))
    ya = jax.nn.sigmoid(_pad_rows(o_ref[...], L)) * hout
    ya_ref[...] = ya[:rows]

    m_new = mrow_hb[L - 1:L, :]
    b_last = b[L - 1:L, :]
    decay = jnp.exp(b_last + m_prev - m_new)
    ws = jnp.exp(b_last - b + gi - m_new)
    kw = k * ws
    upd = _dot(kw.T.astype(BF16), vb) * eblk.astype(F32)
    c_new = decay * c_prev + upd
    n_new = decay * n_prev + jnp.sum(kw, axis=0, keepdims=True)
    c_ref[...] = c_new
    n_ref[...] = n_new
    m_ref[...] = m_new

    @pl.when(step == pl.num_programs(1) - 1)
    def _():
        for i in range(N_HEADS // 2):
            c1_ref[i] = jnp.where(low, c_new[i * pw:i * pw + HEAD_DIM, i * pw:(i + 1) * pw],
                                  c_new[i * pw + HEAD_DIM:(i + 1) * pw, i * pw:(i + 1) * pw])
        n1_ref[...] = n_new
        m1_ref[...] = m_new


def _mlstm_rows(t_len):
    rows = _tile(t_len, MLSTM_ROWS, 8)
    return rows, max(rows, SCAN_ROWS)


def _mlstm(proj, conv_w, conv_b, conv0, c0_pk, n0, m0_hb, tri, eblk, batch, t_len):
    rows, length = _mlstm_rows(t_len)
    assert tri.shape == (length, length)
    steps = t_len // rows
    tok = lambda width: pl.BlockSpec((rows, width), lambda bi, ci: (bi * steps + ci, 0))
    per_b = lambda shape: pl.BlockSpec((None,) + shape, lambda bi, ci: (bi,) + (0,) * len(shape))
    return pl.pallas_call(
        functools.partial(_mlstm_body, rows=rows),
        grid=(batch, steps),
        in_specs=[tok(2 * HB_W), tok(HB_W), tok(HB_W), tok(HB_W), tok(HB_W),
                  _const_spec(conv_w.shape), _const_spec(conv_b.shape),
                  per_b((CONV_PAD, 2 * HB_W)), per_b((N_HEADS // 2, HEAD_DIM, 2 * HEAD_DIM)), per_b((1, HB_W)),
                  per_b((1, HB_W)), _const_spec(tri.shape), _const_spec(eblk.shape)],
        out_specs=[tok(HB_W), per_b((N_HEADS // 2, HEAD_DIM, 2 * HEAD_DIM)), per_b((1, HB_W)), per_b((1, HB_W)),
                   per_b((CONV_PAD, 2 * HB_W))],
        out_shape=[jax.ShapeDtypeStruct((batch * t_len, HB_W), F32),
                   jax.ShapeDtypeStruct((batch, N_HEADS // 2, HEAD_DIM, 2 * HEAD_DIM), F32),
                   jax.ShapeDtypeStruct((batch, 1, HB_W), F32),
                   jax.ShapeDtypeStruct((batch, 1, HB_W), F32),
                   jax.ShapeDtypeStruct((batch, CONV_PAD, 2 * HB_W), F32)],
        scratch_shapes=[pltpu.VMEM((length + CONV_PAD, 2 * HB_W), F32),
                        pltpu.VMEM((HB_W, HB_W), F32),
                        pltpu.VMEM((1, HB_W), F32),
                        pltpu.VMEM((1, HB_W), F32)],
        compiler_params=_cparams(("parallel", "arbitrary")),
        name="mlstm",
    )(proj["aqk"], proj["av"], proj["ao"], proj["gi"], proj["gf"], conv_w, conv_b, conv0, c0_pk, n0, m0_hb,
      tri, eblk)


def _hgrn_body(c_ref, lb_ref, hng_ref, s0_ref, tri16_ref, eblk_ref, esel_ref, erep_ref, yc_ref, s1_ref, s_ref,
               *, rows, length, seqs):
    step = pl.program_id(1)
    L = length

    @pl.when(step == 0)
    def _():
        s_ref[...] = s0_ref[...]

    cin = _pad_rows(c_ref[...], L)
    q = cin[:, 0:HB_W]
    f_pre = cin[:, HB_W:2 * HB_W]
    v = cin[:, 2 * HB_W:3 * HB_W]
    gate = cin[:, 3 * HB_W:]
    lb = lb_ref[...]
    forget = lb + (1.0 - lb) * jax.nn.sigmoid(f_pre)
    logf = jnp.log(forget)
    key = 1.0 - forget
    if rows < L:
        live = lax.broadcasted_iota(jnp.int32, (L, 1), 0) < rows
        logf = jnp.where(live, logf, 0.0)
        key = jnp.where(live, key, 0.0)
    bl = _dot_exact_left(tri16_ref[...], logf) * LOG2E
    eblk = eblk_ref[...]
    groups = L // SUB
    t_loc = lax.broadcasted_iota(jnp.int32, (L, 1), 0) % SUB
    vb = v.astype(BF16)

    def from_source(x, s):
        picked = x.reshape(groups, SUB, HB_W)[:, s:s + 1, :]
        return jnp.broadcast_to(picked, (groups, SUB, HB_W)).reshape(L, HB_W)

    n_sub = max(rows // SUB, 1)
    qd = (q * jnp.exp2(bl)).astype(BF16)
    bl_t = bl.T
    low = lax.broadcasted_iota(jnp.int32, (1, 2 * HEAD_DIM), 1) < HEAD_DIM
    zero_b = jnp.zeros((HEAD_DIM, 2 * HEAD_DIM), BF16)

    def contribution(j):
        r0 = j * SUB
        b_last = bl[r0 + SUB - 1:r0 + SUB, :]
        kw_j = (key[r0:r0 + SUB] * jnp.exp2(b_last - bl[r0:r0 + SUB])).astype(BF16)
        upd = lax.dot_general(kw_j, vb[r0:r0 + SUB], (((0,), (0,)), ((), ())),
                              preferred_element_type=F32)
        dec = jnp.exp2(bl_t[:, r0 + SUB - 1:r0 + SUB])
        pair = []
        for i in range(2):
            rk = 2 * i * HEAD_DIM
            u = jnp.where(low, upd[rk:rk + HEAD_DIM, rk:rk + 2 * HEAD_DIM],
                          upd[rk + HEAD_DIM:rk + 2 * HEAD_DIM, rk:rk + 2 * HEAD_DIM])
            d = jnp.where(low, dec[rk:rk + HEAD_DIM], dec[rk + HEAD_DIM:rk + 2 * HEAD_DIM])
            pair.append((d, u))
        return pair

    per_seq = n_sub // seqs
    states = [[s_ref[q, 0], s_ref[q, 1]] for q in range(seqs)]
    o_parts = []

    def advance(j):
        r0 = j * SUB
        pk = states[j // per_seq]
        pb = [x.astype(BF16) for x in pk]
        s_bd = jnp.concatenate([
            jnp.concatenate([jnp.where(low, pb[0], 0.0).astype(BF16), zero_b], axis=1),
            jnp.concatenate([jnp.where(low, 0.0, pb[0]).astype(BF16), zero_b], axis=1),
            jnp.concatenate([zero_b, jnp.where(low, pb[1], 0.0).astype(BF16)], axis=1),
            jnp.concatenate([zero_b, jnp.where(low, 0.0, pb[1]).astype(BF16)], axis=1)], axis=0)
        o_parts.append(_dot(qd[r0:r0 + SUB], s_bd))
        pk[:] = [d * p + u for p, (d, u) in zip(pk, contrib[j])]

    contrib = []
    att = jnp.zeros((L, 2 * HEAD_DIM), F32)
    for s in range(SUB):
        w = jnp.exp2(bl - from_source(bl, s))
        p = jnp.where(t_loc >= s, q * from_source(key, s) * w, 0.0)
        att = att + _dot(p.astype(BF16), esel_ref[s])
        while len(contrib) < ((s + 1) * n_sub) // SUB:
            contrib.append(contribution(len(contrib)))
    for j in range(n_sub):
        advance(j)
    for q_i in range(seqs):
        s_ref[q_i, 0] = states[q_i][0]
        s_ref[q_i, 1] = states[q_i][1]
    attb = att.astype(BF16)
    same_sub = (lax.broadcasted_iota(jnp.int32, (L, L), 0) // SUB
                == lax.broadcasted_iota(jnp.int32, (L, L), 1) // SUB)
    lane_head = lax.broadcasted_iota(jnp.int32, (1, HB_W), 1) // HEAD_DIM
    o = jnp.zeros((L, HB_W), F32)
    for h in range(N_HEADS):
        full = jnp.where(same_sub, _dot(attb, erep_ref[h]), 0.0)
        o = jnp.where(lane_head == h, _dot(full.astype(BF16), vb), o)

    o_inter = jnp.concatenate(o_parts, axis=0)
    if rows < L:
        o_inter = _pad_rows(o_inter, L)
    o = o + o_inter

    ms = _dot_exact_right(o * o, eblk) * (1.0 / HEAD_DIM)
    yc = o * lax.rsqrt(ms + RMS_EPS) * hng_ref[...] * (gate * jax.nn.sigmoid(gate))
    yc_ref[...] = yc[:rows]

    @pl.when(step == pl.num_programs(1) - 1)
    def _():
        for q_i in range(seqs):
            s1_ref[q_i, 0] = states[q_i][0]
            s1_ref[q_i, 1] = states[q_i][1]


def _hgrn_rows(t_len, batch=1):
    if t_len < SCAN_ROWS and t_len % SUB == 0 and SCAN_ROWS % t_len == 0 and batch % (SCAN_ROWS // t_len) == 0:
        return SCAN_ROWS, SCAN_ROWS, SCAN_ROWS // t_len
    rows = _tile(t_len, HGRN_ROWS, SUB)
    return rows, max(rows, SCAN_ROWS), 1


def _hgrn(c_all, lb, hn_g, s0_pk, tri16, eblk, esel, erep, batch, t_len):
    rows, length, seqs = _hgrn_rows(t_len, batch)
    n = batch * t_len
    steps = n // (batch // seqs) // rows
    pk_spec = pl.BlockSpec((seqs, 2, HEAD_DIM, 2 * HEAD_DIM), lambda bi, ci: (bi, 0, 0, 0))
    return pl.pallas_call(
        functools.partial(_hgrn_body, rows=rows, length=length, seqs=seqs),
        grid=(batch // seqs, steps),
        in_specs=[pl.BlockSpec((rows, 4 * HB_W), lambda bi, ci: (bi * steps + ci, 0)),
                  _const_spec((1, HB_W)), _const_spec((1, HB_W)), pk_spec,
                  _const_spec(tri16.shape), _const_spec(eblk.shape), _const_spec(esel.shape),
                  _const_spec(erep.shape)],
        out_specs=[pl.BlockSpec((rows, HB_W), lambda bi, ci: (bi * steps + ci, 0)), pk_spec],
        out_shape=[jax.ShapeDtypeStruct((n, HB_W), F32),
                   jax.ShapeDtypeStruct((batch, 2, HEAD_DIM, 2 * HEAD_DIM), F32)],
        scratch_shapes=[pltpu.VMEM((seqs, 2, HEAD_DIM, 2 * HEAD_DIM), F32)],
        compiler_params=_cparams(("parallel", "arbitrary")),
        name="hgrn",
    )(c_all, lb, hn_g, s0_pk, tri16, eblk, esel, erep)


def _attn_prompt_body(qi_tab, ki_tab, lam_ref, q_ref, k_ref, vt_ref, g_ref, o_ref, qx_ref, m_ref, acc_ref,
                      s_ref, bmax_ref, *, tq, out_scale):
    pair = pl.program_id(2)
    qi = qi_tab[pair]
    ki = ki_tab[pair]
    ns = tq // Q_STRIP
    dv = 2 * HEAD_DIM

    @pl.when(ki == 0)
    def _():
        qt = (q_ref[...] * (HEAD_DIM ** -0.5 * LOG2E)).T
        first = lax.broadcasted_iota(jnp.int32, (dv, 1), 0) < HEAD_DIM
        qa = jnp.where(first, qt, 0.0).astype(BF16)
        qb = jnp.where(first, 0.0, qt).astype(BF16)
        for st in range(ns):
            qx_ref[0, st] = qa[:, st * Q_STRIP:(st + 1) * Q_STRIP]
            qx_ref[1, st] = qb[:, st * Q_STRIP:(st + 1) * Q_STRIP]
        m_ref[...] = jnp.full(m_ref.shape, NEG_BIG, F32)
        acc_ref[...] = jnp.zeros(acc_ref.shape, F32)

    def scores(kb, st, diag):
        k = k_ref[kb * Q_STRIP:(kb + 1) * Q_STRIP, :]
        for mp in range(2):
            s = _dot(k, qx_ref[mp, st])
            if diag and st == kb:
                k_chunk = lax.broadcasted_iota(jnp.int32, (Q_STRIP, 1), 0) // CHUNK
                q_chunk = lax.broadcasted_iota(jnp.int32, (1, Q_STRIP), 1) // CHUNK
                s = jnp.where(k_chunk <= q_chunk, s, NEG_BIG)
            s_ref[kb % 2, mp, st] = s
            bmax_ref[kb % 2, mp, st] = jnp.max(s, axis=0, keepdims=True)

    def accumulate(kb, st):
        vt = vt_ref[kb]
        for mp in range(2):
            m_old = m_ref[mp, st]
            m_new = jnp.maximum(m_old, bmax_ref[kb % 2, mp, st])
            p = jnp.exp2(s_ref[kb % 2, mp, st] - m_new).astype(BF16)
            corr = jnp.exp2(m_old - m_new)
            acc_ref[mp, st] = corr * acc_ref[mp, st] + _dot(vt, p)
            m_ref[mp, st] = m_new

    def key_tile(diag):
        first_strip = (lambda kb: kb) if diag else (lambda kb: 0)
        for st in range(ns):
            scores(0, st, diag)
        for kb in range(ns):
            for st in range(first_strip(kb), ns):
                accumulate(kb, st)
                if kb + 1 < ns and st >= first_strip(kb + 1):
                    scores(kb + 1, st, diag)

    @pl.when(ki < qi)
    def _():
        key_tile(False)

    @pl.when(ki == qi)
    def _():
        key_tile(True)
        lam = lam_ref[0, 0]
        for st in range(ns):
            a1 = acc_ref[0, st]
            a2 = acc_ref[1, st]
            o = a1[:dv] / a1[dv:dv + 1] - lam * (a2[:dv] / a2[dv:dv + 1])
            ms = jnp.mean(o * o, axis=0, keepdims=True)
            o = o * lax.rsqrt(ms + RMS_EPS) * g_ref[...] * out_scale
            o_ref[st * Q_STRIP:(st + 1) * Q_STRIP, :] = o.T


def _attn_prompt(lam, bq, bkb, vt, dn_g, batch, t_len, out_scale):
    tq = _tile(t_len, ATT_TILE, Q_STRIP)
    nq = t_len // tq
    ns = tq // Q_STRIP
    n = batch * t_len
    w = 2 * HEAD_DIM
    pairs = [(qi, ki) for qi in range(nq) for ki in range(qi + 1)]
    qi_tab = jnp.asarray([p[0] for p in pairs], jnp.int32)
    ki_tab = jnp.asarray([p[1] for p in pairs], jnp.int32)
    qmap = lambda b, h, p, qt, kt: (b * nq + qt[p], h)
    kmap = lambda b, h, p, qt, kt: (b * nq + kt[p], h)
    vmap_ = lambda b, h, p, qt, kt: (h, b * nq + kt[p], 0, 0)
    grid_spec = pltpu.PrefetchScalarGridSpec(
        num_scalar_prefetch=2,
        grid=(batch, N_HEADS, len(pairs)),
        in_specs=[pl.BlockSpec(memory_space=pltpu.SMEM),
                  pl.BlockSpec((tq, w), qmap), pl.BlockSpec((tq, w), kmap),
                  pl.BlockSpec((None, ns, VT_ROWS, Q_STRIP), vmap_),
                  pl.BlockSpec((w, 1), lambda b, h, p, qt, kt: (0, 0))],
        out_specs=pl.BlockSpec((tq, w), qmap),
        scratch_shapes=[pltpu.VMEM((2, ns, w, Q_STRIP), BF16),
                        pltpu.VMEM((2, ns, 1, Q_STRIP), F32),
                        pltpu.VMEM((2, ns, VT_ROWS, Q_STRIP), F32),
                        pltpu.VMEM((2, 2, ns, Q_STRIP, Q_STRIP), F32),
                        pltpu.VMEM((2, 2, ns, 1, Q_STRIP), F32)])
    return pl.pallas_call(
        functools.partial(_attn_prompt_body, tq=tq, out_scale=out_scale),
        grid_spec=grid_spec,
        out_shape=jax.ShapeDtypeStruct((n, ATT_W), F32),
        compiler_params=_cparams(("parallel", "parallel", "arbitrary")),
        name="attn_prompt",
    )(qi_tab, ki_tab, lam, bq, bkb, vt, dn_g.reshape(w, 1))


def _attn_sample_body(lam_ref, q_ref, kn_ref, vn_ref, kp_ref, vp_ref, g_ref, o_ref, *, t_new, past, out_scale):
    w = 2 * HEAD_DIM
    first = lax.broadcasted_iota(jnp.int32, (1, w), 1) < HEAD_DIM
    for h in range(N_HEADS):
        cols = slice(h * w, (h + 1) * w)
        q = q_ref[:, cols] * (HEAD_DIM ** -0.5)
        q2 = jnp.concatenate([jnp.where(first, q, 0.0), jnp.where(first, 0.0, q)], axis=0).astype(BF16)
        kp = kp_ref[pl.ds(h, past, stride=N_HEADS), :].astype(BF16)
        vp = vp_ref[pl.ds(h, past, stride=N_HEADS), :].astype(BF16)
        s_p = _dot_nt(q2, kp)
        kn = kn_ref[pl.ds(h, t_new, stride=N_HEADS), :].astype(BF16)
        vn = vn_ref[pl.ds(h, t_new, stride=N_HEADS), :].astype(BF16)
        s_n = _dot_nt(q2, kn)
        m = jnp.maximum(jnp.max(s_p, axis=1, keepdims=True), jnp.max(s_n, axis=1, keepdims=True))
        p_p = jnp.exp(s_p - m)
        p_n = jnp.exp(s_n - m)
        l = jnp.sum(p_p, axis=1, keepdims=True) + jnp.sum(p_n, axis=1, keepdims=True)
        acc = _dot(p_p.astype(BF16), vp) + _dot(p_n.astype(BF16), vn)
        o2 = acc / l
        o = o2[:t_new] - lam_ref[0, 0] * o2[t_new:]
        ms = jnp.mean(o * o, axis=1, keepdims=True)
        o_ref[:, cols] = o * lax.rsqrt(ms + RMS_EPS) * g_ref[...] * out_scale


def _attn_sample(lam, bq, bk, bv, cache_k, cache_v, layer, dn_g, batch, t_new, out_scale):
    w = 2 * HEAD_DIM
    depth, _, past = cache_k.shape[:3]
    rows = past * N_HEADS
    new = pl.BlockSpec((t_new, ATT_W), lambda b: (b, 0))
    new_kv = pl.BlockSpec((None, t_new * N_HEADS, w), lambda b: (layer, b, 0))
    old = pl.BlockSpec((None, None, rows, w), lambda b: (layer, b, 0, 0))
    return pl.pallas_call(
        functools.partial(_attn_sample_body, t_new=t_new, past=past, out_scale=out_scale),
        grid=(batch,),
        in_specs=[pl.BlockSpec(memory_space=pltpu.SMEM), new, new_kv, new_kv, old, old, _const_spec((1, w))],
        out_specs=new,
        out_shape=jax.ShapeDtypeStruct((batch * t_new, ATT_W), F32),
        compiler_params=_cparams(("parallel",)),
        name="attn_sample",
    )(lam, bq, bk, bv, cache_k.reshape(depth, batch, rows, w), cache_v.reshape(depth, batch, rows, w),
      dn_g.reshape(1, w))


def _to_head_pairs(s):
    b = s.shape[0]
    r = s.reshape(b, N_HEADS // 2, 2, HEAD_DIM, HEAD_DIM)
    return jnp.swapaxes(r, 2, 3).reshape(b, N_HEADS // 2, HEAD_DIM, 2 * HEAD_DIM)


def _from_head_pairs(p):
    b = p.shape[0]
    r = p.reshape(b, N_HEADS // 2, HEAD_DIM, 2, HEAD_DIM)
    return jnp.swapaxes(r, 2, 3).reshape(b, N_HEADS, HEAD_DIM, HEAD_DIM)


def _rearrange_proj(a):
    gates = lambda c0: jnp.repeat(a[..., c0:c0 + N_HEADS], HEAD_DIM, axis=-1)
    out = jnp.concatenate([a[..., A_QK0:A_I0], a[..., A_O0:B_Q0], gates(A_I0), gates(A_F0), a[..., B_Q0:N_IN]],
                          axis=-1)
    assert out.shape[-1] == N_CAT
    return out


def _consts(hgrn_len, mlstm_len):
    r = jnp.arange(mlstm_len)
    tri = (r[None, :] <= r[:, None]).astype(BF16)
    r = jnp.arange(hgrn_len)
    tri16 = ((r[None, :] <= r[:, None]) & (r[None, :] // SUB == r[:, None] // SUB)).astype(BF16)
    hh = jnp.arange(HB_W) // HEAD_DIM
    eblk = (hh[None, :] == hh[:, None]).astype(BF16)
    lane = jnp.arange(2 * HEAD_DIM)
    src = jnp.arange(SUB)
    esel = ((hh[None, :, None] * SUB + src[:, None, None]) == lane[None, None, :]).astype(BF16)
    erep = ((lane[None, :, None] // SUB == jnp.arange(N_HEADS)[:, None, None])
            & (lane[None, :, None] % SUB == r[None, None, :] % SUB)).astype(BF16)
    return tri, tri16, eblk, esel, erep


def _group(x, depth, layer_w, states, cache, batch, t_len, alpha, consts):
    tri, tri16, eblk, esel, erep = consts
    prompt = cache is None
    outs = []
    kv_bufs = tuple(jnp.zeros((depth, batch * t_len * N_HEADS, 2 * HEAD_DIM), F32) for _ in range(2))
    for l in range(depth):
        w = layer_w[l]
        conv0, c0, n0, m0, s0 = states(l)
        x = _ffn_ln(x, w["f_in0"], w["f_out0"], w["ln_g"][0], w["ln_b"][0], alpha)
        res = _in_proj(x, w["w_cat"], w["b_cat"], prompt, l, depth, kv_bufs)
        proj = dict(zip(_PROJ_OUTS, res[:len(_PROJ_OUTS)]))
        bkb = res[len(_PROJ_OUTS)]
        kv_bufs = (proj["bk"], proj["bv"])

        conv0_p = jnp.pad(conv0, ((0, 0), (CONV_PAD - (CONV_W - 1), 0), (0, 0)))
        ya, c1, n1, m1, convn = _mlstm(proj, w["conv_w"], w["conv_b"], conv0_p, _to_head_pairs(c0),
                                       n0.reshape(batch, 1, HB_W),
                                       jnp.repeat(m0, HEAD_DIM, axis=-1).reshape(batch, 1, HB_W),
                                       tri, eblk, batch, t_len)
        lam_init = 0.8 - 0.6 * math.exp(-0.3 * l)
        if prompt:
            yb = _attn_prompt(w["lam"], proj["bq"], bkb, res[len(_PROJ_OUTS) + 1], w["dn_g"], batch, t_len,
                              1.0 - lam_init)
        else:
            yb = _attn_sample(w["lam"], proj["bq"], proj["bk"], proj["bv"], cache[0], cache[1], l, w["dn_g"],
                              batch, t_len, 1.0 - lam_init)
        yc, s1 = _hgrn(proj["c"], w["lb"], w["hn_g"], _to_head_pairs(s0), tri16, eblk, esel, erep, batch, t_len)
        x = _out_proj_ffn(ya, yb, yc, x, w["w_out"], w["ln_g"][1], w["ln_b"][1],
                          w["f_in1"], w["f_out1"], w["ln_g"][2], w["ln_b"][2], alpha)
        outs.append((_from_head_pairs(c1),
                     n1.reshape(batch, N_HEADS, HEAD_DIM),
                     m1.reshape(batch, N_HEADS, HEAD_DIM)[:, :, 0],
                     convn[:, CONV_PAD - (CONV_W - 1):, :],
                     _from_head_pairs(s1)))
    kv = tuple(buf.reshape(depth, batch, t_len, N_HEADS, 2 * HEAD_DIM) for buf in kv_bufs)
    return x, kv + tuple(jnp.stack(a) for a in zip(*outs))


def kernel(x_prompt, x_sample, cache_diff_k, cache_diff_v, state_mlstm_c, state_mlstm_n, state_mlstm_m,
           state_mlstm_conv, state_hgrn_s, w_in, b_in, w_out, mlstm_conv_w, mlstm_conv_b, diff_lambda,
           diff_norm_g, hgrn_lb_logits, hgrn_norm_g, ffn_w_in, ffn_w_out, ln_g, ln_b):
    depth = w_in.shape[0]
    alpha = (2.0 * depth) ** 0.25
    batch, seq, d_model = x_prompt.shape
    dec_batch, dec_seq, _ = x_sample.shape
    past = cache_diff_k.shape[2]

    p_lb = jax.nn.softmax(hgrn_lb_logits.astype(F32), axis=0)
    lower_bounds = jnp.cumsum(p_lb, axis=0) - p_lb[0]
    w_cat = _rearrange_proj(w_in.astype(BF16))
    b_cat = _rearrange_proj(b_in).reshape(depth, 1, N_CAT)
    lp = diff_lambda.astype(F32)
    lam_init = jnp.asarray([0.8 - 0.6 * math.exp(-0.3 * l) for l in range(depth)], F32)
    lam = (jnp.exp(jnp.sum(lp[:, 0] * lp[:, 1], axis=-1)) - jnp.exp(jnp.sum(lp[:, 2] * lp[:, 3], axis=-1))
           + lam_init)
    f_in = ffn_w_in.astype(BF16)
    f_out = ffn_w_out.astype(BF16)
    w_out_b = w_out.astype(BF16)
    layer_w = [dict(w_cat=w_cat[l], b_cat=b_cat[l],
                    conv_w=mlstm_conv_w[l], conv_b=mlstm_conv_b[l].reshape(1, 2 * HB_W),
                    lam=lam[l].reshape(1, 1), dn_g=diff_norm_g[l],
                    lb=lower_bounds[l].reshape(1, HB_W),
                    hn_g=jnp.tile(hgrn_norm_g[l], N_HEADS).reshape(1, HB_W),
                    w_out=w_out_b[l], f_in0=f_in[l, 0], f_out0=f_out[l, 0], f_in1=f_in[l, 1], f_out1=f_out[l, 1],
                    ln_g=ln_g[l], ln_b=ln_b[l]) for l in range(depth)]
    def zero_states(_):
        return (jnp.zeros((batch, CONV_W - 1, 2 * HB_W), F32), jnp.zeros((batch, N_HEADS, HEAD_DIM, HEAD_DIM), F32),
                jnp.zeros((batch, N_HEADS, HEAD_DIM), F32), jnp.zeros((batch, N_HEADS), F32),
                jnp.zeros((batch, N_HEADS, HEAD_DIM, HEAD_DIM), F32))

    def carried_states(l):
        return (state_mlstm_conv[l], state_mlstm_c[l], state_mlstm_n[l], state_mlstm_m[l], state_hgrn_s[l])

    y_p, p_out = _group(x_prompt.reshape(batch * seq, d_model), depth, layer_w, zero_states, None,
                        batch, seq, alpha, _consts(_hgrn_rows(seq)[1], _mlstm_rows(seq)[1]))
    cache = (cache_diff_k, cache_diff_v)
    y_s, s_out = _group(x_sample.reshape(dec_batch * dec_seq, d_model), depth, layer_w, carried_states, cache,
                        dec_batch, dec_seq, alpha, _consts(_hgrn_rows(dec_seq)[1], _mlstm_rows(dec_seq)[1]))
    return (y_p.reshape(batch, seq, d_model), y_s.reshape(dec_batch, dec_seq, d_model)) + p_out + s_out
```

```python
import functools
import math

import jax
import jax.numpy as jnp
from jax import lax
from jax.experimental import pallas as pl
from jax.experimental.pallas import tpu as pltpu

F32 = jnp.float32
BF16 = jnp.bfloat16

HEAD_DIM = 64
N_HEADS = 4
CONV_W = 4
CONV_PAD = 8
CHUNK = 64
HB_W = N_HEADS * HEAD_DIM
ATT_W = N_HEADS * 2 * HEAD_DIM
LN_EPS = 1e-5
RMS_EPS = 1e-6
NEG_BIG = -1e30
SCAN_ROWS = 128
MLSTM_ROWS = 128
HGRN_ROWS = 256
SUB = 16
LOG2E = 1.4426950408889634
VT_ROWS = 2 * HEAD_DIM + 16
Q_STRIP = 256
ATT_TILE = 2048
FFN_CHUNK = 256
FFN_TILE = 1024
FFN_ROWS = 512
V7X_VMEM_LIMIT = 56 * 1024 * 1024

A_QK0 = 0
A_V0 = A_QK0 + 2 * HB_W
A_I0 = A_V0 + HB_W
A_F0 = A_I0 + N_HEADS
A_O0 = A_F0 + N_HEADS
B_Q0 = A_O0 + HB_W
B_K0 = B_Q0 + ATT_W
B_V0 = B_K0 + ATT_W
C_Q0 = B_V0 + ATT_W
N_IN = C_Q0 + 4 * HB_W

_SEGS = {}
_off = 0
for _name, _w in (("aqk", 2 * HB_W), ("av", HB_W), ("ao", HB_W), ("gi", HB_W), ("gf", HB_W),
                  ("bq", ATT_W), ("bk", ATT_W), ("bv", ATT_W), ("c", 4 * HB_W)):
    _SEGS[_name] = (_off, _w)
    _off += _w
N_CAT = _off


def _tile(n, pref, mult=8):
    t = min(pref, n)
    while t > mult and (n % t or t % mult):
        t -= mult
    assert n % t == 0 and t % mult == 0, (n, pref, mult)
    return t


def _cparams(sem):
    return pltpu.CompilerParams(dimension_semantics=sem, vmem_limit_bytes=V7X_VMEM_LIMIT)


def _const_spec(shape, single=False):
    nd = len(shape)
    if single:
        return pl.BlockSpec(shape, lambda *_: (0,) * nd, pipeline_mode=pl.Buffered(1))
    return pl.BlockSpec(shape, lambda *_: (0,) * nd)


def _dot(a, b):
    return jnp.dot(a, b, preferred_element_type=F32)


def _dot_nt(a, b):
    return lax.dot_general(a, b, (((1,), (1,)), ((), ())), preferred_element_type=F32)


def _split3(x):
    a = x.astype(BF16)
    r = x - a.astype(F32)
    b = r.astype(BF16)
    c = (r - b.astype(F32)).astype(BF16)
    return a, b, c


def _dot_exact_left(mat01, x):
    a, b, c = _split3(x)
    return _dot(mat01, a) + _dot(mat01, b) + _dot(mat01, c)


def _dot_exact_right(x, mat01):
    a, b, c = _split3(x)
    return _dot(a, mat01) + _dot(b, mat01) + _dot(c, mat01)


def _layer_norm(y, g, b):
    mu = jnp.mean(y, axis=-1, keepdims=True)
    d = y - mu
    var = jnp.mean(d * d, axis=-1, keepdims=True)
    return d * lax.rsqrt(var + LN_EPS) * g + b


def _pad_rows(x, rows):
    if x.shape[0] == rows:
        return x
    return jnp.concatenate([x, jnp.zeros((rows - x.shape[0],) + x.shape[1:], x.dtype)], axis=0)


def _swiglu_ln(x, wi_ref, wo_ref, g_ref, b_ref, o_ref, acc_ref, alpha, d_ff, tf):
    rb = min(FFN_ROWS, x.shape[0])
    n_blocks = x.shape[0] // rb

    def finish(r):
        rows = slice(r * rb, (r + 1) * rb)
        y = alpha * x[rows] + 0.5 * acc_ref[rows, :]
        o_ref[rows, :] = _layer_norm(y, g_ref[...], b_ref[...])

    for r in range(n_blocks):
        rows = slice(r * rb, (r + 1) * rb)
        xb = x[rows].astype(BF16)
        for c in range(d_ff // tf):
            gate = _dot(xb, wi_ref[:, c * tf:(c + 1) * tf])
            up = _dot(xb, wi_ref[:, d_ff + c * tf:d_ff + (c + 1) * tf])
            h = (gate * jax.nn.sigmoid(gate) * up).astype(BF16)
            part = _dot(h, wo_ref[c * tf:(c + 1) * tf, :])
            if c == 0:
                acc_ref[rows, :] = part
            else:
                acc_ref[rows, :] += part
            if c == 0 and r > 0:
                finish(r - 1)
    finish(n_blocks - 1)


def _ffn_ln_body(x_ref, wi_ref, wo_ref, g_ref, b_ref, o_ref, acc_ref, *, alpha, d_ff, tf):
    _swiglu_ln(x_ref[...], wi_ref, wo_ref, g_ref, b_ref, o_ref, acc_ref, alpha, d_ff, tf)


def _ffn_ln(x, wi, wo, g, b, alpha):
    n, d = x.shape
    d_ff = wo.shape[0]
    tm = _tile(n, FFN_TILE)
    tf = _tile(d_ff, FFN_CHUNK, 128)
    return pl.pallas_call(
        functools.partial(_ffn_ln_body, alpha=alpha, d_ff=d_ff, tf=tf),
        grid=(n // tm,),
        in_specs=[pl.BlockSpec((tm, d), lambda i: (i, 0)),
                  _const_spec(wi.shape, single=True), _const_spec(wo.shape, single=True),
                  _const_spec((1, d)), _const_spec((1, d))],
        out_specs=pl.BlockSpec((tm, d), lambda i: (i, 0)),
        out_shape=jax.ShapeDtypeStruct((n, d), F32),
        scratch_shapes=[pltpu.VMEM((tm, d), F32)],
        compiler_params=_cparams(("parallel",)),
        name="ffn_ln",
    )(x, wi, wo, g.reshape(1, d), b.reshape(1, d))


_PROJ_OUTS = ("aqk", "av", "ao", "gi", "gf", "bq", "bk", "bv", "c")


def _in_proj_body(*refs, emit_vt, n_alias):
    x_ref, w_ref, bias_ref = refs[:3]
    out_refs = refs[3 + n_alias:]
    xb = x_ref[...].astype(BF16)
    tm = xb.shape[0]
    outs = dict(zip(_PROJ_OUTS, out_refs))
    dv = 2 * HEAD_DIM
    ones_rows = jnp.where(lax.broadcasted_iota(jnp.int32, (VT_ROWS - dv, Q_STRIP), 0) == 0, 1.0, 0.0).astype(BF16)
    for name in _PROJ_OUTS:
        c0, w = _SEGS[name]
        for s in range(0, w, 2 * HB_W):
            e = min(s + 2 * HB_W, w)
            val = _dot(xb, w_ref[:, c0 + s:c0 + e]) + bias_ref[:, c0 + s:c0 + e]
            if name in ("bk", "bv"):
                kv = outs[name]
                if n_alias == 0:
                    kv[1:] = jnp.zeros((kv.shape[0] - 1,) + kv.shape[1:], F32)
                    kv = kv.at[0]
                for h in range(N_HEADS):
                    kv[pl.ds(h, tm, stride=N_HEADS), :] = val[:, h * 2 * HEAD_DIM:(h + 1) * 2 * HEAD_DIM]
            else:
                outs[name][:, s:e] = val
            if name == "bk":
                out_refs[len(_PROJ_OUTS)][:, s:e] = val.astype(BF16)
            if name == "bv" and emit_vt:
                vt_ref = out_refs[len(_PROJ_OUTS) + 1]
                for h in range(N_HEADS):
                    vt = val[:, h * dv:(h + 1) * dv].T.astype(BF16)
                    for j in range(tm // Q_STRIP):
                        vt_ref[h, j, 0:dv, :] = vt[:, j * Q_STRIP:(j + 1) * Q_STRIP]
                        vt_ref[h, j, dv:VT_ROWS, :] = ones_rows


def _in_proj(x, w_cat, b_cat, emit_vt, layer, depth, kv_bufs):
    n, d = x.shape
    tm = _tile(n, 512, Q_STRIP if emit_vt else 8)
    out_shape, out_specs = [], []
    for k in _PROJ_OUTS:
        if k in ("bk", "bv"):
            out_shape.append(jax.ShapeDtypeStruct((depth, n * N_HEADS, 2 * HEAD_DIM), F32))
            if kv_bufs is None:
                out_specs.append(pl.BlockSpec((depth, tm * N_HEADS, 2 * HEAD_DIM), lambda i: (0, i, 0)))
            else:
                out_specs.append(pl.BlockSpec((None, tm * N_HEADS, 2 * HEAD_DIM), lambda i: (layer, i, 0)))
        else:
            out_shape.append(jax.ShapeDtypeStruct((n, _SEGS[k][1]), F32))
            out_specs.append(pl.BlockSpec((tm, _SEGS[k][1]), lambda i: (i, 0)))
    out_shape.append(jax.ShapeDtypeStruct((n, ATT_W), BF16))
    out_specs.append(pl.BlockSpec((tm, ATT_W), lambda i: (i, 0)))
    if emit_vt:
        out_shape.append(jax.ShapeDtypeStruct((N_HEADS, n // Q_STRIP, VT_ROWS, Q_STRIP), BF16))
        out_specs.append(pl.BlockSpec((N_HEADS, tm // Q_STRIP, VT_ROWS, Q_STRIP), lambda i: (0, i, 0, 0)))
    in_specs = [pl.BlockSpec((tm, d), lambda i: (i, 0)), _const_spec(w_cat.shape, single=True),
                _const_spec(b_cat.shape)]
    args = [x, w_cat, b_cat]
    aliases = {}
    for buf, name in zip(kv_bufs or (), ("bk", "bv")):
        aliases[len(args)] = _PROJ_OUTS.index(name)
        in_specs.append(pl.BlockSpec(memory_space=pl.ANY))
        args.append(buf)
    return pl.pallas_call(
        functools.partial(_in_proj_body, emit_vt=emit_vt, n_alias=len(aliases)),
        grid=(n // tm,),
        in_specs=in_specs,
        out_specs=out_specs,
        out_shape=out_shape,
        input_output_aliases=aliases,
        compiler_params=_cparams(("parallel",)),
        name="in_proj",
    )(*args)


def _out_proj_ffn_body(ya_ref, yb_ref, yc_ref, x_ref, w_ref, g1_ref, b1_ref, wi_ref, wo_ref, g2_ref, b2_ref,
                       o_ref, acc_ref, *, alpha, d_ff, tf):
    y = _dot(ya_ref[...].astype(BF16), w_ref[0:HB_W, :])
    y += _dot(yb_ref[...].astype(BF16), w_ref[HB_W:HB_W + ATT_W, :])
    y += _dot(yc_ref[...].astype(BF16), w_ref[HB_W + ATT_W:, :])
    x = _layer_norm(alpha * x_ref[...] + y, g1_ref[...], b1_ref[...])
    _swiglu_ln(x, wi_ref, wo_ref, g2_ref, b2_ref, o_ref, acc_ref, alpha, d_ff, tf)


def _out_proj_ffn(ya, yb, yc, x, w, g1, b1, wi, wo, g2, b2, alpha):
    n, d = x.shape
    d_ff = wo.shape[0]
    tm = _tile(n, FFN_TILE)
    tf = _tile(d_ff, FFN_CHUNK, 128)
    row = lambda width: pl.BlockSpec((tm, width), lambda i: (i, 0))
    vec = _const_spec((1, d))
    return pl.pallas_call(
        functools.partial(_out_proj_ffn_body, alpha=alpha, d_ff=d_ff, tf=tf),
        grid=(n // tm,),
        in_specs=[row(HB_W), row(ATT_W), row(HB_W), row(d), _const_spec(w.shape, single=True), vec, vec,
                  _const_spec(wi.shape, single=True), _const_spec(wo.shape, single=True), vec, vec],
        out_specs=row(d),
        out_shape=jax.ShapeDtypeStruct((n, d), F32),
        scratch_shapes=[pltpu.VMEM((tm, d), F32)],
        compiler_params=_cparams(("parallel",)),
        name="out_proj_ffn",
    )(ya, yb, yc, x, w, g1.reshape(1, d), b1.reshape(1, d), wi, wo, g2.reshape(1, d), b2.reshape(1, d))


def _mlstm_body(qk_ref, v_ref, o_ref, gi_ref, gf_ref, cw_ref, cb_ref, conv0_ref, c0_ref, n0_ref, m0_ref,
                tri_ref, eblk_ref,
                ya_ref, c1_ref, n1_ref, m1_ref, convn_ref,
                up_ref, c_ref, n_ref, m_ref, *, rows):
    step = pl.program_id(1)
    low = lax.broadcasted_iota(jnp.int32, (1, 2 * HEAD_DIM), 1) < HEAD_DIM
    pw = 2 * HEAD_DIM
    L = tri_ref.shape[0]

    @pl.when(step == 0)
    def _():
        up_ref[0:CONV_PAD, :] = conv0_ref[...]
        c_ref[...] = jnp.zeros(c_ref.shape, F32)
        for i in range(N_HEADS // 2):
            pair = c0_ref[i]
            c_ref[i * pw:i * pw + HEAD_DIM, i * pw:(i + 1) * pw] = jnp.where(low, pair, 0.0)
            c_ref[i * pw + HEAD_DIM:(i + 1) * pw, i * pw:(i + 1) * pw] = jnp.where(low, 0.0, pair)
        n_ref[...] = n0_ref[...]
        m_ref[...] = m0_ref[...]

    gi = _pad_rows(gi_ref[...], L)
    flog = jax.nn.log_sigmoid(_pad_rows(gf_ref[...], L))
    if rows < L:
        live = lax.broadcasted_iota(jnp.int32, (L, 1), 0) < rows
        gi = jnp.where(live, gi, NEG_BIG)
        flog = jnp.where(live, flog, 0.0)
    b = _dot_exact_left(tri_ref[...], flog)
    b_t = b.T
    gi_t = gi.T

    u = _pad_rows(qk_ref[...], L)
    up_ref[CONV_PAD:CONV_PAD + L, :] = u
    y = cb_ref[...]
    first = CONV_PAD - (CONV_W - 1)
    for j in range(CONV_W):
        y = y + cw_ref[j:j + 1, :] * up_ref[first + j:first + j + L, :]
    convn_ref[...] = up_ref[rows:rows + CONV_PAD, :]
    up_ref[0:CONV_PAD, :] = up_ref[L:L + CONV_PAD, :]
    qk = y * jax.nn.sigmoid(y)
    q = qk[:, :HB_W] * (HEAD_DIM ** -0.5)
    k = qk[:, HB_W:]
    v = _pad_rows(v_ref[...], L)
    kb = k.astype(BF16)
    vb = v.astype(BF16)
    qb = q.astype(BF16)

    lane_head = lax.broadcasted_iota(jnp.int32, (1, HB_W), 1) // HEAD_DIM
    qk_scores = [_dot_nt(jnp.where(lane_head == h, q, 0.0).astype(BF16), kb) for h in range(N_HEADS)]
    c_prev = c_ref[...]
    n_prev = n_ref[...]
    eblk = eblk_ref[...]
    num_state = _dot(qb, c_prev.astype(BF16))
    den_state = _dot((q * n_prev).astype(BF16), eblk)

    m_prev = m_ref[...]
    g = b + m_prev

    t_idx = lax.broadcasted_iota(jnp.int32, (L, L), 0)
    s_idx = lax.broadcasted_iota(jnp.int32, (L, L), 1)
    causal = s_idx <= t_idx

    mrow_hb = jnp.zeros((L, HB_W), F32)
    wg_hb = jnp.zeros((L, HB_W), F32)
    den_hb = jnp.zeros((L, HB_W), F32)
    num_hb = jnp.zeros((L, HB_W), F32)
    for h in range(N_HEADS):
        sel = lane_head == h
        c0 = h * HEAD_DIM
        dmat = b[:, c0:c0 + 1] - b_t[c0:c0 + 1, :] + gi_t[c0:c0 + 1, :]
        dmat = jnp.where(causal, dmat, NEG_BIG)
        gcol = g[:, c0:c0 + 1]
        mrow = jnp.maximum(gcol, jnp.max(dmat, axis=1, keepdims=True))
        wd = jnp.exp(dmat - mrow)
        wg = jnp.exp(gcol - mrow)
        qkw = qk_scores[h] * wd
        den = jnp.sum(qkw, axis=1, keepdims=True)
        num = _dot(qkw.astype(BF16), vb)
        mrow_hb = jnp.where(sel, mrow, mrow_hb)
        wg_hb = jnp.where(sel, wg, wg_hb)
        den_hb = jnp.where(sel, den, den_hb)
        num_hb = jnp.where(sel, num, num_hb)

    num_hb = wg_hb * num_state + num_hb
    den_hb = wg_hb * den_state + den_hb
    hout = num_hb / jnp.maximum(jnp.abs(den_hb), jnp.exp(-mrow_hb))
    ya = jax.nn.sigmoid(_pad_rows(o_ref[...], L)) * hout
    ya_ref[...] = ya[:rows]

    m_new = mrow_hb[L - 1:L, :]
    b_last = b[L - 1:L, :]
    decay = jnp.exp(b_last + m_prev - m_new)
    ws = jnp.exp(b_last - b + gi - m_new)
    kw = k * ws
    upd = _dot(kw.T.astype(BF16), vb) * eblk.astype(F32)
    c_new = decay * c_prev + upd
    n_new = decay * n_prev + jnp.sum(kw, axis=0, keepdims=True)
    c_ref[...] = c_new
    n_ref[...] = n_new
    m_ref[...] = m_new

    @pl.when(step == pl.num_programs(1) - 1)
    def _():
        for i in range(N_HEADS // 2):
            c1_ref[i] = jnp.where(low, c_new[i * pw:i * pw + HEAD_DIM, i * pw:(i + 1) * pw],
                                  c_new[i * pw + HEAD_DIM:(i + 1) * pw, i * pw:(i + 1) * pw])
        n1_ref[...] = n_new
        m1_ref[...] = m_new


def _mlstm_rows(t_len):
    rows = _tile(t_len, MLSTM_ROWS, 8)
    return rows, max(rows, SCAN_ROWS)


def _mlstm(proj, conv_w, conv_b, conv0, c0_pk, n0, m0_hb, tri, eblk, batch, t_len):
    rows, length = _mlstm_rows(t_len)
    assert tri.shape == (length, length)
    steps = t_len // rows
    tok = lambda width: pl.BlockSpec((rows, width), lambda bi, ci: (bi * steps + ci, 0))
    per_b = lambda shape: pl.BlockSpec((None,) + shape, lambda bi, ci: (bi,) + (0,) * len(shape))
    return pl.pallas_call(
        functools.partial(_mlstm_body, rows=rows),
        grid=(batch, steps),
        in_specs=[tok(2 * HB_W), tok(HB_W), tok(HB_W), tok(HB_W), tok(HB_W),
                  _const_spec(conv_w.shape), _const_spec(conv_b.shape),
                  per_b((CONV_PAD, 2 * HB_W)), per_b((N_HEADS // 2, HEAD_DIM, 2 * HEAD_DIM)), per_b((1, HB_W)),
                  per_b((1, HB_W)), _const_spec(tri.shape), _const_spec(eblk.shape)],
        out_specs=[tok(HB_W), per_b((N_HEADS // 2, HEAD_DIM, 2 * HEAD_DIM)), per_b((1, HB_W)), per_b((1, HB_W)),
                   per_b((CONV_PAD, 2 * HB_W))],
        out_shape=[jax.ShapeDtypeStruct((batch * t_len, HB_W), F32),
                   jax.ShapeDtypeStruct((batch, N_HEADS // 2, HEAD_DIM, 2 * HEAD_DIM), F32),
                   jax.ShapeDtypeStruct((batch, 1, HB_W), F32),
                   jax.ShapeDtypeStruct((batch, 1, HB_W), F32),
                   jax.ShapeDtypeStruct((batch, CONV_PAD, 2 * HB_W), F32)],
        scratch_shapes=[pltpu.VMEM((length + CONV_PAD, 2 * HB_W), F32),
                        pltpu.VMEM((HB_W, HB_W), F32),
                        pltpu.VMEM((1, HB_W), F32),
                        pltpu.VMEM((1, HB_W), F32)],
        compiler_params=_cparams(("parallel", "arbitrary")),
        name="mlstm",
    )(proj["aqk"], proj["av"], proj["ao"], proj["gi"], proj["gf"], conv_w, conv_b, conv0, c0_pk, n0, m0_hb,
      tri, eblk)


def _hgrn_body(c_ref, lb_ref, hng_ref, s0_ref, tri16_ref, eblk_ref, esel_ref, erep_ref, yc_ref, s1_ref, s_ref,
               *, rows, length, seqs):
    step = pl.program_id(1)
    L = length

    @pl.when(step == 0)
    def _():
        s_ref[...] = s0_ref[...]

    cin = _pad_rows(c_ref[...], L)
    q = cin[:, 0:HB_W]
    f_pre = cin[:, HB_W:2 * HB_W]
    v = cin[:, 2 * HB_W:3 * HB_W]
    gate = cin[:, 3 * HB_W:]
    lb = lb_ref[...]
    forget = lb + (1.0 - lb) * jax.nn.sigmoid(f_pre)
    logf = jnp.log(forget)
    key = 1.0 - forget
    if rows < L:
        live = lax.broadcasted_iota(jnp.int32, (L, 1), 0) < rows
        logf = jnp.where(live, logf, 0.0)
        key = jnp.where(live, key, 0.0)
    bl = _dot_exact_left(tri16_ref[...], logf) * LOG2E
    eblk = eblk_ref[...]
    groups = L // SUB
    t_loc = lax.broadcasted_iota(jnp.int32, (L, 1), 0) % SUB
    vb = v.astype(BF16)

    def from_source(x, s):
        picked = x.reshape(groups, SUB, HB_W)[:, s:s + 1, :]
        return jnp.broadcast_to(picked, (groups, SUB, HB_W)).reshape(L, HB_W)

    n_sub = max(rows // SUB, 1)
    qd = (q * jnp.exp2(bl)).astype(BF16)
    bl_t = bl.T
    low = lax.broadcasted_iota(jnp.int32, (1, 2 * HEAD_DIM), 1) < HEAD_DIM
    zero_b = jnp.zeros((HEAD_DIM, 2 * HEAD_DIM), BF16)

    def contribution(j):
        r0 = j * SUB
        b_last = bl[r0 + SUB - 1:r0 + SUB, :]
        kw_j = (key[r0:r0 + SUB] * jnp.exp2(b_last - bl[r0:r0 + SUB])).astype(BF16)
        upd = lax.dot_general(kw_j, vb[r0:r0 + SUB], (((0,), (0,)), ((), ())),
                              preferred_element_type=F32)
        dec = jnp.exp2(bl_t[:, r0 + SUB - 1:r0 + SUB])
        pair = []
        for i in range(2):
            rk = 2 * i * HEAD_DIM
            u = jnp.where(low, upd[rk:rk + HEAD_DIM, rk:rk + 2 * HEAD_DIM],
                          upd[rk + HEAD_DIM:rk + 2 * HEAD_DIM, rk:rk + 2 * HEAD_DIM])
            d = jnp.where(low, dec[rk:rk + HEAD_DIM], dec[rk + HEAD_DIM:rk + 2 * HEAD_DIM])
            pair.append((d, u))
        return pair

    per_seq = n_sub // seqs
    states = [[s_ref[q, 0], s_ref[q, 1]] for q in range(seqs)]
    o_parts = []

    def advance(j):
        r0 = j * SUB
        pk = states[j // per_seq]
        pb = [x.astype(BF16) for x in pk]
        s_bd = jnp.concatenate([
            jnp.concatenate([jnp.where(low, pb[0], 0.0).astype(BF16), zero_b], axis=1),
            jnp.concatenate([jnp.where(low, 0.0, pb[0]).astype(BF16), zero_b], axis=1),
            jnp.concatenate([zero_b, jnp.where(low, pb[1], 0.0).astype(BF16)], axis=1),
            jnp.concatenate([zero_b, jnp.where(low, 0.0, pb[1]).astype(BF16)], axis=1)], axis=0)
        o_parts.append(_dot(qd[r0:r0 + SUB], s_bd))
        pk[:] = [d * p + u for p, (d, u) in zip(pk, contrib[j])]

    contrib = []
    att = jnp.zeros((L, 2 * HEAD_DIM), F32)
    for s in range(SUB):
        w = jnp.exp2(bl - from_source(bl, s))
        p = jnp.where(t_loc >= s, q * from_source(key, s) * w, 0.0)
        att = att + _dot(p.astype(BF16), esel_ref[s])
        while len(contrib) < ((s + 1) * n_sub) // SUB:
            contrib.append(contribution(len(contrib)))
    for j in range(n_sub):
        advance(j)
    for q_i in range(seqs):
        s_ref[q_i, 0] = states[q_i][0]
        s_ref[q_i, 1] = states[q_i][1]
    attb = att.astype(BF16)
    same_sub = (lax.broadcasted_iota(jnp.int32, (L, L), 0) // SUB
                == lax.broadcasted_iota(jnp.int32, (L, L), 1) // SUB)
    lane_head = lax.broadcasted_iota(jnp.int32, (1, HB_W), 1) // HEAD_DIM
    o = jnp.zeros((L, HB_W), F32)
    for h in range(N_HEADS):
        full = jnp.where(same_sub, _dot(attb, erep_ref[h]), 0.0)
        o = jnp.where(lane_head == h, _dot(full.astype(BF16), vb), o)

    o_inter = jnp.concatenate(o_parts, axis=0)
    if rows < L:
        o_inter = _pad_rows(o_inter, L)
    o = o + o_inter

    ms = _dot_exact_right(o * o, eblk) * (1.0 / HEAD_DIM)
    yc = o * lax.rsqrt(ms + RMS_EPS) * hng_ref[...] * (gate * jax.nn.sigmoid(gate))
    yc_ref[...] = yc[:rows]

    @pl.when(step == pl.num_programs(1) - 1)
    def _():
        for q_i in range(seqs):
            s1_ref[q_i, 0] = states[q_i][0]
            s1_ref[q_i, 1] = states[q_i][1]


def _hgrn_rows(t_len, batch=1):
    if t_len < SCAN_ROWS and t_len % SUB == 0 and SCAN_ROWS % t_len == 0 and batch % (SCAN_ROWS // t_len) == 0:
        return SCAN_ROWS, SCAN_ROWS, SCAN_ROWS // t_len
    rows = _tile(t_len, HGRN_ROWS, SUB)
    return rows, max(rows, SCAN_ROWS), 1


def _hgrn(c_all, lb, hn_g, s0_pk, tri16, eblk, esel, erep, batch, t_len):
    rows, length, seqs = _hgrn_rows(t_len, batch)
    n = batch * t_len
    steps = n // (batch // seqs) // rows
    pk_spec = pl.BlockSpec((seqs, 2, HEAD_DIM, 2 * HEAD_DIM), lambda bi, ci: (bi, 0, 0, 0))
    return pl.pallas_call(
        functools.partial(_hgrn_body, rows=rows, length=length, seqs=seqs),
        grid=(batch // seqs, steps),
        in_specs=[pl.BlockSpec((rows, 4 * HB_W), lambda bi, ci: (bi * steps + ci, 0)),
                  _const_spec((1, HB_W)), _const_spec((1, HB_W)), pk_spec,
                  _const_spec(tri16.shape), _const_spec(eblk.shape), _const_spec(esel.shape),
                  _const_spec(erep.shape)],
        out_specs=[pl.BlockSpec((rows, HB_W), lambda bi, ci: (bi * steps + ci, 0)), pk_spec],
        out_shape=[jax.ShapeDtypeStruct((n, HB_W), F32),
                   jax.ShapeDtypeStruct((batch, 2, HEAD_DIM, 2 * HEAD_DIM), F32)],
        scratch_shapes=[pltpu.VMEM((seqs, 2, HEAD_DIM, 2 * HEAD_DIM), F32)],
        compiler_params=_cparams(("parallel", "arbitrary")),
        name="hgrn",
    )(c_all, lb, hn_g, s0_pk, tri16, eblk, esel, erep)


def _attn_prompt_body(qi_tab, ki_tab, lam_ref, q_ref, k_ref, vt_ref, g_ref, o_ref, qx_ref, m_ref, acc_ref,
                      s_ref, bmax_ref, *, tq, out_scale):
    pair = pl.program_id(2)
    qi = qi_tab[pair]
    ki = ki_tab[pair]
    ns = tq // Q_STRIP
    dv = 2 * HEAD_DIM

    @pl.when(ki == 0)
    def _():
        qt = (q_ref[...] * (HEAD_DIM ** -0.5 * LOG2E)).T
        first = lax.broadcasted_iota(jnp.int32, (dv, 1), 0) < HEAD_DIM
        qa = jnp.where(first, qt, 0.0).astype(BF16)
        qb = jnp.where(first, 0.0, qt).astype(BF16)
        for st in range(ns):
            qx_ref[0, st] = qa[:, st * Q_STRIP:(st + 1) * Q_STRIP]
            qx_ref[1, st] = qb[:, st * Q_STRIP:(st + 1) * Q_STRIP]
        m_ref[...] = jnp.full(m_ref.shape, NEG_BIG, F32)
        acc_ref[...] = jnp.zeros(acc_ref.shape, F32)

    def scores(kb, st, diag):
        k = k_ref[kb * Q_STRIP:(kb + 1) * Q_STRIP, :]
        for mp in range(2):
            s = _dot(k, qx_ref[mp, st])
            if diag and st == kb:
                k_chunk = lax.broadcasted_iota(jnp.int32, (Q_STRIP, 1), 0) // CHUNK
                q_chunk = lax.broadcasted_iota(jnp.int32, (1, Q_STRIP), 1) // CHUNK
                s = jnp.where(k_chunk <= q_chunk, s, NEG_BIG)
            s_ref[kb % 2, mp, st] = s
            bmax_ref[kb % 2, mp, st] = jnp.max(s, axis=0, keepdims=True)

    def accumulate(kb, st):
        vt = vt_ref[kb]
        for mp in range(2):
            m_old = m_ref[mp, st]
            m_new = jnp.maximum(m_old, bmax_ref[kb % 2, mp, st])
            p = jnp.exp2(s_ref[kb % 2, mp, st] - m_new).astype(BF16)
            corr = jnp.exp2(m_old - m_new)
            acc_ref[mp, st] = corr * acc_ref[mp, st] + _dot(vt, p)
            m_ref[mp, st] = m_new

    def key_tile(diag):
        first_strip = (lambda kb: kb) if diag else (lambda kb: 0)
        for st in range(ns):
            scores(0, st, diag)
        for kb in range(ns):
            for st in range(first_strip(kb), ns):
                accumulate(kb, st)
                if kb + 1 < ns and st >= first_strip(kb + 1):
                    scores(kb + 1, st, diag)

    @pl.when(ki < qi)
    def _():
        key_tile(False)

    @pl.when(ki == qi)
    def _():
        key_tile(True)
        lam = lam_ref[0, 0]
        for st in range(ns):
            a1 = acc_ref[0, st]
            a2 = acc_ref[1, st]
            o = a1[:dv] / a1[dv:dv + 1] - lam * (a2[:dv] / a2[dv:dv + 1])
            ms = jnp.mean(o * o, axis=0, keepdims=True)
            o = o * lax.rsqrt(ms + RMS_EPS) * g_ref[...] * out_scale
            o_ref[st * Q_STRIP:(st + 1) * Q_STRIP, :] = o.T


def _attn_prompt(lam, bq, bkb, vt, dn_g, batch, t_len, out_scale):
    tq = _tile(t_len, ATT_TILE, Q_STRIP)
    nq = t_len // tq
    ns = tq // Q_STRIP
    n = batch * t_len
    w = 2 * HEAD_DIM
    pairs = [(qi, ki) for qi in range(nq) for ki in range(qi + 1)]
    qi_tab = jnp.asarray([p[0] for p in pairs], jnp.int32)
    ki_tab = jnp.asarray([p[1] for p in pairs], jnp.int32)
    qmap = lambda b, h, p, qt, kt: (b * nq + qt[p], h)
    kmap = lambda b, h, p, qt, kt: (b * nq + kt[p], h)
    vmap_ = lambda b, h, p, qt, kt: (h, b * nq + kt[p], 0, 0)
    grid_spec = pltpu.PrefetchScalarGridSpec(
        num_scalar_prefetch=2,
        grid=(batch, N_HEADS, len(pairs)),
        in_specs=[pl.BlockSpec(memory_space=pltpu.SMEM),
                  pl.BlockSpec((tq, w), qmap), pl.BlockSpec((tq, w), kmap),
                  pl.BlockSpec((None, ns, VT_ROWS, Q_STRIP), vmap_),
                  pl.BlockSpec((w, 1), lambda b, h, p, qt, kt: (0, 0))],
        out_specs=pl.BlockSpec((tq, w), qmap),
        scratch_shapes=[pltpu.VMEM((2, ns, w, Q_STRIP), BF16),
                        pltpu.VMEM((2, ns, 1, Q_STRIP), F32),
                        pltpu.VMEM((2, ns, VT_ROWS, Q_STRIP), F32),
                        pltpu.VMEM((2, 2, ns, Q_STRIP, Q_STRIP), F32),
                        pltpu.VMEM((2, 2, ns, 1, Q_STRIP), F32)])
    return pl.pallas_call(
        functools.partial(_attn_prompt_body, tq=tq, out_scale=out_scale),
        grid_spec=grid_spec,
        out_shape=jax.ShapeDtypeStruct((n, ATT_W), F32),
        compiler_params=_cparams(("parallel", "parallel", "arbitrary")),
        name="attn_prompt",
    )(qi_tab, ki_tab, lam, bq, bkb, vt, dn_g.reshape(w, 1))


def _attn_sample_body(lam_ref, q_ref, kn_ref, vn_ref, kp_ref, vp_ref, g_ref, o_ref, *, t_new, past, out_scale):
    w = 2 * HEAD_DIM
    first = lax.broadcasted_iota(jnp.int32, (1, w), 1) < HEAD_DIM
    for h in range(N_HEADS):
        cols = slice(h * w, (h + 1) * w)
        q = q_ref[:, cols] * (HEAD_DIM ** -0.5)
        q2 = jnp.concatenate([jnp.where(first, q, 0.0), jnp.where(first, 0.0, q)], axis=0).astype(BF16)
        kp = kp_ref[pl.ds(h, past, stride=N_HEADS), :].astype(BF16)
        vp = vp_ref[pl.ds(h, past, stride=N_HEADS), :].astype(BF16)
        s_p = _dot_nt(q2, kp)
        kn = kn_ref[pl.ds(h, t_new, stride=N_HEADS), :].astype(BF16)
        vn = vn_ref[pl.ds(h, t_new, stride=N_HEADS), :].astype(BF16)
        s_n = _dot_nt(q2, kn)
        m = jnp.maximum(jnp.max(s_p, axis=1, keepdims=True), jnp.max(s_n, axis=1, keepdims=True))
        p_p = jnp.exp(s_p - m)
        p_n = jnp.exp(s_n - m)
        l = jnp.sum(p_p, axis=1, keepdims=True) + jnp.sum(p_n, axis=1, keepdims=True)
        acc = _dot(p_p.astype(BF16), vp) + _dot(p_n.astype(BF16), vn)
        o2 = acc / l
        o = o2[:t_new] - lam_ref[0, 0] * o2[t_new:]
        ms = jnp.mean(o * o, axis=1, keepdims=True)
        o_ref[:, cols] = o * lax.rsqrt(ms + RMS_EPS) * g_ref[...] * out_scale


def _attn_sample(lam, bq, bk, bv, cache_k, cache_v, layer, dn_g, batch, t_new, out_scale):
    w = 2 * HEAD_DIM
    depth, _, past = cache_k.shape[:3]
    rows = past * N_HEADS
    new = pl.BlockSpec((t_new, ATT_W), lambda b: (b, 0))
    new_kv = pl.BlockSpec((None, t_new * N_HEADS, w), lambda b: (layer, b, 0))
    old = pl.BlockSpec((None, None, rows, w), lambda b: (layer, b, 0, 0))
    return pl.pallas_call(
        functools.partial(_attn_sample_body, t_new=t_new, past=past, out_scale=out_scale),
        grid=(batch,),
        in_specs=[pl.BlockSpec(memory_space=pltpu.SMEM), new, new_kv, new_kv, old, old, _const_spec((1, w))],
        out_specs=new,
        out_shape=jax.ShapeDtypeStruct((batch * t_new, ATT_W), F32),
        compiler_params=_cparams(("parallel",)),
        name="attn_sample",
    )(lam, bq, bk, bv, cache_k.reshape(depth, batch, rows, w), cache_v.reshape(depth, batch, rows, w),
      dn_g.reshape(1, w))


def _to_head_pairs(s):
    b = s.shape[0]
    r = s.reshape(b, N_HEADS // 2, 2, HEAD_DIM, HEAD_DIM)
    return jnp.swapaxes(r, 2, 3).reshape(b, N_HEADS // 2, HEAD_DIM, 2 * HEAD_DIM)


def _from_head_pairs(p):
    b = p.shape[0]
    r = p.reshape(b, N_HEADS // 2, HEAD_DIM, 2, HEAD_DIM)
    return jnp.swapaxes(r, 2, 3).reshape(b, N_HEADS, HEAD_DIM, HEAD_DIM)


def _rearrange_proj(a):
    gates = lambda c0: jnp.repeat(a[..., c0:c0 + N_HEADS], HEAD_DIM, axis=-1)
    out = jnp.concatenate([a[..., A_QK0:A_I0], a[..., A_O0:B_Q0], gates(A_I0), gates(A_F0), a[..., B_Q0:N_IN]],
                          axis=-1)
    assert out.shape[-1] == N_CAT
    return out


def _consts(hgrn_len, mlstm_len):
    r = jnp.arange(mlstm_len)
    tri = (r[None, :] <= r[:, None]).astype(BF16)
    r = jnp.arange(hgrn_len)
    tri16 = ((r[None, :] <= r[:, None]) & (r[None, :] // SUB == r[:, None] // SUB)).astype(BF16)
    hh = jnp.arange(HB_W) // HEAD_DIM
    eblk = (hh[None, :] == hh[:, None]).astype(BF16)
    lane = jnp.arange(2 * HEAD_DIM)
    src = jnp.arange(SUB)
    esel = ((hh[None, :, None] * SUB + src[:, None, None]) == lane[None, None, :]).astype(BF16)
    erep = ((lane[None, :, None] // SUB == jnp.arange(N_HEADS)[:, None, None])
            & (lane[None, :, None] % SUB == r[None, None, :] % SUB)).astype(BF16)
    return tri, tri16, eblk, esel, erep


def _group(x, depth, layer_w, states, cache, batch, t_len, alpha, consts):
    tri, tri16, eblk, esel, erep = consts
    prompt = cache is None
    outs = []
    kv_bufs = None
    for l in range(depth):
        w = layer_w[l]
        conv0, c0, n0, m0, s0 = states(l)
        x = _ffn_ln(x, w["f_in0"], w["f_out0"], w["ln_g"][0], w["ln_b"][0], alpha)
        res = _in_proj(x, w["w_cat"], w["b_cat"], prompt, l, depth, kv_bufs)
        proj = dict(zip(_PROJ_OUTS, res[:len(_PROJ_OUTS)]))
        bkb = res[len(_PROJ_OUTS)]
        kv_bufs = (proj["bk"], proj["bv"])

        conv0_p = jnp.pad(conv0, ((0, 0), (CONV_PAD - (CONV_W - 1), 0), (0, 0)))
        ya, c1, n1, m1, convn = _mlstm(proj, w["conv_w"], w["conv_b"], conv0_p, _to_head_pairs(c0),
                                       n0.reshape(batch, 1, HB_W),
                                       jnp.repeat(m0, HEAD_DIM, axis=-1).reshape(batch, 1, HB_W),
                                       tri, eblk, batch, t_len)
        lam_init = 0.8 - 0.6 * math.exp(-0.3 * l)
        if prompt:
            yb = _attn_prompt(w["lam"], proj["bq"], bkb, res[len(_PROJ_OUTS) + 1], w["dn_g"], batch, t_len,
                              1.0 - lam_init)
        else:
            yb = _attn_sample(w["lam"], proj["bq"], proj["bk"], proj["bv"], cache[0], cache[1], l, w["dn_g"],
                              batch, t_len, 1.0 - lam_init)
        yc, s1 = _hgrn(proj["c"], w["lb"], w["hn_g"], _to_head_pairs(s0), tri16, eblk, esel, erep, batch, t_len)
        x = _out_proj_ffn(ya, yb, yc, x, w["w_out"], w["ln_g"][1], w["ln_b"][1],
                          w["f_in1"], w["f_out1"], w["ln_g"][2], w["ln_b"][2], alpha)
        outs.append((_from_head_pairs(c1),
                     n1.reshape(batch, N_HEADS, HEAD_DIM),
                     m1.reshape(batch, N_HEADS, HEAD_DIM)[:, :, 0],
                     convn[:, CONV_PAD - (CONV_W - 1):, :],
                     _from_head_pairs(s1)))
    kv = tuple(buf.reshape(depth, batch, t_len, N_HEADS, 2 * HEAD_DIM) for buf in kv_bufs)
    return x, kv + tuple(jnp.stack(a) for a in zip(*outs))


def kernel(x_prompt, x_sample, cache_diff_k, cache_diff_v, state_mlstm_c, state_mlstm_n, state_mlstm_m,
           state_mlstm_conv, state_hgrn_s, w_in, b_in, w_out, mlstm_conv_w, mlstm_conv_b, diff_lambda,
           diff_norm_g, hgrn_lb_logits, hgrn_norm_g, ffn_w_in, ffn_w_out, ln_g, ln_b):
    depth = w_in.shape[0]
    alpha = (2.0 * depth) ** 0.25
    batch, seq, d_model = x_prompt.shape
    dec_batch, dec_seq, _ = x_sample.shape
    past = cache_diff_k.shape[2]

    p_lb = jax.nn.softmax(hgrn_lb_logits.astype(F32), axis=0)
    lower_bounds = jnp.cumsum(p_lb, axis=0) - p_lb[0]
    w_cat = _rearrange_proj(w_in.astype(BF16))
    b_cat = _rearrange_proj(b_in).reshape(depth, 1, N_CAT)
    lp = diff_lambda.astype(F32)
    lam_init = jnp.asarray([0.8 - 0.6 * math.exp(-0.3 * l) for l in range(depth)], F32)
    lam = (jnp.exp(jnp.sum(lp[:, 0] * lp[:, 1], axis=-1)) - jnp.exp(jnp.sum(lp[:, 2] * lp[:, 3], axis=-1))
           + lam_init)
    f_in = ffn_w_in.astype(BF16)
    f_out = ffn_w_out.astype(BF16)
    w_out_b = w_out.astype(BF16)
    layer_w = [dict(w_cat=w_cat[l], b_cat=b_cat[l],
                    conv_w=mlstm_conv_w[l], conv_b=mlstm_conv_b[l].reshape(1, 2 * HB_W),
                    lam=lam[l].reshape(1, 1), dn_g=diff_norm_g[l],
                    lb=lower_bounds[l].reshape(1, HB_W),
                    hn_g=jnp.tile(hgrn_norm_g[l], N_HEADS).reshape(1, HB_W),
                    w_out=w_out_b[l], f_in0=f_in[l, 0], f_out0=f_out[l, 0], f_in1=f_in[l, 1], f_out1=f_out[l, 1],
                    ln_g=ln_g[l], ln_b=ln_b[l]) for l in range(depth)]
    def zero_states(_):
        return (jnp.zeros((batch, CONV_W - 1, 2 * HB_W), F32), jnp.zeros((batch, N_HEADS, HEAD_DIM, HEAD_DIM), F32),
                jnp.zeros((batch, N_HEADS, HEAD_DIM), F32), jnp.zeros((batch, N_HEADS), F32),
                jnp.zeros((batch, N_HEADS, HEAD_DIM, HEAD_DIM), F32))

    def carried_states(l):
        return (state_mlstm_conv[l], state_mlstm_c[l], state_mlstm_n[l], state_mlstm_m[l], state_hgrn_s[l])

    y_p, p_out = _group(x_prompt.reshape(batch * seq, d_model), depth, layer_w, zero_states, None,
                        batch, seq, alpha, _consts(_hgrn_rows(seq)[1], _mlstm_rows(seq)[1]))
    cache = (cache_diff_k, cache_diff_v)
    y_s, s_out = _group(x_sample.reshape(dec_batch * dec_seq, d_model), depth, layer_w, carried_states, cache,
                        dec_batch, dec_seq, alpha, _consts(_hgrn_rows(dec_seq)[1], _mlstm_rows(dec_seq)[1]))
    return (y_p.reshape(batch, seq, d_model), y_s.reshape(dec_batch, dec_seq, d_model)) + p_out + s_out
```

```python
import functools
import math

import jax
import jax.numpy as jnp
from jax import lax
from jax.experimental import pallas as pl
from jax.experimental.pallas import tpu as pltpu

F32 = jnp.float32
BF16 = jnp.bfloat16

HEAD_DIM = 64
N_HEADS = 4
CONV_W = 4
CONV_PAD = 8
CHUNK = 64
HB_W = N_HEADS * HEAD_DIM
ATT_W = N_HEADS * 2 * HEAD_DIM
LN_EPS = 1e-5
RMS_EPS = 1e-6
NEG_BIG = -1e30
SCAN_ROWS = 128
MLSTM_ROWS = 128
HGRN_ROWS = 256
SUB = 16
LOG2E = 1.4426950408889634
VT_ROWS = 2 * HEAD_DIM + 16
Q_STRIP = 256
ATT_TILE = 2048
FFN_CHUNK = 256
PROJ_TILE = 512
FFN_TILE = 1024
FFN_ROWS = 512
V7X_VMEM_LIMIT = 56 * 1024 * 1024

A_QK0 = 0
A_V0 = A_QK0 + 2 * HB_W
A_I0 = A_V0 + HB_W
A_F0 = A_I0 + N_HEADS
A_O0 = A_F0 + N_HEADS
B_Q0 = A_O0 + HB_W
B_K0 = B_Q0 + ATT_W
B_V0 = B_K0 + ATT_W
C_Q0 = B_V0 + ATT_W
N_IN = C_Q0 + 4 * HB_W

_SEGS = {}
_off = 0
for _name, _w in (("aqk", 2 * HB_W), ("av", HB_W), ("ao", HB_W), ("gi", HB_W), ("gf", HB_W),
                  ("bq", ATT_W), ("bk", ATT_W), ("bv", ATT_W), ("c", 4 * HB_W)):
    _SEGS[_name] = (_off, _w)
    _off += _w
N_CAT = _off


def _tile(n, pref, mult=8):
    t = min(pref, n)
    while t > mult and (n % t or t % mult):
        t -= mult
    assert n % t == 0 and t % mult == 0, (n, pref, mult)
    return t


def _cparams(sem):
    return pltpu.CompilerParams(dimension_semantics=sem, vmem_limit_bytes=V7X_VMEM_LIMIT)


def _const_spec(shape, single=False):
    nd = len(shape)
    if single:
        return pl.BlockSpec(shape, lambda *_: (0,) * nd, pipeline_mode=pl.Buffered(1))
    return pl.BlockSpec(shape, lambda *_: (0,) * nd)


def _dot(a, b):
    return jnp.dot(a, b, preferred_element_type=F32)


def _dot_nt(a, b):
    return lax.dot_general(a, b, (((1,), (1,)), ((), ())), preferred_element_type=F32)


def _split3(x):
    a = x.astype(BF16)
    r = x - a.astype(F32)
    b = r.astype(BF16)
    c = (r - b.astype(F32)).astype(BF16)
    return a, b, c


def _dot_exact_left(mat01, x):
    a, b, c = _split3(x)
    return _dot(mat01, a) + _dot(mat01, b) + _dot(mat01, c)


def _dot_exact_right(x, mat01):
    a, b, c = _split3(x)
    return _dot(a, mat01) + _dot(b, mat01) + _dot(c, mat01)


def _layer_norm(y, g, b):
    mu = jnp.mean(y, axis=-1, keepdims=True)
    d = y - mu
    var = jnp.mean(d * d, axis=-1, keepdims=True)
    return d * lax.rsqrt(var + LN_EPS) * g + b


def _pad_rows(x, rows):
    if x.shape[0] == rows:
        return x
    return jnp.concatenate([x, jnp.zeros((rows - x.shape[0],) + x.shape[1:], x.dtype)], axis=0)


def _swiglu_ln(x, wi_ref, wo_ref, g_ref, b_ref, o_ref, acc_ref, alpha, d_ff, tf):
    rb = min(FFN_ROWS, x.shape[0])
    n_blocks = x.shape[0] // rb

    def finish(r):
        rows = slice(r * rb, (r + 1) * rb)
        y = alpha * x[rows] + 0.5 * acc_ref[rows, :]
        o_ref[rows, :] = _layer_norm(y, g_ref[...], b_ref[...])

    for r in range(n_blocks):
        rows = slice(r * rb, (r + 1) * rb)
        xb = x[rows].astype(BF16)
        for c in range(d_ff // tf):
            gate = _dot(xb, wi_ref[:, c * tf:(c + 1) * tf])
            up = _dot(xb, wi_ref[:, d_ff + c * tf:d_ff + (c + 1) * tf])
            h = (gate * jax.nn.sigmoid(gate) * up).astype(BF16)
            part = _dot(h, wo_ref[c * tf:(c + 1) * tf, :])
            if c == 0:
                acc_ref[rows, :] = part
            else:
                acc_ref[rows, :] += part
            if c == 0 and r > 0:
                finish(r - 1)
    finish(n_blocks - 1)


def _ffn_ln_body(x_ref, wi_ref, wo_ref, g_ref, b_ref, o_ref, acc_ref, *, alpha, d_ff, tf):
    _swiglu_ln(x_ref[...], wi_ref, wo_ref, g_ref, b_ref, o_ref, acc_ref, alpha, d_ff, tf)


def _ffn_ln(x, wi, wo, g, b, alpha):
    n, d = x.shape
    d_ff = wo.shape[0]
    tm = _tile(n, FFN_TILE)
    tf = _tile(d_ff, FFN_CHUNK, 128)
    return pl.pallas_call(
        functools.partial(_ffn_ln_body, alpha=alpha, d_ff=d_ff, tf=tf),
        grid=(n // tm,),
        in_specs=[pl.BlockSpec((tm, d), lambda i: (i, 0)),
                  _const_spec(wi.shape, single=True), _const_spec(wo.shape, single=True),
                  _const_spec((1, d)), _const_spec((1, d))],
        out_specs=pl.BlockSpec((tm, d), lambda i: (i, 0)),
        out_shape=jax.ShapeDtypeStruct((n, d), F32),
        scratch_shapes=[pltpu.VMEM((tm, d), F32)],
        compiler_params=_cparams(("parallel",)),
        name="ffn_ln",
    )(x, wi, wo, g.reshape(1, d), b.reshape(1, d))


_PROJ_OUTS = ("aqk", "av", "ao", "gi", "gf", "bq", "bk", "bv", "c")


def _in_proj_body(*refs, emit_vt, n_alias):
    x_ref, w_ref, bias_ref = refs[:3]
    out_refs = refs[3 + n_alias:]
    xb = x_ref[...].astype(BF16)
    tm = xb.shape[0]
    outs = dict(zip(_PROJ_OUTS, out_refs))
    dv = 2 * HEAD_DIM
    ones_rows = jnp.where(lax.broadcasted_iota(jnp.int32, (VT_ROWS - dv, Q_STRIP), 0) == 0, 1.0, 0.0).astype(BF16)
    for name in _PROJ_OUTS:
        c0, w = _SEGS[name]
        for s in range(0, w, 2 * HB_W):
            e = min(s + 2 * HB_W, w)
            val = _dot(xb, w_ref[:, c0 + s:c0 + e]) + bias_ref[:, c0 + s:c0 + e]
            if name in ("bk", "bv"):
                kv = outs[name]
                if n_alias == 0:
                    kv[1:] = jnp.zeros((kv.shape[0] - 1,) + kv.shape[1:], F32)
                    kv = kv.at[0]
                for h in range(N_HEADS):
                    kv[pl.ds(h, tm, stride=N_HEADS), :] = val[:, h * 2 * HEAD_DIM:(h + 1) * 2 * HEAD_DIM]
            else:
                outs[name][:, s:e] = val
            if name == "bk":
                out_refs[len(_PROJ_OUTS)][:, s:e] = val.astype(BF16)
            if name == "bv" and emit_vt:
                vt_ref = out_refs[len(_PROJ_OUTS) + 1]
                for h in range(N_HEADS):
                    vt = val[:, h * dv:(h + 1) * dv].T.astype(BF16)
                    for j in range(tm // Q_STRIP):
                        vt_ref[h, j, 0:dv, :] = vt[:, j * Q_STRIP:(j + 1) * Q_STRIP]
                        vt_ref[h, j, dv:VT_ROWS, :] = ones_rows


def _in_proj(x, w_cat, b_cat, emit_vt, layer, depth, kv_bufs):
    n, d = x.shape
    tm = _tile(n, PROJ_TILE, Q_STRIP if emit_vt else 8)
    out_shape, out_specs = [], []
    for k in _PROJ_OUTS:
        if k in ("bk", "bv"):
            out_shape.append(jax.ShapeDtypeStruct((depth, n * N_HEADS, 2 * HEAD_DIM), F32))
            if kv_bufs is None:
                out_specs.append(pl.BlockSpec((depth, tm * N_HEADS, 2 * HEAD_DIM), lambda i: (0, i, 0)))
            else:
                out_specs.append(pl.BlockSpec((None, tm * N_HEADS, 2 * HEAD_DIM), lambda i: (layer, i, 0)))
        else:
            out_shape.append(jax.ShapeDtypeStruct((n, _SEGS[k][1]), F32))
            out_specs.append(pl.BlockSpec((tm, _SEGS[k][1]), lambda i: (i, 0)))
    out_shape.append(jax.ShapeDtypeStruct((n, ATT_W), BF16))
    out_specs.append(pl.BlockSpec((tm, ATT_W), lambda i: (i, 0)))
    if emit_vt:
        out_shape.append(jax.ShapeDtypeStruct((N_HEADS, n // Q_STRIP, VT_ROWS, Q_STRIP), BF16))
        out_specs.append(pl.BlockSpec((N_HEADS, tm // Q_STRIP, VT_ROWS, Q_STRIP), lambda i: (0, i, 0, 0)))
    in_specs = [pl.BlockSpec((tm, d), lambda i: (i, 0)), _const_spec(w_cat.shape, single=True),
                _const_spec(b_cat.shape)]
    args = [x, w_cat, b_cat]
    aliases = {}
    for buf, name in zip(kv_bufs or (), ("bk", "bv")):
        aliases[len(args)] = _PROJ_OUTS.index(name)
        in_specs.append(pl.BlockSpec(memory_space=pl.ANY))
        args.append(buf)
    return pl.pallas_call(
        functools.partial(_in_proj_body, emit_vt=emit_vt, n_alias=len(aliases)),
        grid=(n // tm,),
        in_specs=in_specs,
        out_specs=out_specs,
        out_shape=out_shape,
        input_output_aliases=aliases,
        compiler_params=_cparams(("parallel",)),
        name="in_proj",
    )(*args)


def _out_proj_ffn_body(ya_ref, yb_ref, yc_ref, x_ref, w_ref, g1_ref, b1_ref, wi_ref, wo_ref, g2_ref, b2_ref,
                       o_ref, acc_ref, *, alpha, d_ff, tf):
    y = _dot(ya_ref[...].astype(BF16), w_ref[0:HB_W, :])
    y += _dot(yb_ref[...].astype(BF16), w_ref[HB_W:HB_W + ATT_W, :])
    y += _dot(yc_ref[...].astype(BF16), w_ref[HB_W + ATT_W:, :])
    x = _layer_norm(alpha * x_ref[...] + y, g1_ref[...], b1_ref[...])
    _swiglu_ln(x, wi_ref, wo_ref, g2_ref, b2_ref, o_ref, acc_ref, alpha, d_ff, tf)


def _out_proj_ffn(ya, yb, yc, x, w, g1, b1, wi, wo, g2, b2, alpha):
    n, d = x.shape
    d_ff = wo.shape[0]
    tm = _tile(n, FFN_TILE)
    tf = _tile(d_ff, FFN_CHUNK, 128)
    row = lambda width: pl.BlockSpec((tm, width), lambda i: (i, 0))
    vec = _const_spec((1, d))
    return pl.pallas_call(
        functools.partial(_out_proj_ffn_body, alpha=alpha, d_ff=d_ff, tf=tf),
        grid=(n // tm,),
        in_specs=[row(HB_W), row(ATT_W), row(HB_W), row(d), _const_spec(w.shape, single=True), vec, vec,
                  _const_spec(wi.shape, single=True), _const_spec(wo.shape, single=True), vec, vec],
        out_specs=row(d),
        out_shape=jax.ShapeDtypeStruct((n, d), F32),
        scratch_shapes=[pltpu.VMEM((tm, d), F32)],
        compiler_params=_cparams(("parallel",)),
        name="out_proj_ffn",
    )(ya, yb, yc, x, w, g1.reshape(1, d), b1.reshape(1, d), wi, wo, g2.reshape(1, d), b2.reshape(1, d))


def _mlstm_body(qk_ref, v_ref, o_ref, gi_ref, gf_ref, cw_ref, cb_ref, conv0_ref, c0_ref, n0_ref, m0_ref,
                tri_ref, eblk_ref,
                ya_ref, c1_ref, n1_ref, m1_ref, convn_ref,
                up_ref, c_ref, n_ref, m_ref, *, rows):
    step = pl.program_id(1)
    low = lax.broadcasted_iota(jnp.int32, (1, 2 * HEAD_DIM), 1) < HEAD_DIM
    pw = 2 * HEAD_DIM
    L = tri_ref.shape[0]

    @pl.when(step == 0)
    def _():
        up_ref[0:CONV_PAD, :] = conv0_ref[...]
        c_ref[...] = jnp.zeros(c_ref.shape, F32)
        for i in range(N_HEADS // 2):
            pair = c0_ref[i]
            c_ref[i * pw:i * pw + HEAD_DIM, i * pw:(i + 1) * pw] = jnp.where(low, pair, 0.0)
            c_ref[i * pw + HEAD_DIM:(i + 1) * pw, i * pw:(i + 1) * pw] = jnp.where(low, 0.0, pair)
        n_ref[...] = n0_ref[...]
        m_ref[...] = m0_ref[...]

    gi = _pad_rows(gi_ref[...], L)
    flog = jax.nn.log_sigmoid(_pad_rows(gf_ref[...], L))
    if rows < L:
        live = lax.broadcasted_iota(jnp.int32, (L, 1), 0) < rows
        gi = jnp.where(live, gi, NEG_BIG)
        flog = jnp.where(live, flog, 0.0)
    b = _dot_exact_left(tri_ref[...], flog)
    b_t = b.T
    gi_t = gi.T

    u = _pad_rows(qk_ref[...], L)
    up_ref[CONV_PAD:CONV_PAD + L, :] = u
    y = cb_ref[...]
    first = CONV_PAD - (CONV_W - 1)
    for j in range(CONV_W):
        y = y + cw_ref[j:j + 1, :] * up_ref[first + j:first + j + L, :]
    convn_ref[...] = up_ref[rows:rows + CONV_PAD, :]
    up_ref[0:CONV_PAD, :] = up_ref[L:L + CONV_PAD, :]
    qk = y * jax.nn.sigmoid(y)
    q = qk[:, :HB_W] * (HEAD_DIM ** -0.5)
    k = qk[:, HB_W:]
    v = _pad_rows(v_ref[...], L)
    kb = k.astype(BF16)
    vb = v.astype(BF16)
    qb = q.astype(BF16)

    lane_head = lax.broadcasted_iota(jnp.int32, (1, HB_W), 1) // HEAD_DIM
    qk_scores = [_dot_nt(jnp.where(lane_head == h, q, 0.0).astype(BF16), kb) for h in range(N_HEADS)]
    c_prev = c_ref[...]
    n_prev = n_ref[...]
    eblk = eblk_ref[...]
    num_state = _dot(qb, c_prev.astype(BF16))
    den_state = _dot((q * n_prev).astype(BF16), eblk)

    m_prev = m_ref[...]
    g = b + m_prev

    t_idx = lax.broadcasted_iota(jnp.int32, (L, L), 0)
    s_idx = lax.broadcasted_iota(jnp.int32, (L, L), 1)
    causal = s_idx <= t_idx

    mrow_hb = jnp.zeros((L, HB_W), F32)
    wg_hb = jnp.zeros((L, HB_W), F32)
    den_hb = jnp.zeros((L, HB_W), F32)
    num_hb = jnp.zeros((L, HB_W), F32)
    for h in range(N_HEADS):
        sel = lane_head == h
        c0 = h * HEAD_DIM
        dmat = b[:, c0:c0 + 1] - b_t[c0:c0 + 1, :] + gi_t[c0:c0 + 1, :]
        dmat = jnp.where(causal, dmat, NEG_BIG)
        gcol = g[:, c0:c0 + 1]
        mrow = jnp.maximum(gcol, jnp.max(dmat, axis=1, keepdims=True))
        wd = jnp.exp(dmat - mrow)
        wg = jnp.exp(gcol - mrow)
        qkw = qk_scores[h] * wd
        den = jnp.sum(qkw, axis=1, keepdims=True)
        num = _dot(qkw.astype(BF16), vb)
        mrow_hb = jnp.where(sel, mrow, mrow_hb)
        wg_hb = jnp.where(sel, wg, wg_hb)
        den_hb = jnp.where(sel, den, den_hb)
        num_hb = jnp.where(sel, num, num_hb)

    num_hb = wg_hb * num_state + num_hb
    den_hb = wg_hb * den_state + den_hb
    hout = num_hb / jnp.maximum(jnp.abs(den_hb), jnp.exp(-mrow_hb))
    ya = jax.nn.sigmoid(_pad_rows(o_ref[...], L)) * hout
    ya_ref[...] = ya[:rows]

    m_new = mrow_hb[L - 1:L, :]
    b_last = b[L - 1:L, :]
    decay = jnp.exp(b_last + m_prev - m_new)
    ws = jnp.exp(b_last - b + gi - m_new)
    kw = k * ws
    upd = _dot(kw.T.astype(BF16), vb) * eblk.astype(F32)
    c_new = decay * c_prev + upd
    n_new = decay * n_prev + jnp.sum(kw, axis=0, keepdims=True)
    c_ref[...] = c_new
    n_ref[...] = n_new
    m_ref[...] = m_new

    @pl.when(step == pl.num_programs(1) - 1)
    def _():
        for i in range(N_HEADS // 2):
            c1_ref[i] = jnp.where(low, c_new[i * pw:i * pw + HEAD_DIM, i * pw:(i + 1) * pw],
                                  c_new[i * pw + HEAD_DIM:(i + 1) * pw, i * pw:(i + 1) * pw])
        n1_ref[...] = n_new
        m1_ref[...] = m_new


def _mlstm_rows(t_len):
    rows = _tile(t_len, MLSTM_ROWS, 8)
    return rows, max(rows, SCAN_ROWS)


def _mlstm(proj, conv_w, conv_b, conv0, c0_pk, n0, m0_hb, tri, eblk, batch, t_len):
    rows, length = _mlstm_rows(t_len)
    assert tri.shape == (length, length)
    steps = t_len // rows
    tok = lambda width: pl.BlockSpec((rows, width), lambda bi, ci: (bi * steps + ci, 0))
    per_b = lambda shape: pl.BlockSpec((None,) + shape, lambda bi, ci: (bi,) + (0,) * len(shape))
    return pl.pallas_call(
        functools.partial(_mlstm_body, rows=rows),
        grid=(batch, steps),
        in_specs=[tok(2 * HB_W), tok(HB_W), tok(HB_W), tok(HB_W), tok(HB_W),
                  _const_spec(conv_w.shape), _const_spec(conv_b.shape),
                  per_b((CONV_PAD, 2 * HB_W)), per_b((N_HEADS // 2, HEAD_DIM, 2 * HEAD_DIM)), per_b((1, HB_W)),
                  per_b((1, HB_W)), _const_spec(tri.shape), _const_spec(eblk.shape)],
        out_specs=[tok(HB_W), per_b((N_HEADS // 2, HEAD_DIM, 2 * HEAD_DIM)), per_b((1, HB_W)), per_b((1, HB_W)),
                   per_b((CONV_PAD, 2 * HB_W))],
        out_shape=[jax.ShapeDtypeStruct((batch * t_len, HB_W), F32),
                   jax.ShapeDtypeStruct((batch, N_HEADS // 2, HEAD_DIM, 2 * HEAD_DIM), F32),
                   jax.ShapeDtypeStruct((batch, 1, HB_W), F32),
                   jax.ShapeDtypeStruct((batch, 1, HB_W), F32),
                   jax.ShapeDtypeStruct((batch, CONV_PAD, 2 * HB_W), F32)],
        scratch_shapes=[pltpu.VMEM((length + CONV_PAD, 2 * HB_W), F32),
                        pltpu.VMEM((HB_W, HB_W), F32),
                        pltpu.VMEM((1, HB_W), F32),
                        pltpu.VMEM((1, HB_W), F32)],
        compiler_params=_cparams(("parallel", "arbitrary")),
        name="mlstm",
    )(proj["aqk"], proj["av"], proj["ao"], proj["gi"], proj["gf"], conv_w, conv_b, conv0, c0_pk, n0, m0_hb,
      tri, eblk)


def _hgrn_body(c_ref, lb_ref, hng_ref, s0_ref, tri16_ref, eblk_ref, esel_ref, erep_ref, yc_ref, s1_ref, s_ref,
               *, rows, length, seqs):
    step = pl.program_id(1)
    L = length

    @pl.when(step == 0)
    def _():
        s_ref[...] = s0_ref[...]

    cin = _pad_rows(c_ref[...], L)
    q = cin[:, 0:HB_W]
    f_pre = cin[:, HB_W:2 * HB_W]
    v = cin[:, 2 * HB_W:3 * HB_W]
    gate = cin[:, 3 * HB_W:]
    lb = lb_ref[...]
    forget = lb + (1.0 - lb) * jax.nn.sigmoid(f_pre)
    logf = jnp.log(forget)
    key = 1.0 - forget
    if rows < L:
        live = lax.broadcasted_iota(jnp.int32, (L, 1), 0) < rows
        logf = jnp.where(live, logf, 0.0)
        key = jnp.where(live, key, 0.0)
    bl = _dot_exact_left(tri16_ref[...], logf) * LOG2E
    eblk = eblk_ref[...]
    groups = L // SUB
    t_loc = lax.broadcasted_iota(jnp.int32, (L, 1), 0) % SUB
    vb = v.astype(BF16)

    def from_source(x, s):
        picked = x.reshape(groups, SUB, HB_W)[:, s:s + 1, :]
        return jnp.broadcast_to(picked, (groups, SUB, HB_W)).reshape(L, HB_W)

    n_sub = max(rows // SUB, 1)
    qd = (q * jnp.exp2(bl)).astype(BF16)
    bl_t = bl.T
    low = lax.broadcasted_iota(jnp.int32, (1, 2 * HEAD_DIM), 1) < HEAD_DIM
    zero_b = jnp.zeros((HEAD_DIM, 2 * HEAD_DIM), BF16)

    def contribution(j):
        r0 = j * SUB
        b_last = bl[r0 + SUB - 1:r0 + SUB, :]
        kw_j = (key[r0:r0 + SUB] * jnp.exp2(b_last - bl[r0:r0 + SUB])).astype(BF16)
        upd = lax.dot_general(kw_j, vb[r0:r0 + SUB], (((0,), (0,)), ((), ())),
                              preferred_element_type=F32)
        dec = jnp.exp2(bl_t[:, r0 + SUB - 1:r0 + SUB])
        pair = []
        for i in range(2):
            rk = 2 * i * HEAD_DIM
            u = jnp.where(low, upd[rk:rk + HEAD_DIM, rk:rk + 2 * HEAD_DIM],
                          upd[rk + HEAD_DIM:rk + 2 * HEAD_DIM, rk:rk + 2 * HEAD_DIM])
            d = jnp.where(low, dec[rk:rk + HEAD_DIM], dec[rk + HEAD_DIM:rk + 2 * HEAD_DIM])
            pair.append((d, u))
        return pair

    per_seq = n_sub // seqs
    states = [[s_ref[q, 0], s_ref[q, 1]] for q in range(seqs)]
    o_parts = []

    def advance(j):
        r0 = j * SUB
        pk = states[j // per_seq]
        pb = [x.astype(BF16) for x in pk]
        s_bd = jnp.concatenate([
            jnp.concatenate([jnp.where(low, pb[0], 0.0).astype(BF16), zero_b], axis=1),
            jnp.concatenate([jnp.where(low, 0.0, pb[0]).astype(BF16), zero_b], axis=1),
            jnp.concatenate([zero_b, jnp.where(low, pb[1], 0.0).astype(BF16)], axis=1),
            jnp.concatenate([zero_b, jnp.where(low, 0.0, pb[1]).astype(BF16)], axis=1)], axis=0)
        o_parts.append(_dot(qd[r0:r0 + SUB], s_bd))
        pk[:] = [d * p + u for p, (d, u) in zip(pk, contrib[j])]

    contrib = []
    att = jnp.zeros((L, 2 * HEAD_DIM), F32)
    for s in range(SUB):
        w = jnp.exp2(bl - from_source(bl, s))
        p = jnp.where(t_loc >= s, q * from_source(key, s) * w, 0.0)
        att = att + _dot(p.astype(BF16), esel_ref[s])
        while len(contrib) < ((s + 1) * n_sub) // SUB:
            contrib.append(contribution(len(contrib)))
    for j in range(n_sub):
        advance(j)
    for q_i in range(seqs):
        s_ref[q_i, 0] = states[q_i][0]
        s_ref[q_i, 1] = states[q_i][1]
    attb = att.astype(BF16)
    same_sub = (lax.broadcasted_iota(jnp.int32, (L, L), 0) // SUB
                == lax.broadcasted_iota(jnp.int32, (L, L), 1) // SUB)
    lane_head = lax.broadcasted_iota(jnp.int32, (1, HB_W), 1) // HEAD_DIM
    o = jnp.zeros((L, HB_W), F32)
    for h in range(N_HEADS):
        full = jnp.where(same_sub, _dot(attb, erep_ref[h]), 0.0)
        o = jnp.where(lane_head == h, _dot(full.astype(BF16), vb), o)

    o_inter = jnp.concatenate(o_parts, axis=0)
    if rows < L:
        o_inter = _pad_rows(o_inter, L)
    o = o + o_inter

    ms = _dot_exact_right(o * o, eblk) * (1.0 / HEAD_DIM)
    yc = o * lax.rsqrt(ms + RMS_EPS) * hng_ref[...] * (gate * jax.nn.sigmoid(gate))
    yc_ref[...] = yc[:rows]

    @pl.when(step == pl.num_programs(1) - 1)
    def _():
        for q_i in range(seqs):
            s1_ref[q_i, 0] = states[q_i][0]
            s1_ref[q_i, 1] = states[q_i][1]


def _hgrn_rows(t_len, batch=1):
    if t_len < SCAN_ROWS and t_len % SUB == 0 and SCAN_ROWS % t_len == 0 and batch % (SCAN_ROWS // t_len) == 0:
        return SCAN_ROWS, SCAN_ROWS, SCAN_ROWS // t_len
    rows = _tile(t_len, HGRN_ROWS, SUB)
    return rows, max(rows, SCAN_ROWS), 1


def _hgrn(c_all, lb, hn_g, s0_pk, tri16, eblk, esel, erep, batch, t_len):
    rows, length, seqs = _hgrn_rows(t_len, batch)
    n = batch * t_len
    steps = n // (batch // seqs) // rows
    pk_spec = pl.BlockSpec((seqs, 2, HEAD_DIM, 2 * HEAD_DIM), lambda bi, ci: (bi, 0, 0, 0))
    return pl.pallas_call(
        functools.partial(_hgrn_body, rows=rows, length=length, seqs=seqs),
        grid=(batch // seqs, steps),
        in_specs=[pl.BlockSpec((rows, 4 * HB_W), lambda bi, ci: (bi * steps + ci, 0)),
                  _const_spec((1, HB_W)), _const_spec((1, HB_W)), pk_spec,
                  _const_spec(tri16.shape), _const_spec(eblk.shape), _const_spec(esel.shape),
                  _const_spec(erep.shape)],
        out_specs=[pl.BlockSpec((rows, HB_W), lambda bi, ci: (bi * steps + ci, 0)), pk_spec],
        out_shape=[jax.ShapeDtypeStruct((n, HB_W), F32),
                   jax.ShapeDtypeStruct((batch, 2, HEAD_DIM, 2 * HEAD_DIM), F32)],
        scratch_shapes=[pltpu.VMEM((seqs, 2, HEAD_DIM, 2 * HEAD_DIM), F32)],
        compiler_params=_cparams(("parallel", "arbitrary")),
        name="hgrn",
    )(c_all, lb, hn_g, s0_pk, tri16, eblk, esel, erep)


def _attn_prompt_body(qi_tab, ki_tab, lam_ref, q_ref, k_ref, vt_ref, g_ref, o_ref, qx_ref, m_ref, acc_ref,
                      s_ref, bmax_ref, *, tq, out_scale):
    pair = pl.program_id(2)
    qi = qi_tab[pair]
    ki = ki_tab[pair]
    ns = tq // Q_STRIP
    dv = 2 * HEAD_DIM

    @pl.when(ki == 0)
    def _():
        qt = (q_ref[...] * (HEAD_DIM ** -0.5 * LOG2E)).T
        first = lax.broadcasted_iota(jnp.int32, (dv, 1), 0) < HEAD_DIM
        qa = jnp.where(first, qt, 0.0).astype(BF16)
        qb = jnp.where(first, 0.0, qt).astype(BF16)
        for st in range(ns):
            qx_ref[0, st] = qa[:, st * Q_STRIP:(st + 1) * Q_STRIP]
            qx_ref[1, st] = qb[:, st * Q_STRIP:(st + 1) * Q_STRIP]
        m_ref[...] = jnp.full(m_ref.shape, NEG_BIG, F32)
        acc_ref[...] = jnp.zeros(acc_ref.shape, F32)

    def scores(kb, st, diag):
        k = k_ref[kb * Q_STRIP:(kb + 1) * Q_STRIP, :]
        for mp in range(2):
            s = _dot(k, qx_ref[mp, st])
            if diag and st == kb:
                k_chunk = lax.broadcasted_iota(jnp.int32, (Q_STRIP, 1), 0) // CHUNK
                q_chunk = lax.broadcasted_iota(jnp.int32, (1, Q_STRIP), 1) // CHUNK
                s = jnp.where(k_chunk <= q_chunk, s, NEG_BIG)
            s_ref[kb % 2, mp, st] = s
            bmax_ref[kb % 2, mp, st] = jnp.max(s, axis=0, keepdims=True)

    def accumulate(kb, st):
        vt = vt_ref[kb]
        for mp in range(2):
            m_old = m_ref[mp, st]
            m_new = jnp.maximum(m_old, bmax_ref[kb % 2, mp, st])
            p = jnp.exp2(s_ref[kb % 2, mp, st] - m_new).astype(BF16)
            corr = jnp.exp2(m_old - m_new)
            acc_ref[mp, st] = corr * acc_ref[mp, st] + _dot(vt, p)
            m_ref[mp, st] = m_new

    def key_tile(diag):
        first_strip = (lambda kb: kb) if diag else (lambda kb: 0)
        for st in range(ns):
            scores(0, st, diag)
        for kb in range(ns):
            for st in range(first_strip(kb), ns):
                accumulate(kb, st)
                if kb + 1 < ns and st >= first_strip(kb + 1):
                    scores(kb + 1, st, diag)

    @pl.when(ki < qi)
    def _():
        key_tile(False)

    @pl.when(ki == qi)
    def _():
        key_tile(True)
        lam = lam_ref[0, 0]
        for st in range(ns):
            a1 = acc_ref[0, st]
            a2 = acc_ref[1, st]
            o = a1[:dv] / a1[dv:dv + 1] - lam * (a2[:dv] / a2[dv:dv + 1])
            ms = jnp.mean(o * o, axis=0, keepdims=True)
            o = o * lax.rsqrt(ms + RMS_EPS) * g_ref[...] * out_scale
            o_ref[st * Q_STRIP:(st + 1) * Q_STRIP, :] = o.T


def _attn_prompt(lam, bq, bkb, vt, dn_g, batch, t_len, out_scale):
    tq = _tile(t_len, ATT_TILE, Q_STRIP)
    nq = t_len // tq
    ns = tq // Q_STRIP
    n = batch * t_len
    w = 2 * HEAD_DIM
    pairs = [(qi, ki) for qi in range(nq) for ki in range(qi + 1)]
    qi_tab = jnp.asarray([p[0] for p in pairs], jnp.int32)
    ki_tab = jnp.asarray([p[1] for p in pairs], jnp.int32)
    qmap = lambda b, h, p, qt, kt: (b * nq + qt[p], h)
    kmap = lambda b, h, p, qt, kt: (b * nq + kt[p], h)
    vmap_ = lambda b, h, p, qt, kt: (h, b * nq + kt[p], 0, 0)
    grid_spec = pltpu.PrefetchScalarGridSpec(
        num_scalar_prefetch=2,
        grid=(batch, N_HEADS, len(pairs)),
        in_specs=[pl.BlockSpec(memory_space=pltpu.SMEM),
                  pl.BlockSpec((tq, w), qmap), pl.BlockSpec((tq, w), kmap),
                  pl.BlockSpec((None, ns, VT_ROWS, Q_STRIP), vmap_),
                  pl.BlockSpec((w, 1), lambda b, h, p, qt, kt: (0, 0))],
        out_specs=pl.BlockSpec((tq, w), qmap),
        scratch_shapes=[pltpu.VMEM((2, ns, w, Q_STRIP), BF16),
                        pltpu.VMEM((2, ns, 1, Q_STRIP), F32),
                        pltpu.VMEM((2, ns, VT_ROWS, Q_STRIP), F32),
                        pltpu.VMEM((2, 2, ns, Q_STRIP, Q_STRIP), F32),
                        pltpu.VMEM((2, 2, ns, 1, Q_STRIP), F32)])
    return pl.pallas_call(
        functools.partial(_attn_prompt_body, tq=tq, out_scale=out_scale),
        grid_spec=grid_spec,
        out_shape=jax.ShapeDtypeStruct((n, ATT_W), F32),
        compiler_params=_cparams(("parallel", "parallel", "arbitrary")),
        name="attn_prompt",
    )(qi_tab, ki_tab, lam, bq, bkb, vt, dn_g.reshape(w, 1))


def _attn_sample_body(lam_ref, q_ref, kn_ref, vn_ref, kp_ref, vp_ref, g_ref, o_ref, *, t_new, past, out_scale):
    w = 2 * HEAD_DIM
    first = lax.broadcasted_iota(jnp.int32, (1, w), 1) < HEAD_DIM
    for h in range(N_HEADS):
        cols = slice(h * w, (h + 1) * w)
        q = q_ref[:, cols] * (HEAD_DIM ** -0.5)
        q2 = jnp.concatenate([jnp.where(first, q, 0.0), jnp.where(first, 0.0, q)], axis=0).astype(BF16)
        kp = kp_ref[pl.ds(h, past, stride=N_HEADS), :].astype(BF16)
        vp = vp_ref[pl.ds(h, past, stride=N_HEADS), :].astype(BF16)
        s_p = _dot_nt(q2, kp)
        kn = kn_ref[pl.ds(h, t_new, stride=N_HEADS), :].astype(BF16)
        vn = vn_ref[pl.ds(h, t_new, stride=N_HEADS), :].astype(BF16)
        s_n = _dot_nt(q2, kn)
        m = jnp.maximum(jnp.max(s_p, axis=1, keepdims=True), jnp.max(s_n, axis=1, keepdims=True))
        p_p = jnp.exp(s_p - m)
        p_n = jnp.exp(s_n - m)
        l = jnp.sum(p_p, axis=1, keepdims=True) + jnp.sum(p_n, axis=1, keepdims=True)
        acc = _dot(p_p.astype(BF16), vp) + _dot(p_n.astype(BF16), vn)
        o2 = acc / l
        o = o2[:t_new] - lam_ref[0, 0] * o2[t_new:]
        ms = jnp.mean(o * o, axis=1, keepdims=True)
        o_ref[:, cols] = o * lax.rsqrt(ms + RMS_EPS) * g_ref[...] * out_scale


def _attn_sample(lam, bq, bk, bv, cache_k, cache_v, layer, dn_g, batch, t_new, out_scale):
    w = 2 * HEAD_DIM
    depth, _, past = cache_k.shape[:3]
    rows = past * N_HEADS
    new = pl.BlockSpec((t_new, ATT_W), lambda b: (b, 0))
    new_kv = pl.BlockSpec((None, t_new * N_HEADS, w), lambda b: (layer, b, 0))
    old = pl.BlockSpec((None, None, rows, w), lambda b: (layer, b, 0, 0))
    return pl.pallas_call(
        functools.partial(_attn_sample_body, t_new=t_new, past=past, out_scale=out_scale),
        grid=(batch,),
        in_specs=[pl.BlockSpec(memory_space=pltpu.SMEM), new, new_kv, new_kv, old, old, _const_spec((1, w))],
        out_specs=new,
        out_shape=jax.ShapeDtypeStruct((batch * t_new, ATT_W), F32),
        compiler_params=_cparams(("parallel",)),
        name="attn_sample",
    )(lam, bq, bk, bv, cache_k.reshape(depth, batch, rows, w), cache_v.reshape(depth, batch, rows, w),
      dn_g.reshape(1, w))


def _to_head_pairs(s):
    b = s.shape[0]
    r = s.reshape(b, N_HEADS // 2, 2, HEAD_DIM, HEAD_DIM)
    return jnp.swapaxes(r, 2, 3).reshape(b, N_HEADS // 2, HEAD_DIM, 2 * HEAD_DIM)


def _from_head_pairs(p):
    b = p.shape[0]
    r = p.reshape(b, N_HEADS // 2, HEAD_DIM, 2, HEAD_DIM)
    return jnp.swapaxes(r, 2, 3).reshape(b, N_HEADS, HEAD_DIM, HEAD_DIM)


def _rearrange_proj(a):
    gates = lambda c0: jnp.repeat(a[..., c0:c0 + N_HEADS], HEAD_DIM, axis=-1)
    out = jnp.concatenate([a[..., A_QK0:A_I0], a[..., A_O0:B_Q0], gates(A_I0), gates(A_F0), a[..., B_Q0:N_IN]],
                          axis=-1)
    assert out.shape[-1] == N_CAT
    return out


def _consts(hgrn_len, mlstm_len):
    r = jnp.arange(mlstm_len)
    tri = (r[None, :] <= r[:, None]).astype(BF16)
    r = jnp.arange(hgrn_len)
    tri16 = ((r[None, :] <= r[:, None]) & (r[None, :] // SUB == r[:, None] // SUB)).astype(BF16)
    hh = jnp.arange(HB_W) // HEAD_DIM
    eblk = (hh[None, :] == hh[:, None]).astype(BF16)
    lane = jnp.arange(2 * HEAD_DIM)
    src = jnp.arange(SUB)
    esel = ((hh[None, :, None] * SUB + src[:, None, None]) == lane[None, None, :]).astype(BF16)
    erep = ((lane[None, :, None] // SUB == jnp.arange(N_HEADS)[:, None, None])
            & (lane[None, :, None] % SUB == r[None, None, :] % SUB)).astype(BF16)
    return tri, tri16, eblk, esel, erep


def _group(x, depth, layer_w, states, cache, batch, t_len, alpha, consts):
    tri, tri16, eblk, esel, erep = consts
    prompt = cache is None
    outs = []
    kv_bufs = None
    for l in range(depth):
        w = layer_w[l]
        conv0, c0, n0, m0, s0 = states(l)
        x = _ffn_ln(x, w["f_in0"], w["f_out0"], w["ln_g"][0], w["ln_b"][0], alpha)
        res = _in_proj(x, w["w_cat"], w["b_cat"], prompt, l, depth, kv_bufs)
        proj = dict(zip(_PROJ_OUTS, res[:len(_PROJ_OUTS)]))
        bkb = res[len(_PROJ_OUTS)]
        kv_bufs = (proj["bk"], proj["bv"])

        conv0_p = jnp.pad(conv0, ((0, 0), (CONV_PAD - (CONV_W - 1), 0), (0, 0)))
        ya, c1, n1, m1, convn = _mlstm(proj, w["conv_w"], w["conv_b"], conv0_p, _to_head_pairs(c0),
                                       n0.reshape(batch, 1, HB_W),
                                       jnp.repeat(m0, HEAD_DIM, axis=-1).reshape(batch, 1, HB_W),
                                       tri, eblk, batch, t_len)
        lam_init = 0.8 - 0.6 * math.exp(-0.3 * l)
        if prompt:
            yb = _attn_prompt(w["lam"], proj["bq"], bkb, res[len(_PROJ_OUTS) + 1], w["dn_g"], batch, t_len,
                              1.0 - lam_init)
        else:
            yb = _attn_sample(w["lam"], proj["bq"], proj["bk"], proj["bv"], cache[0], cache[1], l, w["dn_g"],
                              batch, t_len, 1.0 - lam_init)
        yc, s1 = _hgrn(proj["c"], w["lb"], w["hn_g"], _to_head_pairs(s0), tri16, eblk, esel, erep, batch, t_len)
        x = _out_proj_ffn(ya, yb, yc, x, w["w_out"], w["ln_g"][1], w["ln_b"][1],
                          w["f_in1"], w["f_out1"], w["ln_g"][2], w["ln_b"][2], alpha)
        outs.append((_from_head_pairs(c1),
                     n1.reshape(batch, N_HEADS, HEAD_DIM),
                     m1.reshape(batch, N_HEADS, HEAD_DIM)[:, :, 0],
                     convn[:, CONV_PAD - (CONV_W - 1):, :],
                     _from_head_pairs(s1)))
    kv = tuple(buf.reshape(depth, batch, t_len, N_HEADS, 2 * HEAD_DIM) for buf in kv_bufs)
    return x, kv + tuple(jnp.stack(a) for a in zip(*outs))


def kernel(x_prompt, x_sample, cache_diff_k, cache_diff_v, state_mlstm_c, state_mlstm_n, state_mlstm_m,
           state_mlstm_conv, state_hgrn_s, w_in, b_in, w_out, mlstm_conv_w, mlstm_conv_b, diff_lambda,
           diff_norm_g, hgrn_lb_logits, hgrn_norm_g, ffn_w_in, ffn_w_out, ln_g, ln_b):
    depth = w_in.shape[0]
    alpha = (2.0 * depth) ** 0.25
    batch, seq, d_model = x_prompt.shape
    dec_batch, dec_seq, _ = x_sample.shape
    past = cache_diff_k.shape[2]

    p_lb = jax.nn.softmax(hgrn_lb_logits.astype(F32), axis=0)
    lower_bounds = jnp.cumsum(p_lb, axis=0) - p_lb[0]
    w_cat = _rearrange_proj(w_in.astype(BF16))
    b_cat = _rearrange_proj(b_in).reshape(depth, 1, N_CAT)
    lp = diff_lambda.astype(F32)
    lam_init = jnp.asarray([0.8 - 0.6 * math.exp(-0.3 * l) for l in range(depth)], F32)
    lam = (jnp.exp(jnp.sum(lp[:, 0] * lp[:, 1], axis=-1)) - jnp.exp(jnp.sum(lp[:, 2] * lp[:, 3], axis=-1))
           + lam_init)
    f_in = ffn_w_in.astype(BF16)
    f_out = ffn_w_out.astype(BF16)
    w_out_b = w_out.astype(BF16)
    layer_w = [dict(w_cat=w_cat[l], b_cat=b_cat[l],
                    conv_w=mlstm_conv_w[l], conv_b=mlstm_conv_b[l].reshape(1, 2 * HB_W),
                    lam=lam[l].reshape(1, 1), dn_g=diff_norm_g[l],
                    lb=lower_bounds[l].reshape(1, HB_W),
                    hn_g=jnp.tile(hgrn_norm_g[l], N_HEADS).reshape(1, HB_W),
                    w_out=w_out_b[l], f_in0=f_in[l, 0], f_out0=f_out[l, 0], f_in1=f_in[l, 1], f_out1=f_out[l, 1],
                    ln_g=ln_g[l], ln_b=ln_b[l]) for l in range(depth)]
    def zero_states(_):
        return (jnp.zeros((batch, CONV_W - 1, 2 * HB_W), F32), jnp.zeros((batch, N_HEADS, HEAD_DIM, HEAD_DIM), F32),
                jnp.zeros((batch, N_HEADS, HEAD_DIM), F32), jnp.zeros((batch, N_HEADS), F32),
                jnp.zeros((batch, N_HEADS, HEAD_DIM, HEAD_DIM), F32))

    def carried_states(l):
        return (state_mlstm_conv[l], state_mlstm_c[l], state_mlstm_n[l], state_mlstm_m[l], state_hgrn_s[l])

    y_p, p_out = _group(x_prompt.reshape(batch * seq, d_model), depth, layer_w, zero_states, None,
                        batch, seq, alpha, _consts(_hgrn_rows(seq)[1], _mlstm_rows(seq)[1]))
    cache = (cache_diff_k, cache_diff_v)
    y_s, s_out = _group(x_sample.reshape(dec_batch * dec_seq, d_model), depth, layer_w, carried_states, cache,
                        dec_batch, dec_seq, alpha, _consts(_hgrn_rows(dec_seq)[1], _mlstm_rows(dec_seq)[1]))
    return (y_p.reshape(batch, seq, d_model), y_s.reshape(dec_batch, dec_seq, d_model)) + p_out + s_out
```

```python
import functools
import math

import jax
import jax.numpy as jnp
from jax import lax
from jax.experimental import pallas as pl
from jax.experimental.pallas import tpu as pltpu

F32 = jnp.float32
BF16 = jnp.bfloat16

HEAD_DIM = 64
N_HEADS = 4
CONV_W = 4
CONV_PAD = 8
CHUNK = 64
HB_W = N_HEADS * HEAD_DIM
ATT_W = N_HEADS * 2 * HEAD_DIM
LN_EPS = 1e-5
RMS_EPS = 1e-6
NEG_BIG = -1e30
SCAN_ROWS = 128
MLSTM_ROWS = 128
HGRN_ROWS = 256
SUB = 16
LOG2E = 1.4426950408889634
VT_ROWS = 2 * HEAD_DIM + 16
Q_STRIP = 256
ATT_TILE = 2048
FFN_CHUNK = 256
PROJ_TILE = 512
FFN_TILE = 1024
FFN_ROWS = 512
V7X_VMEM_LIMIT = 56 * 1024 * 1024

A_QK0 = 0
A_V0 = A_QK0 + 2 * HB_W
A_I0 = A_V0 + HB_W
A_F0 = A_I0 + N_HEADS
A_O0 = A_F0 + N_HEADS
B_Q0 = A_O0 + HB_W
B_K0 = B_Q0 + ATT_W
B_V0 = B_K0 + ATT_W
C_Q0 = B_V0 + ATT_W
N_IN = C_Q0 + 4 * HB_W

_SEGS = {}
_off = 0
for _name, _w in (("aqk", 2 * HB_W), ("av", HB_W), ("ao", HB_W), ("gi", HB_W), ("gf", HB_W),
                  ("bq", ATT_W), ("bk", ATT_W), ("bv", ATT_W), ("c", 4 * HB_W)):
    _SEGS[_name] = (_off, _w)
    _off += _w
N_CAT = _off


def _tile(n, pref, mult=8):
    t = min(pref, n)
    while t > mult and (n % t or t % mult):
        t -= mult
    assert n % t == 0 and t % mult == 0, (n, pref, mult)
    return t


def _cparams(sem):
    return pltpu.CompilerParams(dimension_semantics=sem, vmem_limit_bytes=V7X_VMEM_LIMIT)


def _const_spec(shape, single=False):
    nd = len(shape)
    if single:
        return pl.BlockSpec(shape, lambda *_: (0,) * nd, pipeline_mode=pl.Buffered(1))
    return pl.BlockSpec(shape, lambda *_: (0,) * nd)


def _dot(a, b):
    return jnp.dot(a, b, preferred_element_type=F32)


def _dot_nt(a, b):
    return lax.dot_general(a, b, (((1,), (1,)), ((), ())), preferred_element_type=F32)


def _split3(x):
    a = x.astype(BF16)
    r = x - a.astype(F32)
    b = r.astype(BF16)
    c = (r - b.astype(F32)).astype(BF16)
    return a, b, c


def _dot_exact_left(mat01, x):
    a, b, c = _split3(x)
    return _dot(mat01, a) + _dot(mat01, b) + _dot(mat01, c)


def _dot_exact_right(x, mat01):
    a, b, c = _split3(x)
    return _dot(a, mat01) + _dot(b, mat01) + _dot(c, mat01)


def _layer_norm(y, g, b):
    mu = jnp.mean(y, axis=-1, keepdims=True)
    d = y - mu
    var = jnp.mean(d * d, axis=-1, keepdims=True)
    return d * lax.rsqrt(var + LN_EPS) * g + b


def _pad_rows(x, rows):
    if x.shape[0] == rows:
        return x
    return jnp.concatenate([x, jnp.zeros((rows - x.shape[0],) + x.shape[1:], x.dtype)], axis=0)


def _swiglu_ln(x, wi_ref, wo_ref, g_ref, b_ref, o_ref, acc_ref, alpha, d_ff, tf):
    rb = min(FFN_ROWS, x.shape[0])
    n_blocks = x.shape[0] // rb

    def finish(r):
        rows = slice(r * rb, (r + 1) * rb)
        y = alpha * x[rows] + 0.5 * acc_ref[rows, :]
        o_ref[rows, :] = _layer_norm(y, g_ref[...], b_ref[...])

    for r in range(n_blocks):
        rows = slice(r * rb, (r + 1) * rb)
        xb = x[rows].astype(BF16)
        for c in range(d_ff // tf):
            gate = _dot(xb, wi_ref[:, c * tf:(c + 1) * tf])
            up = _dot(xb, wi_ref[:, d_ff + c * tf:d_ff + (c + 1) * tf])
            h = (gate * jax.nn.sigmoid(gate) * up).astype(BF16)
            part = _dot(h, wo_ref[c * tf:(c + 1) * tf, :])
            if c == 0:
                acc_ref[rows, :] = part
            else:
                acc_ref[rows, :] += part
            if c == 0 and r > 0:
                finish(r - 1)
    finish(n_blocks - 1)


def _ffn_ln_body(x_ref, wi_ref, wo_ref, g_ref, b_ref, o_ref, acc_ref, *, alpha, d_ff, tf):
    _swiglu_ln(x_ref[...], wi_ref, wo_ref, g_ref, b_ref, o_ref, acc_ref, alpha, d_ff, tf)


def _ffn_ln(x, wi, wo, g, b, alpha):
    n, d = x.shape
    d_ff = wo.shape[0]
    tm = _tile(n, FFN_TILE)
    tf = _tile(d_ff, FFN_CHUNK, 128)
    return pl.pallas_call(
        functools.partial(_ffn_ln_body, alpha=alpha, d_ff=d_ff, tf=tf),
        grid=(n // tm,),
        in_specs=[pl.BlockSpec((tm, d), lambda i: (i, 0)),
                  _const_spec(wi.shape, single=True), _const_spec(wo.shape, single=True),
                  _const_spec((1, d)), _const_spec((1, d))],
        out_specs=pl.BlockSpec((tm, d), lambda i: (i, 0)),
        out_shape=jax.ShapeDtypeStruct((n, d), F32),
        scratch_shapes=[pltpu.VMEM((tm, d), F32)],
        compiler_params=_cparams(("parallel",)),
        name="ffn_ln",
    )(x, wi, wo, g.reshape(1, d), b.reshape(1, d))


_PROJ_OUTS = ("aqk", "av", "ao", "gi", "gf", "bq", "bk", "bv", "c")


def _in_proj_body(*refs, emit_vt, n_alias):
    x_ref, w_ref, bias_ref = refs[:3]
    out_refs = refs[3 + n_alias:]
    xb = x_ref[...].astype(BF16)
    tm = xb.shape[0]
    outs = dict(zip(_PROJ_OUTS, out_refs))
    dv = 2 * HEAD_DIM
    ones_rows = jnp.where(lax.broadcasted_iota(jnp.int32, (VT_ROWS - dv, Q_STRIP), 0) == 0, 1.0, 0.0).astype(BF16)
    for name in _PROJ_OUTS:
        c0, w = _SEGS[name]
        for s in range(0, w, 2 * HB_W):
            e = min(s + 2 * HB_W, w)
            val = _dot(xb, w_ref[:, c0 + s:c0 + e]) + bias_ref[:, c0 + s:c0 + e]
            if name in ("bk", "bv"):
                kv = outs[name]
                if n_alias == 0:
                    if kv.shape[0] > 1:
                        kv[1:] = jnp.zeros((kv.shape[0] - 1,) + kv.shape[1:], F32)
                    kv = kv.at[0]
                for h in range(N_HEADS):
                    kv[pl.ds(h, tm, stride=N_HEADS), :] = val[:, h * 2 * HEAD_DIM:(h + 1) * 2 * HEAD_DIM]
            else:
                outs[name][:, s:e] = val
            if name == "bk":
                out_refs[len(_PROJ_OUTS)][:, s:e] = val.astype(BF16)
            if name == "bv" and emit_vt:
                vt_ref = out_refs[len(_PROJ_OUTS) + 1]
                for h in range(N_HEADS):
                    vt = val[:, h * dv:(h + 1) * dv].T.astype(BF16)
                    for j in range(tm // Q_STRIP):
                        vt_ref[h, j, 0:dv, :] = vt[:, j * Q_STRIP:(j + 1) * Q_STRIP]
                        vt_ref[h, j, dv:VT_ROWS, :] = ones_rows


def _in_proj(x, w_cat, b_cat, emit_vt, layer, depth, kv_bufs):
    n, d = x.shape
    tm = _tile(n, PROJ_TILE, Q_STRIP if emit_vt else 8)
    out_shape, out_specs = [], []
    for k in _PROJ_OUTS:
        if k in ("bk", "bv"):
            out_shape.append(jax.ShapeDtypeStruct((depth, n * N_HEADS, 2 * HEAD_DIM), F32))
            if kv_bufs is None:
                out_specs.append(pl.BlockSpec((depth, tm * N_HEADS, 2 * HEAD_DIM), lambda i: (0, i, 0)))
            else:
                out_specs.append(pl.BlockSpec((None, tm * N_HEADS, 2 * HEAD_DIM), lambda i: (layer, i, 0)))
        else:
            out_shape.append(jax.ShapeDtypeStruct((n, _SEGS[k][1]), F32))
            out_specs.append(pl.BlockSpec((tm, _SEGS[k][1]), lambda i: (i, 0)))
    out_shape.append(jax.ShapeDtypeStruct((n, ATT_W), BF16))
    out_specs.append(pl.BlockSpec((tm, ATT_W), lambda i: (i, 0)))
    if emit_vt:
        out_shape.append(jax.ShapeDtypeStruct((N_HEADS, n // Q_STRIP, VT_ROWS, Q_STRIP), BF16))
        out_specs.append(pl.BlockSpec((N_HEADS, tm // Q_STRIP, VT_ROWS, Q_STRIP), lambda i: (0, i, 0, 0)))
    in_specs = [pl.BlockSpec((tm, d), lambda i: (i, 0)), _const_spec(w_cat.shape, single=True),
                _const_spec(b_cat.shape)]
    args = [x, w_cat, b_cat]
    aliases = {}
    for buf, name in zip(kv_bufs or (), ("bk", "bv")):
        aliases[len(args)] = _PROJ_OUTS.index(name)
        in_specs.append(pl.BlockSpec(memory_space=pl.ANY))
        args.append(buf)
    return pl.pallas_call(
        functools.partial(_in_proj_body, emit_vt=emit_vt, n_alias=len(aliases)),
        grid=(n // tm,),
        in_specs=in_specs,
        out_specs=out_specs,
        out_shape=out_shape,
        input_output_aliases=aliases,
        compiler_params=_cparams(("parallel",)),
        name="in_proj",
    )(*args)


def _out_proj_ffn_body(ya_ref, yb_ref, yc_ref, x_ref, w_ref, g1_ref, b1_ref, wi_ref, wo_ref, g2_ref, b2_ref,
                       o_ref, acc_ref, *, alpha, d_ff, tf):
    y = _dot(ya_ref[...].astype(BF16), w_ref[0:HB_W, :])
    y += _dot(yb_ref[...].astype(BF16), w_ref[HB_W:HB_W + ATT_W, :])
    y += _dot(yc_ref[...].astype(BF16), w_ref[HB_W + ATT_W:, :])
    x = _layer_norm(alpha * x_ref[...] + y, g1_ref[...], b1_ref[...])
    _swiglu_ln(x, wi_ref, wo_ref, g2_ref, b2_ref, o_ref, acc_ref, alpha, d_ff, tf)


def _out_proj_ffn(ya, yb, yc, x, w, g1, b1, wi, wo, g2, b2, alpha):
    n, d = x.shape
    d_ff = wo.shape[0]
    tm = _tile(n, FFN_TILE)
    tf = _tile(d_ff, FFN_CHUNK, 128)
    row = lambda width: pl.BlockSpec((tm, width), lambda i: (i, 0))
    vec = _const_spec((1, d))
    return pl.pallas_call(
        functools.partial(_out_proj_ffn_body, alpha=alpha, d_ff=d_ff, tf=tf),
        grid=(n // tm,),
        in_specs=[row(HB_W), row(ATT_W), row(HB_W), row(d), _const_spec(w.shape, single=True), vec, vec,
                  _const_spec(wi.shape, single=True), _const_spec(wo.shape, single=True), vec, vec],
        out_specs=row(d),
        out_shape=jax.ShapeDtypeStruct((n, d), F32),
        scratch_shapes=[pltpu.VMEM((tm, d), F32)],
        compiler_params=_cparams(("parallel",)),
        name="out_proj_ffn",
    )(ya, yb, yc, x, w, g1.reshape(1, d), b1.reshape(1, d), wi, wo, g2.reshape(1, d), b2.reshape(1, d))


def _mlstm_body(qk_ref, v_ref, o_ref, gi_ref, gf_ref, cw_ref, cb_ref, conv0_ref, c0_ref, n0_ref, m0_ref,
                tri_ref, eblk_ref,
                ya_ref, c1_ref, n1_ref, m1_ref, convn_ref,
                up_ref, c_ref, n_ref, m_ref, *, rows):
    step = pl.program_id(1)
    low = lax.broadcasted_iota(jnp.int32, (1, 2 * HEAD_DIM), 1) < HEAD_DIM
    pw = 2 * HEAD_DIM
    L = tri_ref.shape[0]

    @pl.when(step == 0)
    def _():
        up_ref[0:CONV_PAD, :] = conv0_ref[...]
        c_ref[...] = jnp.zeros(c_ref.shape, F32)
        for i in range(N_HEADS // 2):
            pair = c0_ref[i]
            c_ref[i * pw:i * pw + HEAD_DIM, i * pw:(i + 1) * pw] = jnp.where(low, pair, 0.0)
            c_ref[i * pw + HEAD_DIM:(i + 1) * pw, i * pw:(i + 1) * pw] = jnp.where(low, 0.0, pair)
        n_ref[...] = n0_ref[...]
        m_ref[...] = m0_ref[...]

    gi = _pad_rows(gi_ref[...], L)
    flog = jax.nn.log_sigmoid(_pad_rows(gf_ref[...], L))
    if rows < L:
        live = lax.broadcasted_iota(jnp.int32, (L, 1), 0) < rows
        gi = jnp.where(live, gi, NEG_BIG)
        flog = jnp.where(live, flog, 0.0)
    b = _dot_exact_left(tri_ref[...], flog)
    b_t = b.T
    gi_t = gi.T

    u = _pad_rows(qk_ref[...], L)
    up_ref[CONV_PAD:CONV_PAD + L, :] = u
    y = cb_ref[...]
    first = CONV_PAD - (CONV_W - 1)
    for j in range(CONV_W):
        y = y + cw_ref[j:j + 1, :] * up_ref[first + j:first + j + L, :]
    convn_ref[...] = up_ref[rows:rows + CONV_PAD, :]
    up_ref[0:CONV_PAD, :] = up_ref[L:L + CONV_PAD, :]
    qk = y * jax.nn.sigmoid(y)
    q = qk[:, :HB_W] * (HEAD_DIM ** -0.5)
    k = qk[:, HB_W:]
    v = _pad_rows(v_ref[...], L)
    kb = k.astype(BF16)
    vb = v.astype(BF16)
    qb = q.astype(BF16)

    lane_head = lax.broadcasted_iota(jnp.int32, (1, HB_W), 1) // HEAD_DIM
    qk_scores = [_dot_nt(jnp.where(lane_head == h, q, 0.0).astype(BF16), kb) for h in range(N_HEADS)]
    c_prev = c_ref[...]
    n_prev = n_ref[...]
    eblk = eblk_ref[...]
    num_state = _dot(qb, c_prev.astype(BF16))
    den_state = _dot((q * n_prev).astype(BF16), eblk)

    m_prev = m_ref[...]
    g = b + m_prev

    t_idx = lax.broadcasted_iota(jnp.int32, (L, L), 0)
    s_idx = lax.broadcasted_iota(jnp.int32, (L, L), 1)
    causal = s_idx <= t_idx

    mrow_hb = jnp.zeros((L, HB_W), F32)
    wg_hb = jnp.zeros((L, HB_W), F32)
    den_hb = jnp.zeros((L, HB_W), F32)
    num_hb = jnp.zeros((L, HB_W), F32)
    for h in range(N_HEADS):
        sel = lane_head == h
        c0 = h * HEAD_DIM
        dmat = b[:, c0:c0 + 1] - b_t[c0:c0 + 1, :] + gi_t[c0:c0 + 1, :]
        dmat = jnp.where(causal, dmat, NEG_BIG)
        gcol = g[:, c0:c0 + 1]
        mrow = jnp.maximum(gcol, jnp.max(dmat, axis=1, keepdims=True))
        wd = jnp.exp(dmat - mrow)
        wg = jnp.exp(gcol - mrow)
        qkw = qk_scores[h] * wd
        den = jnp.sum(qkw, axis=1, keepdims=True)
        num = _dot(qkw.astype(BF16), vb)
        mrow_hb = jnp.where(sel, mrow, mrow_hb)
        wg_hb = jnp.where(sel, wg, wg_hb)
        den_hb = jnp.where(sel, den, den_hb)
        num_hb = jnp.where(sel, num, num_hb)

    num_hb = wg_hb * num_state + num_hb
    den_hb = wg_hb * den_state + den_hb
    hout = num_hb / jnp.maximum(jnp.abs(den_hb), jnp.exp(-mrow_hb))
    ya = jax.nn.sigmoid(_pad_rows(o_ref[...], L)) * hout
    ya_ref[...] = ya[:rows]

    m_new = mrow_hb[L - 1:L, :]
    b_last = b[L - 1:L, :]
    decay = jnp.exp(b_last + m_prev - m_new)
    ws = jnp.exp(b_last - b + gi - m_new)
    kw = k * ws
    upd = _dot(kw.T.astype(BF16), vb) * eblk.astype(F32)
    c_new = decay * c_prev + upd
    n_new = decay * n_prev + jnp.sum(kw, axis=0, keepdims=True)
    c_ref[...] = c_new
    n_ref[...] = n_new
    m_ref[...] = m_new

    @pl.when(step == pl.num_programs(1) - 1)
    def _():
        for i in range(N_HEADS // 2):
            c1_ref[i] = jnp.where(low, c_new[i * pw:i * pw + HEAD_DIM, i * pw:(i + 1) * pw],
                                  c_new[i * pw + HEAD_DIM:(i + 1) * pw, i * pw:(i + 1) * pw])
        n1_ref[...] = n_new
        m1_ref[...] = m_new


def _mlstm_rows(t_len):
    rows = _tile(t_len, MLSTM_ROWS, 8)
    return rows, max(rows, SCAN_ROWS)


def _mlstm(proj, conv_w, conv_b, conv0, c0_pk, n0, m0_hb, tri, eblk, batch, t_len):
    rows, length = _mlstm_rows(t_len)
    assert tri.shape == (length, length)
    steps = t_len // rows
    tok = lambda width: pl.BlockSpec((rows, width), lambda bi, ci: (bi * steps + ci, 0))
    per_b = lambda shape: pl.BlockSpec((None,) + shape, lambda bi, ci: (bi,) + (0,) * len(shape))
    return pl.pallas_call(
        functools.partial(_mlstm_body, rows=rows),
        grid=(batch, steps),
        in_specs=[tok(2 * HB_W), tok(HB_W), tok(HB_W), tok(HB_W), tok(HB_W),
                  _const_spec(conv_w.shape), _const_spec(conv_b.shape),
                  per_b((CONV_PAD, 2 * HB_W)), per_b((N_HEADS // 2, HEAD_DIM, 2 * HEAD_DIM)), per_b((1, HB_W)),
                  per_b((1, HB_W)), _const_spec(tri.shape), _const_spec(eblk.shape)],
        out_specs=[tok(HB_W), per_b((N_HEADS // 2, HEAD_DIM, 2 * HEAD_DIM)), per_b((1, HB_W)), per_b((1, HB_W)),
                   per_b((CONV_PAD, 2 * HB_W))],
        out_shape=[jax.ShapeDtypeStruct((batch * t_len, HB_W), F32),
                   jax.ShapeDtypeStruct((batch, N_HEADS // 2, HEAD_DIM, 2 * HEAD_DIM), F32),
                   jax.ShapeDtypeStruct((batch, 1, HB_W), F32),
                   jax.ShapeDtypeStruct((batch, 1, HB_W), F32),
                   jax.ShapeDtypeStruct((batch, CONV_PAD, 2 * HB_W), F32)],
        scratch_shapes=[pltpu.VMEM((length + CONV_PAD, 2 * HB_W), F32),
                        pltpu.VMEM((HB_W, HB_W), F32),
                        pltpu.VMEM((1, HB_W), F32),
                        pltpu.VMEM((1, HB_W), F32)],
        compiler_params=_cparams(("parallel", "arbitrary")),
        name="mlstm",
    )(proj["aqk"], proj["av"], proj["ao"], proj["gi"], proj["gf"], conv_w, conv_b, conv0, c0_pk, n0, m0_hb,
      tri, eblk)


def _hgrn_body(c_ref, lb_ref, hng_ref, s0_ref, tri16_ref, eblk_ref, esel_ref, erep_ref, yc_ref, s1_ref, s_ref,
               *, rows, length, seqs):
    step = pl.program_id(1)
    L = length

    @pl.when(step == 0)
    def _():
        s_ref[...] = s0_ref[...]

    cin = _pad_rows(c_ref[...], L)
    q = cin[:, 0:HB_W]
    f_pre = cin[:, HB_W:2 * HB_W]
    v = cin[:, 2 * HB_W:3 * HB_W]
    gate = cin[:, 3 * HB_W:]
    lb = lb_ref[...]
    forget = lb + (1.0 - lb) * jax.nn.sigmoid(f_pre)
    logf = jnp.log(forget)
    key = 1.0 - forget
    if rows < L:
        live = lax.broadcasted_iota(jnp.int32, (L, 1), 0) < rows
        logf = jnp.where(live, logf, 0.0)
        key = jnp.where(live, key, 0.0)
    bl = _dot_exact_left(tri16_ref[...], logf) * LOG2E
    eblk = eblk_ref[...]
    groups = L // SUB
    t_loc = lax.broadcasted_iota(jnp.int32, (L, 1), 0) % SUB
    vb = v.astype(BF16)

    def from_source(x, s):
        picked = x.reshape(groups, SUB, HB_W)[:, s:s + 1, :]
        return jnp.broadcast_to(picked, (groups, SUB, HB_W)).reshape(L, HB_W)

    n_sub = max(rows // SUB, 1)
    qd = (q * jnp.exp2(bl)).astype(BF16)
    bl_t = bl.T
    low = lax.broadcasted_iota(jnp.int32, (1, 2 * HEAD_DIM), 1) < HEAD_DIM
    zero_b = jnp.zeros((HEAD_DIM, 2 * HEAD_DIM), BF16)

    def contribution(j):
        r0 = j * SUB
        b_last = bl[r0 + SUB - 1:r0 + SUB, :]
        kw_j = (key[r0:r0 + SUB] * jnp.exp2(b_last - bl[r0:r0 + SUB])).astype(BF16)
        upd = lax.dot_general(kw_j, vb[r0:r0 + SUB], (((0,), (0,)), ((), ())),
                              preferred_element_type=F32)
        dec = jnp.exp2(bl_t[:, r0 + SUB - 1:r0 + SUB])
        pair = []
        for i in range(2):
            rk = 2 * i * HEAD_DIM
            u = jnp.where(low, upd[rk:rk + HEAD_DIM, rk:rk + 2 * HEAD_DIM],
                          upd[rk + HEAD_DIM:rk + 2 * HEAD_DIM, rk:rk + 2 * HEAD_DIM])
            d = jnp.where(low, dec[rk:rk + HEAD_DIM], dec[rk + HEAD_DIM:rk + 2 * HEAD_DIM])
            pair.append((d, u))
        return pair

    per_seq = n_sub // seqs
    states = [[s_ref[q, 0], s_ref[q, 1]] for q in range(seqs)]
    o_parts = []

    def advance(j):
        r0 = j * SUB
        pk = states[j // per_seq]
        pb = [x.astype(BF16) for x in pk]
        s_bd = jnp.concatenate([
            jnp.concatenate([jnp.where(low, pb[0], 0.0).astype(BF16), zero_b], axis=1),
            jnp.concatenate([jnp.where(low, 0.0, pb[0]).astype(BF16), zero_b], axis=1),
            jnp.concatenate([zero_b, jnp.where(low, pb[1], 0.0).astype(BF16)], axis=1),
            jnp.concatenate([zero_b, jnp.where(low, 0.0, pb[1]).astype(BF16)], axis=1)], axis=0)
        o_parts.append(_dot(qd[r0:r0 + SUB], s_bd))
        pk[:] = [d * p + u for p, (d, u) in zip(pk, contrib[j])]

    contrib = []
    att = jnp.zeros((L, 2 * HEAD_DIM), F32)
    for s in range(SUB):
        w = jnp.exp2(bl - from_source(bl, s))
        p = jnp.where(t_loc >= s, q * from_source(key, s) * w, 0.0)
        att = att + _dot(p.astype(BF16), esel_ref[s])
        while len(contrib) < ((s + 1) * n_sub) // SUB:
            contrib.append(contribution(len(contrib)))
    for j in range(n_sub):
        advance(j)
    for q_i in range(seqs):
        s_ref[q_i, 0] = states[q_i][0]
        s_ref[q_i, 1] = states[q_i][1]
    attb = att.astype(BF16)
    same_sub = (lax.broadcasted_iota(jnp.int32, (L, L), 0) // SUB
                == lax.broadcasted_iota(jnp.int32, (L, L), 1) // SUB)
    lane_head = lax.broadcasted_iota(jnp.int32, (1, HB_W), 1) // HEAD_DIM
    o = jnp.zeros((L, HB_W), F32)
    for h in range(N_HEADS):
        full = jnp.where(same_sub, _dot(attb, erep_ref[h]), 0.0)
        o = jnp.where(lane_head == h, _dot(full.astype(BF16), vb), o)

    o_inter = jnp.concatenate(o_parts, axis=0)
    if rows < L:
        o_inter = _pad_rows(o_inter, L)
    o = o + o_inter

    ms = _dot_exact_right(o * o, eblk) * (1.0 / HEAD_DIM)
    yc = o * lax.rsqrt(ms + RMS_EPS) * hng_ref[...] * (gate * jax.nn.sigmoid(gate))
    yc_ref[...] = yc[:rows]

    @pl.when(step == pl.num_programs(1) - 1)
    def _():
        for q_i in range(seqs):
            s1_ref[q_i, 0] = states[q_i][0]
            s1_ref[q_i, 1] = states[q_i][1]


def _hgrn_rows(t_len, batch=1):
    if t_len < SCAN_ROWS and t_len % SUB == 0 and SCAN_ROWS % t_len == 0 and batch % (SCAN_ROWS // t_len) == 0:
        return SCAN_ROWS, SCAN_ROWS, SCAN_ROWS // t_len
    rows = _tile(t_len, HGRN_ROWS, SUB)
    return rows, max(rows, SCAN_ROWS), 1


def _hgrn(c_all, lb, hn_g, s0_pk, tri16, eblk, esel, erep, batch, t_len):
    rows, length, seqs = _hgrn_rows(t_len, batch)
    n = batch * t_len
    steps = n // (batch // seqs) // rows
    pk_spec = pl.BlockSpec((seqs, 2, HEAD_DIM, 2 * HEAD_DIM), lambda bi, ci: (bi, 0, 0, 0))
    return pl.pallas_call(
        functools.partial(_hgrn_body, rows=rows, length=length, seqs=seqs),
        grid=(batch // seqs, steps),
        in_specs=[pl.BlockSpec((rows, 4 * HB_W), lambda bi, ci: (bi * steps + ci, 0)),
                  _const_spec((1, HB_W)), _const_spec((1, HB_W)), pk_spec,
                  _const_spec(tri16.shape), _const_spec(eblk.shape), _const_spec(esel.shape),
                  _const_spec(erep.shape)],
        out_specs=[pl.BlockSpec((rows, HB_W), lambda bi, ci: (bi * steps + ci, 0)), pk_spec],
        out_shape=[jax.ShapeDtypeStruct((n, HB_W), F32),
                   jax.ShapeDtypeStruct((batch, 2, HEAD_DIM, 2 * HEAD_DIM), F32)],
        scratch_shapes=[pltpu.VMEM((seqs, 2, HEAD_DIM, 2 * HEAD_DIM), F32)],
        compiler_params=_cparams(("parallel", "arbitrary")),
        name="hgrn",
    )(c_all, lb, hn_g, s0_pk, tri16, eblk, esel, erep)


def _attn_prompt_body(qi_tab, ki_tab, lam_ref, q_ref, k_ref, vt_ref, g_ref, o_ref, qx_ref, m_ref, acc_ref,
                      s_ref, bmax_ref, *, tq, out_scale):
    pair = pl.program_id(2)
    qi = qi_tab[pair]
    ki = ki_tab[pair]
    ns = tq // Q_STRIP
    dv = 2 * HEAD_DIM

    @pl.when(ki == 0)
    def _():
        qt = (q_ref[...] * (HEAD_DIM ** -0.5 * LOG2E)).T
        first = lax.broadcasted_iota(jnp.int32, (dv, 1), 0) < HEAD_DIM
        qa = jnp.where(first, qt, 0.0).astype(BF16)
        qb = jnp.where(first, 0.0, qt).astype(BF16)
        for st in range(ns):
            qx_ref[0, st] = qa[:, st * Q_STRIP:(st + 1) * Q_STRIP]
            qx_ref[1, st] = qb[:, st * Q_STRIP:(st + 1) * Q_STRIP]
        m_ref[...] = jnp.full(m_ref.shape, NEG_BIG, F32)
        acc_ref[...] = jnp.zeros(acc_ref.shape, F32)

    def scores(kb, st, diag):
        k = k_ref[kb * Q_STRIP:(kb + 1) * Q_STRIP, :]
        for mp in range(2):
            s = _dot(k, qx_ref[mp, st])
            if diag and st == kb:
                k_chunk = lax.broadcasted_iota(jnp.int32, (Q_STRIP, 1), 0) // CHUNK
                q_chunk = lax.broadcasted_iota(jnp.int32, (1, Q_STRIP), 1) // CHUNK
                s = jnp.where(k_chunk <= q_chunk, s, NEG_BIG)
            s_ref[kb % 2, mp, st] = s
            bmax_ref[kb % 2, mp, st] = jnp.max(s, axis=0, keepdims=True)

    def accumulate(kb, st):
        vt = vt_ref[kb]
        for mp in range(2):
            m_old = m_ref[mp, st]
            m_new = jnp.maximum(m_old, bmax_ref[kb % 2, mp, st])
            p = jnp.exp2(s_ref[kb % 2, mp, st] - m_new).astype(BF16)
            corr = jnp.exp2(m_old - m_new)
            acc_ref[mp, st] = corr * acc_ref[mp, st] + _dot(vt, p)
            m_ref[mp, st] = m_new

    def key_tile(diag):
        first_strip = (lambda kb: kb) if diag else (lambda kb: 0)
        for st in range(ns):
            scores(0, st, diag)
        for kb in range(ns):
            for st in range(first_strip(kb), ns):
                accumulate(kb, st)
                if kb + 1 < ns and st >= first_strip(kb + 1):
                    scores(kb + 1, st, diag)

    @pl.when(ki < qi)
    def _():
        key_tile(False)

    @pl.when(ki == qi)
    def _():
        key_tile(True)
        lam = lam_ref[0, 0]
        for st in range(ns):
            a1 = acc_ref[0, st]
            a2 = acc_ref[1, st]
            o = a1[:dv] / a1[dv:dv + 1] - lam * (a2[:dv] / a2[dv:dv + 1])
            ms = jnp.mean(o * o, axis=0, keepdims=True)
            o = o * lax.rsqrt(ms + RMS_EPS) * g_ref[...] * out_scale
            o_ref[st * Q_STRIP:(st + 1) * Q_STRIP, :] = o.T


def _attn_prompt(lam, bq, bkb, vt, dn_g, batch, t_len, out_scale):
    tq = _tile(t_len, ATT_TILE, Q_STRIP)
    nq = t_len // tq
    ns = tq // Q_STRIP
    n = batch * t_len
    w = 2 * HEAD_DIM
    pairs = [(qi, ki) for qi in range(nq) for ki in range(qi + 1)]
    qi_tab = jnp.asarray([p[0] for p in pairs], jnp.int32)
    ki_tab = jnp.asarray([p[1] for p in pairs], jnp.int32)
    qmap = lambda b, h, p, qt, kt: (b * nq + qt[p], h)
    kmap = lambda b, h, p, qt, kt: (b * nq + kt[p], h)
    vmap_ = lambda b, h, p, qt, kt: (h, b * nq + kt[p], 0, 0)
    grid_spec = pltpu.PrefetchScalarGridSpec(
        num_scalar_prefetch=2,
        grid=(batch, N_HEADS, len(pairs)),
        in_specs=[pl.BlockSpec(memory_space=pltpu.SMEM),
                  pl.BlockSpec((tq, w), qmap), pl.BlockSpec((tq, w), kmap),
                  pl.BlockSpec((None, ns, VT_ROWS, Q_STRIP), vmap_),
                  pl.BlockSpec((w, 1), lambda b, h, p, qt, kt: (0, 0))],
        out_specs=pl.BlockSpec((tq, w), qmap),
        scratch_shapes=[pltpu.VMEM((2, ns, w, Q_STRIP), BF16),
                        pltpu.VMEM((2, ns, 1, Q_STRIP), F32),
                        pltpu.VMEM((2, ns, VT_ROWS, Q_STRIP), F32),
                        pltpu.VMEM((2, 2, ns, Q_STRIP, Q_STRIP), F32),
                        pltpu.VMEM((2, 2, ns, 1, Q_STRIP), F32)])
    return pl.pallas_call(
        functools.partial(_attn_prompt_body, tq=tq, out_scale=out_scale),
        grid_spec=grid_spec,
        out_shape=jax.ShapeDtypeStruct((n, ATT_W), F32),
        compiler_params=_cparams(("parallel", "parallel", "arbitrary")),
        name="attn_prompt",
    )(qi_tab, ki_tab, lam, bq, bkb, vt, dn_g.reshape(w, 1))


def _attn_sample_body(lam_ref, q_ref, kn_ref, vn_ref, kp_ref, vp_ref, g_ref, o_ref, *, t_new, past, out_scale):
    w = 2 * HEAD_DIM
    first = lax.broadcasted_iota(jnp.int32, (1, w), 1) < HEAD_DIM
    for h in range(N_HEADS):
        cols = slice(h * w, (h + 1) * w)
        q = q_ref[:, cols] * (HEAD_DIM ** -0.5)
        q2 = jnp.concatenate([jnp.where(first, q, 0.0), jnp.where(first, 0.0, q)], axis=0).astype(BF16)
        kp = kp_ref[pl.ds(h, past, stride=N_HEADS), :].astype(BF16)
        vp = vp_ref[pl.ds(h, past, stride=N_HEADS), :].astype(BF16)
        s_p = _dot_nt(q2, kp)
        kn = kn_ref[pl.ds(h, t_new, stride=N_HEADS), :].astype(BF16)
        vn = vn_ref[pl.ds(h, t_new, stride=N_HEADS), :].astype(BF16)
        s_n = _dot_nt(q2, kn)
        m = jnp.maximum(jnp.max(s_p, axis=1, keepdims=True), jnp.max(s_n, axis=1, keepdims=True))
        p_p = jnp.exp(s_p - m)
        p_n = jnp.exp(s_n - m)
        l = jnp.sum(p_p, axis=1, keepdims=True) + jnp.sum(p_n, axis=1, keepdims=True)
        acc = _dot(p_p.astype(BF16), vp) + _dot(p_n.astype(BF16), vn)
        o2 = acc / l
        o = o2[:t_new] - lam_ref[0, 0] * o2[t_new:]
        ms = jnp.mean(o * o, axis=1, keepdims=True)
        o_ref[:, cols] = o * lax.rsqrt(ms + RMS_EPS) * g_ref[...] * out_scale


def _attn_sample(lam, bq, bk, bv, cache_k, cache_v, layer, dn_g, batch, t_new, out_scale):
    w = 2 * HEAD_DIM
    depth, _, past = cache_k.shape[:3]
    rows = past * N_HEADS
    new = pl.BlockSpec((t_new, ATT_W), lambda b: (b, 0))
    new_kv = pl.BlockSpec((None, t_new * N_HEADS, w), lambda b: (layer, b, 0))
    old = pl.BlockSpec((None, None, rows, w), lambda b: (layer, b, 0, 0))
    return pl.pallas_call(
        functools.partial(_attn_sample_body, t_new=t_new, past=past, out_scale=out_scale),
        grid=(batch,),
        in_specs=[pl.BlockSpec(memory_space=pltpu.SMEM), new, new_kv, new_kv, old, old, _const_spec((1, w))],
        out_specs=new,
        out_shape=jax.ShapeDtypeStruct((batch * t_new, ATT_W), F32),
        compiler_params=_cparams(("parallel",)),
        name="attn_sample",
    )(lam, bq, bk, bv, cache_k.reshape(depth, batch, rows, w), cache_v.reshape(depth, batch, rows, w),
      dn_g.reshape(1, w))


def _to_head_pairs(s):
    b = s.shape[0]
    r = s.reshape(b, N_HEADS // 2, 2, HEAD_DIM, HEAD_DIM)
    return jnp.swapaxes(r, 2, 3).reshape(b, N_HEADS // 2, HEAD_DIM, 2 * HEAD_DIM)


def _from_head_pairs(p):
    b = p.shape[0]
    r = p.reshape(b, N_HEADS // 2, HEAD_DIM, 2, HEAD_DIM)
    return jnp.swapaxes(r, 2, 3).reshape(b, N_HEADS, HEAD_DIM, HEAD_DIM)


def _rearrange_proj(a):
    gates = lambda c0: jnp.repeat(a[..., c0:c0 + N_HEADS], HEAD_DIM, axis=-1)
    out = jnp.concatenate([a[..., A_QK0:A_I0], a[..., A_O0:B_Q0], gates(A_I0), gates(A_F0), a[..., B_Q0:N_IN]],
                          axis=-1)
    assert out.shape[-1] == N_CAT
    return out


def _consts(hgrn_len, mlstm_len):
    r = jnp.arange(mlstm_len)
    tri = (r[None, :] <= r[:, None]).astype(BF16)
    r = jnp.arange(hgrn_len)
    tri16 = ((r[None, :] <= r[:, None]) & (r[None, :] // SUB == r[:, None] // SUB)).astype(BF16)
    hh = jnp.arange(HB_W) // HEAD_DIM
    eblk = (hh[None, :] == hh[:, None]).astype(BF16)
    lane = jnp.arange(2 * HEAD_DIM)
    src = jnp.arange(SUB)
    esel = ((hh[None, :, None] * SUB + src[:, None, None]) == lane[None, None, :]).astype(BF16)
    erep = ((lane[None, :, None] // SUB == jnp.arange(N_HEADS)[:, None, None])
            & (lane[None, :, None] % SUB == r[None, None, :] % SUB)).astype(BF16)
    return tri, tri16, eblk, esel, erep


def _group(x, depth, layer_w, states, cache, batch, t_len, alpha, consts):
    tri, tri16, eblk, esel, erep = consts
    prompt = cache is None
    outs = []
    kv_bufs = None
    for l in range(depth):
        w = layer_w[l]
        conv0, c0, n0, m0, s0 = states(l)
        x = _ffn_ln(x, w["f_in0"], w["f_out0"], w["ln_g"][0], w["ln_b"][0], alpha)
        res = _in_proj(x, w["w_cat"], w["b_cat"], prompt, l, depth, kv_bufs)
        proj = dict(zip(_PROJ_OUTS, res[:len(_PROJ_OUTS)]))
        bkb = res[len(_PROJ_OUTS)]
        kv_bufs = (proj["bk"], proj["bv"])

        conv0_p = jnp.pad(conv0, ((0, 0), (CONV_PAD - (CONV_W - 1), 0), (0, 0)))
        ya, c1, n1, m1, convn = _mlstm(proj, w["conv_w"], w["conv_b"], conv0_p, _to_head_pairs(c0),
                                       n0.reshape(batch, 1, HB_W),
                                       jnp.repeat(m0, HEAD_DIM, axis=-1).reshape(batch, 1, HB_W),
                                       tri, eblk, batch, t_len)
        lam_init = 0.8 - 0.6 * math.exp(-0.3 * l)
        if prompt:
            yb = _attn_prompt(w["lam"], proj["bq"], bkb, res[len(_PROJ_OUTS) + 1], w["dn_g"], batch, t_len,
                              1.0 - lam_init)
        else:
            yb = _attn_sample(w["lam"], proj["bq"], proj["bk"], proj["bv"], cache[0], cache[1], l, w["dn_g"],
                              batch, t_len, 1.0 - lam_init)
        yc, s1 = _hgrn(proj["c"], w["lb"], w["hn_g"], _to_head_pairs(s0), tri16, eblk, esel, erep, batch, t_len)
        x = _out_proj_ffn(ya, yb, yc, x, w["w_out"], w["ln_g"][1], w["ln_b"][1],
                          w["f_in1"], w["f_out1"], w["ln_g"][2], w["ln_b"][2], alpha)
        outs.append((_from_head_pairs(c1),
                     n1.reshape(batch, N_HEADS, HEAD_DIM),
                     m1.reshape(batch, N_HEADS, HEAD_DIM)[:, :, 0],
                     convn[:, CONV_PAD - (CONV_W - 1):, :],
                     _from_head_pairs(s1)))
    kv = tuple(buf.reshape(depth, batch, t_len, N_HEADS, 2 * HEAD_DIM) for buf in kv_bufs)
    return x, kv + tuple(jnp.stack(a) for a in zip(*outs))


def kernel(x_prompt, x_sample, cache_diff_k, cache_diff_v, state_mlstm_c, state_mlstm_n, state_mlstm_m,
           state_mlstm_conv, state_hgrn_s, w_in, b_in, w_out, mlstm_conv_w, mlstm_conv_b, diff_lambda,
           diff_norm_g, hgrn_lb_logits, hgrn_norm_g, ffn_w_in, ffn_w_out, ln_g, ln_b):
    depth = w_in.shape[0]
    alpha = (2.0 * depth) ** 0.25
    batch, seq, d_model = x_prompt.shape
    dec_batch, dec_seq, _ = x_sample.shape
    past = cache_diff_k.shape[2]

    p_lb = jax.nn.softmax(hgrn_lb_logits.astype(F32), axis=0)
    lower_bounds = jnp.cumsum(p_lb, axis=0) - p_lb[0]
    w_cat = _rearrange_proj(w_in.astype(BF16))
    b_cat = _rearrange_proj(b_in).reshape(depth, 1, N_CAT)
    lp = diff_lambda.astype(F32)
    lam_init = jnp.asarray([0.8 - 0.6 * math.exp(-0.3 * l) for l in range(depth)], F32)
    lam = (jnp.exp(jnp.sum(lp[:, 0] * lp[:, 1], axis=-1)) - jnp.exp(jnp.sum(lp[:, 2] * lp[:, 3], axis=-1))
           + lam_init)
    f_in = ffn_w_in.astype(BF16)
    f_out = ffn_w_out.astype(BF16)
    w_out_b = w_out.astype(BF16)
    layer_w = [dict(w_cat=w_cat[l], b_cat=b_cat[l],
                    conv_w=mlstm_conv_w[l], conv_b=mlstm_conv_b[l].reshape(1, 2 * HB_W),
                    lam=lam[l].reshape(1, 1), dn_g=diff_norm_g[l],
                    lb=lower_bounds[l].reshape(1, HB_W),
                    hn_g=jnp.tile(hgrn_norm_g[l], N_HEADS).reshape(1, HB_W),
                    w_out=w_out_b[l], f_in0=f_in[l, 0], f_out0=f_out[l, 0], f_in1=f_in[l, 1], f_out1=f_out[l, 1],
                    ln_g=ln_g[l], ln_b=ln_b[l]) for l in range(depth)]
    def zero_states(_):
        return (jnp.zeros((batch, CONV_W - 1, 2 * HB_W), F32), jnp.zeros((batch, N_HEADS, HEAD_DIM, HEAD_DIM), F32),
                jnp.zeros((batch, N_HEADS, HEAD_DIM), F32), jnp.zeros((batch, N_HEADS), F32),
                jnp.zeros((batch, N_HEADS, HEAD_DIM, HEAD_DIM), F32))

    def carried_states(l):
        return (state_mlstm_conv[l], state_mlstm_c[l], state_mlstm_n[l], state_mlstm_m[l], state_hgrn_s[l])

    y_p, p_out = _group(x_prompt.reshape(batch * seq, d_model), depth, layer_w, zero_states, None,
                        batch, seq, alpha, _consts(_hgrn_rows(seq)[1], _mlstm_rows(seq)[1]))
    cache = (cache_diff_k, cache_diff_v)
    y_s, s_out = _group(x_sample.reshape(dec_batch * dec_seq, d_model), depth, layer_w, carried_states, cache,
                        dec_batch, dec_seq, alpha, _consts(_hgrn_rows(dec_seq)[1], _mlstm_rows(dec_seq)[1]))
    return (y_p.reshape(batch, seq, d_model), y_s.reshape(dec_batch, dec_seq, d_model)) + p_out + s_out
```
